```python
import math
import jax
import jax.numpy as jnp
from jax import lax
import numpy as np

D_MODEL = 1024
BATCH = 8
SEQ = 4096
DEPTH = 2

GRID_W = 64
CTX_LEN = 256
EPS = 1e-6
F32 = jnp.float32

D_MIX = D_MODEL
N_MIXERS = 4
D_GROUP = D_MIX // N_MIXERS

HY_ORDER = 2
HY_HEADS = 4
HY_SHORT = 3
HY_EMB = 33
HY_BANDS = (HY_EMB - 1) // 2
HY_FFN = 64
HY_TARGET = 1e-2
HY_FAST_PCT = 0.3
HY_SLOW_PCT = 1.5

GM_CHUNK = 128
GM_HEADS = 4

CV_WIDTH = 31
CV_GROUPS = 4

MLA_HEADS = 4
MLA_NOPE = 64
MLA_ROPE = 32
MLA_V = 64
MLA_Q_RANK = 192
MLA_KV_RANK = 128
ROPE_BASE = 10000.0
ATTN_BLOCK = 128

N_EXPERTS = 16
N_EXPERT_GROUPS = 4
EXPERTS_PER_GROUP = N_EXPERTS // N_EXPERT_GROUPS
TOP_K = 2
GROUP_SCORE_TOPK = 2
D_EXPERT = 512

HY_COLS = (HY_ORDER + 1) * D_GROUP
GM_COLS = 2 * D_GROUP
CV_COLS = 2 * D_GROUP
MQ_COLS = MLA_Q_RANK
MKV_COLS = MLA_KV_RANK + MLA_ROPE
HY_OFF = 0
GM_OFF = HY_OFF + HY_COLS
CV_OFF = GM_OFF + GM_COLS
MQ_OFF = CV_OFF + CV_COLS
MKV_OFF = MQ_OFF + MQ_COLS
IN_COLS = MKV_OFF + MKV_COLS

kernel_name = "hybrid_hymba_flow_block"


def rms_norm(x, g):
    x32 = x.astype(F32)
    y = x32 * lax.rsqrt(jnp.mean(x32 * x32, axis=-1, keepdims=True) + EPS)
    return (y * g.astype(F32)).astype(x.dtype)


def layer_norm(x, g, b):
    x32 = x.astype(F32)
    mu = jnp.mean(x32, axis=-1, keepdims=True)
    var = jnp.mean(jnp.square(x32 - mu), axis=-1, keepdims=True)
    return ((x32 - mu) * lax.rsqrt(var + EPS) * g.astype(F32) + b.astype(F32)).astype(x.dtype)


def depthwise_conv(x, w, b):
    pad = (w.shape[0] - 1) // 2
    y = lax.conv_general_dilated(x, w[:, None, :].astype(x.dtype), window_strides=(1,),
                                 padding=[(pad, pad)], dimension_numbers=('NWC', 'WIO', 'NWC'),
                                 feature_group_count=x.shape[-1])
    return y + b.astype(x.dtype)


def modulation(cond, w, b):
    m = jax.nn.silu(cond) @ w + b
    return jnp.split(jnp.expand_dims(m, -2), 6, axis=-1)


def axial_rope_tables(n_lat):
    rows = n_lat // GRID_W
    row = jnp.repeat(jnp.arange(rows), GRID_W).astype(F32)
    col = jnp.tile(jnp.arange(GRID_W), rows).astype(F32)
    n_freq = MLA_ROPE // 4
    inv = ROPE_BASE ** (-jnp.arange(n_freq, dtype=F32) / n_freq)
    ang = jnp.concatenate([row[:, None] * inv, col[:, None] * inv], axis=-1)
    return jnp.cos(ang), jnp.sin(ang)


def apply_rope(x, cos, sin):
    x32 = x.astype(F32)
    x1, x2 = jnp.split(x32, 2, axis=-1)
    return jnp.concatenate([x1 * cos - x2 * sin, x1 * sin + x2 * cos], axis=-1).astype(x.dtype)


def hyena_filter_spectra(L, w1, b1, freq, w2, b2, w3):
    w1, b1, freq, w2, b2, w3 = (a.astype(F32) for a in (w1, b1, freq, w2, b2, w3))
    t = jnp.linspace(0.0, 1.0, L, dtype=F32)[:, None]
    w = 2.0 * math.pi * jnp.arange(L, dtype=F32)[:, None] / L
    f = jnp.linspace(1e-4, HY_BANDS - 1, HY_BANDS, dtype=F32)[None, :]
    z = jnp.concatenate([t, jnp.cos(f * w), -jnp.sin(f * w)], axis=-1)
    h = jnp.sin(freq * (z @ w1 + b1))
    h = jnp.sin(freq * (h @ w2 + b2))
    h = (h @ w3).reshape(L, HY_ORDER, 2, D_GROUP)
    deltas = jnp.abs(jnp.linspace(math.log(HY_TARGET) / HY_SLOW_PCT,
                                  math.log(HY_TARGET) / HY_FAST_PCT, D_GROUP, dtype=F32))
    h = h * jnp.exp(-t * deltas)[:, None, None, :]
    kf, kb = h[:, :, 0], h[:, :, 1]
    k2 = jnp.concatenate([kf, jnp.zeros_like(kf[:1]), kb[1:][::-1]], axis=0)
    k2 = k2 / jnp.sum(jnp.abs(k2), axis=0, keepdims=True)
    return jnp.fft.rfft(k2, n=2 * L, axis=0)


def long_conv(u, k_spec, bias):
    L = u.shape[1]
    u32 = u.astype(F32)
    y = jnp.fft.irfft(jnp.fft.rfft(u32, n=2 * L, axis=1) * k_spec[None], n=2 * L, axis=1)[:, :L]
    return (y + u32 * bias.astype(F32)).astype(u.dtype)


def hyena_mixer(p_hy, short_w, short_b, spectra, hy_bias):
    z = depthwise_conv(p_hy, short_w, short_b)
    x1, x2, v = jnp.split(z, HY_ORDER + 1, axis=-1)
    y = v
    for n, gate in enumerate((x1, x2)):
        y = gate * long_conv(y, spectra[:, n], hy_bias[n])
    return y


def gmlp_mixer(p_gm, ln_g, ln_b, ws, bs):
    B, L, _ = p_gm.shape
    z = jax.nn.gelu(p_gm)
    u, v = jnp.split(z, 2, axis=-1)
    v = layer_norm(v, ln_g, ln_b).reshape(B, L // GM_CHUNK, GM_CHUNK, GM_HEADS, D_GROUP // GM_HEADS)
    s = jnp.einsum('gij,bcjgd->bcigd', ws.astype(v.dtype), v) + bs.T.astype(v.dtype)[None, None, :, :, None]
    return u * s.reshape(B, L, D_GROUP)


def conv_mixer(p_cv, dw_w, dw_b, ln_g, ln_b):
    a, b = jnp.split(p_cv, 2, axis=-1)
    g = depthwise_conv(a * jax.nn.sigmoid(b), dw_w, dw_b)
    B, L, _ = g.shape
    g = layer_norm(g.reshape(B, L, CV_GROUPS, D_GROUP // CV_GROUPS),
                   ln_g.reshape(CV_GROUPS, -1), ln_b.reshape(CV_GROUPS, -1)).reshape(B, L, D_GROUP)
    return jax.nn.silu(g)


def mla_queries(cq, qa_norm, w_uq, rope):
    B, L, _ = cq.shape
    q = (rms_norm(cq, qa_norm) @ w_uq).reshape(B, L, MLA_HEADS, MLA_NOPE + MLA_ROPE)
    if rope is not None:
        cos, sin = rope
        q = jnp.concatenate([q[..., :MLA_NOPE],
                             apply_rope(q[..., MLA_NOPE:], cos[None, :, None], sin[None, :, None])], axis=-1)
    return q


def mla_keys_values(ckv, kva_norm, w_ukv, rope):
    B, L, _ = ckv.shape
    c_kv, k_rope = ckv[..., :MLA_KV_RANK], ckv[..., MLA_KV_RANK:]
    kv = (rms_norm(c_kv, kva_norm) @ w_ukv).reshape(B, L, MLA_HEADS, MLA_NOPE + MLA_V)
    k_nope, v = kv[..., :MLA_NOPE], kv[..., MLA_NOPE:]
    if rope is not None:
        cos, sin = rope
        k_rope = apply_rope(k_rope, cos[None], sin[None])
    k = jnp.concatenate([k_nope, jnp.broadcast_to(k_rope[:, :, None, :], (B, L, MLA_HEADS, MLA_ROPE))], axis=-1)
    return k, v


def block_attention(q, k, v):
    B, Lq, H, Dk = q.shape
    nb = Lq // ATTN_BLOCK
    qb = q.reshape(B, nb, ATTN_BLOCK, H, Dk).transpose(1, 0, 2, 3, 4)
    scale = 1.0 / math.sqrt(Dk)

    def attend(qblk):
        s = jnp.einsum('bqhd,bkhd->bhqk', qblk, k).astype(F32) * scale
        p = jax.nn.softmax(s, axis=-1).astype(v.dtype)
        return jnp.einsum('bhqk,bkhd->bqhd', p, v)

    o = lax.map(attend, qb)
    return o.transpose(1, 0, 2, 3, 4).reshape(B, Lq, H * v.shape[-1])


def mix_tokens(proj, k, v, rope, spectra, lp):
    B, L, _ = proj.shape
    y_hy = hyena_mixer(proj[..., HY_OFF:GM_OFF], lp['hy_short_w'], lp['hy_short_b'], spectra, lp['hy_bias'])
    y_gm = gmlp_mixer(proj[..., GM_OFF:CV_OFF], lp['gm_ln_g'], lp['gm_ln_b'], lp['gm_ws'], lp['gm_bs'])
    y_cv = conv_mixer(proj[..., CV_OFF:MQ_OFF], lp['cv_dw_w'], lp['cv_dw_b'], lp['cv_ln_g'], lp['cv_ln_b'])
    q = mla_queries(proj[..., MQ_OFF:MKV_OFF], lp['mla_qa_norm'], lp['w_uq'], rope)
    y_at = block_attention(q, k, v)
    y = jnp.stack([y_hy, y_gm, y_cv, y_at], axis=-2)
    y = rms_norm(y, lp['mix_norm_g'].reshape(N_MIXERS, D_GROUP)).reshape(B, L, D_MIX)
    return y @ lp['w_out']


def moe_ffn(h, w_router, router_bias, w_gate, w_up, w_down):
    s = jax.nn.sigmoid((h @ w_router).astype(F32))
    sel = s + router_bias.astype(F32)
    grp = sel.reshape(*sel.shape[:-1], N_EXPERT_GROUPS, EXPERTS_PER_GROUP)
    grp_score = jnp.sum(lax.top_k(grp, GROUP_SCORE_TOPK)[0], axis=-1)
    best = jnp.argmax(grp_score, axis=-1)
    in_group = (jnp.arange(N_EXPERTS) // EXPERTS_PER_GROUP) == best[..., None]
    _, idx = lax.top_k(jnp.where(in_group, sel, -jnp.inf), TOP_K)
    w = jnp.take_along_axis(s, idx, axis=-1)
    w = w / jnp.sum(w, axis=-1, keepdims=True)
    gates = jnp.sum(jax.nn.one_hot(idx, N_EXPERTS, dtype=F32) * w[..., None], axis=-2).astype(h.dtype)
    y = jnp.zeros_like(h)
    for e in range(N_EXPERTS):
        a = jax.nn.silu(h @ w_gate[e]) * (h @ w_up[e])
        y = y + gates[..., e:e + 1] * (a @ w_down[e])
    return y


def setup_inputs(seed: int = 0) -> dict:
    key = jax.random.key(seed)
    ks = iter(jax.random.split(key, 48))

    def nrm(shape, scale):
        return jax.random.normal(next(ks), shape, F32) * scale

    def gain(shape):
        return 1.0 + nrm(shape, 0.05)

    D, E, F = D_MODEL, N_EXPERTS, D_EXPERT
    return {
        "x": nrm((BATCH, SEQ, D), 1.0),
        "c": nrm((BATCH, D), 1.0),
        "ctx": nrm((BATCH, CTX_LEN, D), 1.0),
        "c_ctx": nrm((D,), 1.0),
        "ada_w": nrm((DEPTH, D, 6 * D), 0.5 * D ** -0.5),
        "ada_b": nrm((DEPTH, 6 * D), 0.02),
        "norm1_g": gain((DEPTH, D)),
        "norm2_g": gain((DEPTH, D)),
        "w_in": nrm((DEPTH, D, IN_COLS), D ** -0.5),
        "hy_short_w": nrm((DEPTH, HY_SHORT, HY_COLS), HY_SHORT ** -0.5),
        "hy_short_b": nrm((DEPTH, HY_COLS), 0.02),
        "hy_f_w1": nrm((DEPTH, HY_EMB, HY_FFN), HY_EMB ** -0.5),
        "hy_f_b1": nrm((DEPTH, HY_FFN), 0.02),
        "hy_f_freq": 1.0 + nrm((DEPTH, HY_FFN), 0.1),
        "hy_f_w2": nrm((DEPTH, HY_FFN, HY_FFN), HY_FFN ** -0.5),
        "hy_f_b2": nrm((DEPTH, HY_FFN), 0.02),
        "hy_f_w3": nrm((DEPTH, HY_FFN, HY_ORDER * 2 * D_GROUP), HY_FFN ** -0.5),
        "hy_bias": nrm((DEPTH, HY_ORDER, D_GROUP), 0.5),
        "gm_ln_g": gain((DEPTH, D_GROUP)),
        "gm_ln_b": nrm((DEPTH, D_GROUP), 0.02),
        "gm_ws": nrm((DEPTH, GM_HEADS, GM_CHUNK, GM_CHUNK), 0.5 * GM_CHUNK ** -0.5),
        "gm_bs": 1.0 + nrm((DEPTH, GM_HEADS, GM_CHUNK), 0.02),
        "cv_dw_w": nrm((DEPTH, CV_WIDTH, D_GROUP), CV_WIDTH ** -0.5),
        "cv_dw_b": nrm((DEPTH, D_GROUP), 0.02),
        "cv_ln_g": gain((DEPTH, D_GROUP)),
        "cv_ln_b": nrm((DEPTH, D_GROUP), 0.02),
        "mla_qa_norm": gain((DEPTH, MLA_Q_RANK)),
        "w_uq": nrm((DEPTH, MLA_Q_RANK, MLA_HEADS * (MLA_NOPE + MLA_ROPE)), MLA_Q_RANK ** -0.5),
        "mla_kva_norm": gain((DEPTH, MLA_KV_RANK)),
        "w_ukv": nrm((DEPTH, MLA_KV_RANK, MLA_HEADS * (MLA_NOPE + MLA_V)), MLA_KV_RANK ** -0.5),
        "mix_norm_g": gain((DEPTH, D_MIX)),
        "w_out": nrm((DEPTH, D_MIX, D), D_MIX ** -0.5),
        "w_router": nrm((D, E), D ** -0.5),
        "router_bias": nrm((E,), 0.01),
        "exp_w_gate": nrm((DEPTH, E, D, F), D ** -0.5),
        "exp_w_up": nrm((DEPTH, E, D, F), D ** -0.5),
        "exp_w_down": nrm((DEPTH, E, F, D), F ** -0.5),
        "final_norm_g": gain((D,)),
    }


def reference(x, c, ctx, c_ctx, ada_w, ada_b, norm1_g, norm2_g, w_in, hy_short_w, hy_short_b,
              hy_f_w1, hy_f_b1, hy_f_freq, hy_f_w2, hy_f_b2, hy_f_w3, hy_bias, gm_ln_g, gm_ln_b,
              gm_ws, gm_bs, cv_dw_w, cv_dw_b, cv_ln_g, cv_ln_b, mla_qa_norm, w_uq, mla_kva_norm,
              w_ukv, mix_norm_g, w_out, w_router, router_bias, exp_w_gate, exp_w_up, exp_w_down,
              final_norm_g):
    n_lat = x.shape[1]
    n_ctx = ctx.shape[1]
    rope = axial_rope_tables(n_lat)
    for l in range(DEPTH):
        last = l == DEPTH - 1
        lp = {
            'hy_short_w': hy_short_w[l], 'hy_short_b': hy_short_b[l], 'hy_bias': hy_bias[l],
            'gm_ln_g': gm_ln_g[l], 'gm_ln_b': gm_ln_b[l], 'gm_ws': gm_ws[l], 'gm_bs': gm_bs[l],
            'cv_dw_w': cv_dw_w[l], 'cv_dw_b': cv_dw_b[l], 'cv_ln_g': cv_ln_g[l], 'cv_ln_b': cv_ln_b[l],
            'mla_qa_norm': mla_qa_norm[l], 'w_uq': w_uq[l],
            'mix_norm_g': mix_norm_g[l], 'w_out': w_out[l],
        }
        filt = (hy_f_w1[l], hy_f_b1[l], hy_f_freq[l], hy_f_w2[l], hy_f_b2[l], hy_f_w3[l])
        sh1, sc1, g1, sh2, sc2, g2 = modulation(c, ada_w[l], ada_b[l])
        csh1, csc1, cg1, csh2, csc2, cg2 = modulation(c_ctx, ada_w[l], ada_b[l])

        h_lat = rms_norm(x, norm1_g[l]) * (1.0 + sc1) + sh1
        h_ctx = rms_norm(ctx, norm1_g[l]) * (1.0 + csc1) + csh1
        proj_lat = h_lat @ w_in[l]
        proj_ctx = h_ctx @ (w_in[l][:, MKV_OFF:] if last else w_in[l])
        k_ctx, v_ctx = mla_keys_values(proj_ctx[..., -MKV_COLS:], mla_kva_norm[l], w_ukv[l], None)
        k_lat, v_lat = mla_keys_values(proj_lat[..., MKV_OFF:], mla_kva_norm[l], w_ukv[l], rope)
        k_all = jnp.concatenate([k_lat, k_ctx], axis=1)
        v_all = jnp.concatenate([v_lat, v_ctx], axis=1)
        spectra_lat = hyena_filter_spectra(n_lat, *filt)
        x_mid = x + g1 * mix_tokens(proj_lat, k_all, v_all, rope, spectra_lat, lp)

        hf = rms_norm(x_mid, norm2_g[l]) * (1.0 + sc2) + sh2
        x = x_mid + g2 * moe_ffn(hf, w_router, router_bias, exp_w_gate[l], exp_w_up[l], exp_w_down[l])

        if not last:
            spectra_ctx = hyena_filter_spectra(n_ctx, *filt)
            ctx_mid = ctx + cg1 * mix_tokens(proj_ctx, k_ctx, v_ctx, None, spectra_ctx, lp)
            hc = rms_norm(ctx_mid, norm2_g[l]) * (1.0 + csc2) + csh2
            ctx = ctx_mid + cg2 * moe_ffn(hc, w_router, router_bias, exp_w_gate[l], exp_w_up[l], exp_w_down[l])
    return rms_norm(x, final_norm_g)
```

```python
import functools
import math

import jax
import jax.numpy as jnp
from jax import lax
from jax.experimental import pallas as pl
from jax.experimental.pallas import tpu as pltpu

F32 = jnp.float32
BF16 = jnp.bfloat16

D_MODEL = 1024
DEPTH = 2
GRID_W = 64
EPS = 1e-6

D_GROUP = 256
N_MIXERS = 4
HY_ORDER = 2
HY_EMB = 33
HY_BANDS = (HY_EMB - 1) // 2
HY_TARGET = 1e-2
HY_FAST_PCT = 0.3
HY_SLOW_PCT = 1.5
GM_CHUNK = 128
GM_HEADS = 4
CV_GROUPS = 4
MLA_HEADS = 4
MLA_NOPE = 64
MLA_ROPE = 32
MLA_V = 64
MLA_Q_RANK = 192
MLA_KV_RANK = 128
ROPE_BASE = 10000.0
ATTN_BLOCK = 128
N_EXPERTS = 16
N_EXPERT_GROUPS = 4
EXPERTS_PER_GROUP = N_EXPERTS // N_EXPERT_GROUPS
TOP_K = 2
GROUP_SCORE_TOPK = 2
D_EXPERT = 512

HY_COLS = (HY_ORDER + 1) * D_GROUP
GM_COLS = 2 * D_GROUP
CV_COLS = 2 * D_GROUP
MQ_COLS = MLA_Q_RANK
MKV_COLS = MLA_KV_RANK + MLA_ROPE
HY_OFF = 0
GM_OFF = HY_OFF + HY_COLS
CV_OFF = GM_OFF + GM_COLS
MQ_OFF = CV_OFF + CV_COLS
MKV_OFF = MQ_OFF + MQ_COLS
IN_COLS = MKV_OFF + MKV_COLS

MOE_TILE = 512


def _rms_norm(x, g):
    x32 = x.astype(F32)
    y = x32 * lax.rsqrt(jnp.mean(x32 * x32, axis=-1, keepdims=True) + EPS)
    return (y * g.astype(F32)).astype(x.dtype)


def _layer_norm(x, g, b):
    x32 = x.astype(F32)
    mu = jnp.mean(x32, axis=-1, keepdims=True)
    var = jnp.mean(jnp.square(x32 - mu), axis=-1, keepdims=True)
    return ((x32 - mu) * lax.rsqrt(var + EPS) * g.astype(F32) + b.astype(F32)).astype(x.dtype)


def _depthwise_conv(x, w, b):
    pad = (w.shape[0] - 1) // 2
    y = lax.conv_general_dilated(x, w[:, None, :].astype(x.dtype), window_strides=(1,),
                                 padding=[(pad, pad)], dimension_numbers=('NWC', 'WIO', 'NWC'),
                                 feature_group_count=x.shape[-1])
    return y + b.astype(x.dtype)


def _modulation(cond, w, b):
    m = jax.nn.silu(cond) @ w + b
    return jnp.split(jnp.expand_dims(m, -2), 6, axis=-1)


def _axial_rope_tables(n_lat):
    rows = n_lat // GRID_W
    row = jnp.repeat(jnp.arange(rows), GRID_W).astype(F32)
    col = jnp.tile(jnp.arange(GRID_W), rows).astype(F32)
    n_freq = MLA_ROPE // 4
    inv = ROPE_BASE ** (-jnp.arange(n_freq, dtype=F32) / n_freq)
    ang = jnp.concatenate([row[:, None] * inv, col[:, None] * inv], axis=-1)
    return jnp.cos(ang), jnp.sin(ang)


def _apply_rope(x, cos, sin):
    x32 = x.astype(F32)
    x1, x2 = jnp.split(x32, 2, axis=-1)
    return jnp.concatenate([x1 * cos - x2 * sin, x1 * sin + x2 * cos], axis=-1).astype(x.dtype)


def _hyena_filter_spectra(L, w1, b1, freq, w2, b2, w3):
    w1, b1, freq, w2, b2, w3 = (a.astype(F32) for a in (w1, b1, freq, w2, b2, w3))
    t = jnp.linspace(0.0, 1.0, L, dtype=F32)[:, None]
    w = 2.0 * math.pi * jnp.arange(L, dtype=F32)[:, None] / L
    f = jnp.linspace(1e-4, HY_BANDS - 1, HY_BANDS, dtype=F32)[None, :]
    z = jnp.concatenate([t, jnp.cos(f * w), -jnp.sin(f * w)], axis=-1)
    h = jnp.sin(freq * (z @ w1 + b1))
    h = jnp.sin(freq * (h @ w2 + b2))
    h = (h @ w3).reshape(L, HY_ORDER, 2, D_GROUP)
    deltas = jnp.abs(jnp.linspace(math.log(HY_TARGET) / HY_SLOW_PCT,
                                  math.log(HY_TARGET) / HY_FAST_PCT, D_GROUP, dtype=F32))
    h = h * jnp.exp(-t * deltas)[:, None, None, :]
    kf, kb = h[:, :, 0], h[:, :, 1]
    k2 = jnp.concatenate([kf, jnp.zeros_like(kf[:1]), kb[1:][::-1]], axis=0)
    k2 = k2 / jnp.sum(jnp.abs(k2), axis=0, keepdims=True)
    return jnp.fft.rfft(k2, n=2 * L, axis=0)


def _long_conv(u, k_spec, bias):
    L = u.shape[1]
    u32 = u.astype(F32)
    y = jnp.fft.irfft(jnp.fft.rfft(u32, n=2 * L, axis=1) * k_spec[None], n=2 * L, axis=1)[:, :L]
    return (y + u32 * bias.astype(F32)).astype(u.dtype)


def _hyena_mixer(p_hy, short_w, short_b, spectra, hy_bias):
    z = _depthwise_conv(p_hy, short_w, short_b)
    x1, x2, v = jnp.split(z, HY_ORDER + 1, axis=-1)
    y = v
    for n, gate in enumerate((x1, x2)):
        y = gate * _long_conv(y, spectra[:, n], hy_bias[n])
    return y


def _gmlp_mixer(p_gm, ln_g, ln_b, ws, bs):
    B, L, _ = p_gm.shape
    z = jax.nn.gelu(p_gm)
    u, v = jnp.split(z, 2, axis=-1)
    v = _layer_norm(v, ln_g, ln_b).reshape(B, L // GM_CHUNK, GM_CHUNK, GM_HEADS, D_GROUP // GM_HEADS)
    s = jnp.einsum('gij,bcjgd->bcigd', ws.astype(v.dtype), v) + bs.T.astype(v.dtype)[None, None, :, :, None]
    return u * s.reshape(B, L, D_GROUP)


def _conv_mixer(p_cv, dw_w, dw_b, ln_g, ln_b):
    a, b = jnp.split(p_cv, 2, axis=-1)
    g = _depthwise_conv(a * jax.nn.sigmoid(b), dw_w, dw_b)
    B, L, _ = g.shape
    g = _layer_norm(g.reshape(B, L, CV_GROUPS, D_GROUP // CV_GROUPS),
                    ln_g.reshape(CV_GROUPS, -1), ln_b.reshape(CV_GROUPS, -1)).reshape(B, L, D_GROUP)
    return jax.nn.silu(g)


def _mla_queries(cq, qa_norm, w_uq, rope):
    B, L, _ = cq.shape
    q = (_rms_norm(cq, qa_norm) @ w_uq).reshape(B, L, MLA_HEADS, MLA_NOPE + MLA_ROPE)
    if rope is not None:
        cos, sin = rope
        q = jnp.concatenate([q[..., :MLA_NOPE],
                             _apply_rope(q[..., MLA_NOPE:], cos[None, :, None], sin[None, :, None])], axis=-1)
    return q


def _mla_keys_values(ckv, kva_norm, w_ukv, rope):
    B, L, _ = ckv.shape
    c_kv, k_rope = ckv[..., :MLA_KV_RANK], ckv[..., MLA_KV_RANK:]
    kv = (_rms_norm(c_kv, kva_norm) @ w_ukv).reshape(B, L, MLA_HEADS, MLA_NOPE + MLA_V)
    k_nope, v = kv[..., :MLA_NOPE], kv[..., MLA_NOPE:]
    if rope is not None:
        cos, sin = rope
        k_rope = _apply_rope(k_rope, cos[None], sin[None])
    k = jnp.concatenate([k_nope, jnp.broadcast_to(k_rope[:, :, None, :], (B, L, MLA_HEADS, MLA_ROPE))], axis=-1)
    return k, v


def _block_attention(q, k, v):
    B, Lq, H, Dk = q.shape
    nb = Lq // ATTN_BLOCK
    qb = q.reshape(B, nb, ATTN_BLOCK, H, Dk).transpose(1, 0, 2, 3, 4)
    scale = 1.0 / math.sqrt(Dk)

    def attend(qblk):
        s = jnp.einsum('bqhd,bkhd->bhqk', qblk, k).astype(F32) * scale
        p = jax.nn.softmax(s, axis=-1).astype(v.dtype)
        return jnp.einsum('bhqk,bkhd->bqhd', p, v)

    o = lax.map(attend, qb)
    return o.transpose(1, 0, 2, 3, 4).reshape(B, Lq, H * v.shape[-1])


def _mix_tokens(proj, k, v, rope, spectra, lp):
    B, L, _ = proj.shape
    y_hy = _hyena_mixer(proj[..., HY_OFF:GM_OFF], lp['hy_short_w'], lp['hy_short_b'], spectra, lp['hy_bias'])
    y_gm = _gmlp_mixer(proj[..., GM_OFF:CV_OFF], lp['gm_ln_g'], lp['gm_ln_b'], lp['gm_ws'], lp['gm_bs'])
    y_cv = _conv_mixer(proj[..., CV_OFF:MQ_OFF], lp['cv_dw_w'], lp['cv_dw_b'], lp['cv_ln_g'], lp['cv_ln_b'])
    q = _mla_queries(proj[..., MQ_OFF:MKV_OFF], lp['mla_qa_norm'], lp['w_uq'], rope)
    y_at = _block_attention(q, k, v)
    y = jnp.stack([y_hy, y_gm, y_cv, y_at], axis=-2)
    y = _rms_norm(y, lp['mix_norm_g'].reshape(N_MIXERS, D_GROUP)).reshape(B, L, D_GROUP * N_MIXERS)
    return y @ lp['w_out']


def _route(h, w_router, router_bias):
    s = jax.nn.sigmoid(jnp.dot(h, w_router, precision=lax.Precision.HIGHEST).astype(F32))
    sel = s + router_bias.astype(F32)
    grp = sel.reshape(sel.shape[0], N_EXPERT_GROUPS, EXPERTS_PER_GROUP)
    grp_score = jnp.sum(lax.top_k(grp, GROUP_SCORE_TOPK)[0], axis=-1)
    best = jnp.argmax(grp_score, axis=-1)
    in_group = (jnp.arange(N_EXPERTS) // EXPERTS_PER_GROUP) == best[..., None]
    _, idx = lax.top_k(jnp.where(in_group, sel, -jnp.inf), TOP_K)
    w = jnp.take_along_axis(s, idx, axis=-1)
    w = w / jnp.sum(w, axis=-1, keepdims=True)
    return idx.astype(jnp.int32), w


def _dispatch_plan(idx, w, tile):
    T = idx.shape[0]
    n_slots = T * TOP_K
    P = n_slots + N_EXPERTS * tile
    n_tiles = P // tile
    flat_e = idx.reshape(n_slots)
    order = jnp.argsort(flat_e, stable=True).astype(jnp.int32)
    onehot = (flat_e[:, None] == jnp.arange(N_EXPERTS, dtype=jnp.int32)[None, :]).astype(jnp.int32)
    csum = jnp.cumsum(onehot, axis=0)
    counts = csum[-1]
    rank = jnp.take_along_axis(csum, flat_e[:, None], axis=1)[:, 0] - 1
    padded = ((counts + tile - 1) // tile) * tile
    seg_end = jnp.cumsum(padded)
    seg_start = seg_end - padded
    src_start = jnp.cumsum(counts) - counts
    slot_row = (seg_start[flat_e] + rank).reshape(T, TOP_K)

    tile_first = jnp.arange(n_tiles, dtype=jnp.int32) * tile
    tile_expert = jnp.minimum(jnp.searchsorted(seg_end, tile_first, side='right'), N_EXPERTS - 1).astype(jnp.int32)
    tile_valid = (tile_first < seg_end[-1]).astype(jnp.int32)

    p = jnp.arange(P, dtype=jnp.int32)
    pe = tile_expert[p // tile]
    r = p - seg_start[pe]
    valid = (r < counts[pe]) & (p < seg_end[-1])
    src = jnp.clip(src_start[pe] + r, 0, n_slots - 1)
    slot = order[src]
    row_token = jnp.where(valid, slot // TOP_K, 0)
    row_gate = jnp.where(valid, w.reshape(n_slots)[slot], 0.0)
    return row_token, row_gate, tile_expert, tile_valid, slot_row


def _moe_tile_kernel(te_ref, tv_ref, x_ref, g_ref, wg_ref, wu_ref, wd_ref, o_ref):
    i = pl.program_id(0)

    @pl.when(tv_ref[i] != 0)
    def _():
        x = x_ref[...]
        hg = jnp.dot(x, wg_ref[...], preferred_element_type=F32)
        hu = jnp.dot(x, wu_ref[...], preferred_element_type=F32)
        a = (hg * jax.nn.sigmoid(hg) * hu).astype(BF16)
        y = jnp.dot(a, wd_ref[...], preferred_element_type=F32)
        o_ref[...] = y * g_ref[...]

    @pl.when(tv_ref[i] == 0)
    def _():
        o_ref[...] = jnp.zeros_like(o_ref)


def _moe_grouped(xs, gates, tile_expert, tile_valid, wg, wu, wd, tile):
    P, D = xs.shape
    n_tiles = P // tile
    F = wg.shape[-1]
    grid_spec = pltpu.PrefetchScalarGridSpec(
        num_scalar_prefetch=2,
        grid=(n_tiles,),
        in_specs=[
            pl.BlockSpec((tile, D), lambda i, te, tv: (i, 0)),
            pl.BlockSpec((tile, 1), lambda i, te, tv: (i, 0)),
            pl.BlockSpec((None, D, F), lambda i, te, tv: (te[i], 0, 0)),
            pl.BlockSpec((None, D, F), lambda i, te, tv: (te[i], 0, 0)),
            pl.BlockSpec((None, F, D), lambda i, te, tv: (te[i], 0, 0)),
        ],
        out_specs=pl.BlockSpec((tile, D), lambda i, te, tv: (i, 0)),
    )
    return pl.pallas_call(
        _moe_tile_kernel,
        grid_spec=grid_spec,
        out_shape=jax.ShapeDtypeStruct((P, D), F32),
        compiler_params=pltpu.CompilerParams(
            dimension_semantics=("arbitrary",),
            vmem_limit_bytes=48 * 1024 * 1024),
        name="moe_grouped",
    )(tile_expert, tile_valid, xs, gates, wg, wu, wd)


def _moe_ffn(h, w_router, router_bias, wg, wu, wd):
    idx, w = _route(h, w_router, router_bias)
    row_token, row_gate, tile_expert, tile_valid, slot_row = _dispatch_plan(idx, w, MOE_TILE)
    xs = jnp.take(h.astype(BF16), row_token, axis=0)
    ys = _moe_grouped(xs, row_gate[:, None], tile_expert, tile_valid, wg, wu, wd, MOE_TILE)
    return jnp.take(ys, slot_row[:, 0], axis=0) + jnp.take(ys, slot_row[:, 1], axis=0)


def kernel(x, c, ctx, c_ctx, ada_w, ada_b, norm1_g, norm2_g, w_in, hy_short_w, hy_short_b, hy_f_w1, hy_f_b1, hy_f_freq, hy_f_w2, hy_f_b2, hy_f_w3, hy_bias, gm_ln_g, gm_ln_b, gm_ws, gm_bs, cv_dw_w, cv_dw_b, cv_ln_g, cv_ln_b, mla_qa_norm, w_uq, mla_kva_norm, w_ukv, mix_norm_g, w_out, w_router, router_bias, exp_w_gate, exp_w_up, exp_w_down, final_norm_g):
    B, n_lat, D = x.shape
    n_ctx = ctx.shape[1]
    rope = _axial_rope_tables(n_lat)
    for l in range(DEPTH):
        last = l == DEPTH - 1
        lp = {
            'hy_short_w': hy_short_w[l], 'hy_short_b': hy_short_b[l], 'hy_bias': hy_bias[l],
            'gm_ln_g': gm_ln_g[l], 'gm_ln_b': gm_ln_b[l], 'gm_ws': gm_ws[l], 'gm_bs': gm_bs[l],
            'cv_dw_w': cv_dw_w[l], 'cv_dw_b': cv_dw_b[l], 'cv_ln_g': cv_ln_g[l], 'cv_ln_b': cv_ln_b[l],
            'mla_qa_norm': mla_qa_norm[l], 'w_uq': w_uq[l],
            'mix_norm_g': mix_norm_g[l], 'w_out': w_out[l],
        }
        filt = (hy_f_w1[l], hy_f_b1[l], hy_f_freq[l], hy_f_w2[l], hy_f_b2[l], hy_f_w3[l])
        wg, wu, wd = exp_w_gate[l].astype(BF16), exp_w_up[l].astype(BF16), exp_w_down[l].astype(BF16)
        sh1, sc1, g1, sh2, sc2, g2 = _modulation(c, ada_w[l], ada_b[l])
        csh1, csc1, cg1, csh2, csc2, cg2 = _modulation(c_ctx, ada_w[l], ada_b[l])

        h_lat = _rms_norm(x, norm1_g[l]) * (1.0 + sc1) + sh1
        h_ctx = _rms_norm(ctx, norm1_g[l]) * (1.0 + csc1) + csh1
        proj_lat = h_lat @ w_in[l]
        proj_ctx = h_ctx @ (w_in[l][:, MKV_OFF:] if last else w_in[l])
        k_ctx, v_ctx = _mla_keys_values(proj_ctx[..., -MKV_COLS:], mla_kva_norm[l], w_ukv[l], None)
        k_lat, v_lat = _mla_keys_values(proj_lat[..., MKV_OFF:], mla_kva_norm[l], w_ukv[l], rope)
        k_all = jnp.concatenate([k_lat, k_ctx], axis=1)
        v_all = jnp.concatenate([v_lat, v_ctx], axis=1)
        spectra_lat = _hyena_filter_spectra(n_lat, *filt)
        x_mid = x + g1 * _mix_tokens(proj_lat, k_all, v_all, rope, spectra_lat, lp)
        hf = _rms_norm(x_mid, norm2_g[l]) * (1.0 + sc2) + sh2

        if not last:
            spectra_ctx = _hyena_filter_spectra(n_ctx, *filt)
            ctx_mid = ctx + cg1 * _mix_tokens(proj_ctx, k_ctx, v_ctx, None, spectra_ctx, lp)
            hc = _rms_norm(ctx_mid, norm2_g[l]) * (1.0 + csc2) + csh2
            h_all = jnp.concatenate([hf.reshape(B * n_lat, D), hc.reshape(B * n_ctx, D)], axis=0)
            y_all = _moe_ffn(h_all, w_router, router_bias, wg, wu, wd)
            x = x_mid + g2 * y_all[:B * n_lat].reshape(B, n_lat, D)
            ctx = ctx_mid + cg2 * y_all[B * n_lat:].reshape(B, n_ctx, D)
        else:
            y = _moe_ffn(hf.reshape(B * n_lat, D), w_router, router_bias, wg, wu, wd)
            x = x_mid + g2 * y.reshape(B, n_lat, D)
    return _rms_norm(x, final_norm_g)
```

```python
import functools
import math

import jax
import jax.numpy as jnp
import numpy as np
from jax import lax
from jax.experimental import pallas as pl
from jax.experimental.pallas import tpu as pltpu

F32 = jnp.float32
BF16 = jnp.bfloat16

D_MODEL = 1024
DEPTH = 2
GRID_W = 64
EPS = 1e-6

D_GROUP = 256
N_MIXERS = 4
HY_ORDER = 2
HY_EMB = 33
HY_BANDS = (HY_EMB - 1) // 2
HY_TARGET = 1e-2
HY_FAST_PCT = 0.3
HY_SLOW_PCT = 1.5
GM_CHUNK = 128
GM_HEADS = 4
CV_GROUPS = 4
MLA_HEADS = 4
MLA_NOPE = 64
MLA_ROPE = 32
MLA_V = 64
MLA_Q_RANK = 192
MLA_KV_RANK = 128
ROPE_BASE = 10000.0
N_EXPERTS = 16
N_EXPERT_GROUPS = 4
EXPERTS_PER_GROUP = N_EXPERTS // N_EXPERT_GROUPS
TOP_K = 2
GROUP_SCORE_TOPK = 2
D_EXPERT = 512

HY_COLS = (HY_ORDER + 1) * D_GROUP
GM_COLS = 2 * D_GROUP
CV_COLS = 2 * D_GROUP
MQ_COLS = MLA_Q_RANK
MKV_COLS = MLA_KV_RANK + MLA_ROPE
HY_OFF = 0
GM_OFF = HY_OFF + HY_COLS
CV_OFF = GM_OFF + GM_COLS
MQ_OFF = CV_OFF + CV_COLS
MKV_OFF = MQ_OFF + MQ_COLS
IN_COLS = MKV_OFF + MKV_COLS

LANES = 128
HEAD_SLOT = LANES
QK_COLS = MLA_HEADS * HEAD_SLOT
V_COLS = MLA_HEADS * MLA_V
PQ_OFF = MQ_OFF
PQ_PAD = 256
PKV_OFF = PQ_OFF + PQ_PAD
PKR_OFF = PKV_OFF + MLA_KV_RANK
PROJ_COLS = PKR_OFF + LANES

TOK_TILE = 512
ATT_TILE = 512
MOE_TILE = 512
VMEM_LIMIT = 56 * 1024 * 1024


def _rms_norm(x, g):
    x32 = x.astype(F32)
    y = x32 * lax.rsqrt(jnp.mean(x32 * x32, axis=-1, keepdims=True) + EPS)
    return (y * g.astype(F32)).astype(x.dtype)


def _layer_norm(x, g, b):
    x32 = x.astype(F32)
    mu = jnp.mean(x32, axis=-1, keepdims=True)
    var = jnp.mean(jnp.square(x32 - mu), axis=-1, keepdims=True)
    return ((x32 - mu) * lax.rsqrt(var + EPS) * g.astype(F32) + b.astype(F32)).astype(x.dtype)


def _depthwise_conv(x, w, b):
    pad = (w.shape[0] - 1) // 2
    y = lax.conv_general_dilated(x, w[:, None, :].astype(x.dtype), window_strides=(1,),
                                 padding=[(pad, pad)], dimension_numbers=('NWC', 'WIO', 'NWC'),
                                 feature_group_count=x.shape[-1])
    return y + b.astype(x.dtype)


def _axial_rope_tables(n_lat):
    rows = n_lat // GRID_W
    row = jnp.repeat(jnp.arange(rows), GRID_W).astype(F32)
    col = jnp.tile(jnp.arange(GRID_W), rows).astype(F32)
    n_freq = MLA_ROPE // 4
    inv = ROPE_BASE ** (-jnp.arange(n_freq, dtype=F32) / n_freq)
    ang = jnp.concatenate([row[:, None] * inv, col[:, None] * inv], axis=-1)
    return jnp.cos(ang), jnp.sin(ang)


def _hyena_filter_spectra(L, w1, b1, freq, w2, b2, w3):
    w1, b1, freq, w2, b2, w3 = (a.astype(F32) for a in (w1, b1, freq, w2, b2, w3))
    t = jnp.linspace(0.0, 1.0, L, dtype=F32)[:, None]
    w = 2.0 * math.pi * jnp.arange(L, dtype=F32)[:, None] / L
    f = jnp.linspace(1e-4, HY_BANDS - 1, HY_BANDS, dtype=F32)[None, :]
    z = jnp.concatenate([t, jnp.cos(f * w), -jnp.sin(f * w)], axis=-1)
    h = jnp.sin(freq * (z @ w1 + b1))
    h = jnp.sin(freq * (h @ w2 + b2))
    h = (h @ w3).reshape(L, HY_ORDER, 2, D_GROUP)
    deltas = jnp.abs(jnp.linspace(math.log(HY_TARGET) / HY_SLOW_PCT,
                                  math.log(HY_TARGET) / HY_FAST_PCT, D_GROUP, dtype=F32))
    h = h * jnp.exp(-t * deltas)[:, None, None, :]
    kf, kb = h[:, :, 0], h[:, :, 1]
    k2 = jnp.concatenate([kf, jnp.zeros_like(kf[:1]), kb[1:][::-1]], axis=0)
    k2 = k2 / jnp.sum(jnp.abs(k2), axis=0, keepdims=True)
    return jnp.fft.rfft(k2, n=2 * L, axis=0)


def _long_conv(u, k_spec, bias):
    L = u.shape[1]
    u32 = u.astype(F32)
    y = jnp.fft.irfft(jnp.fft.rfft(u32, n=2 * L, axis=1) * k_spec[None], n=2 * L, axis=1)[:, :L]
    return (y + u32 * bias.astype(F32)).astype(u.dtype)


def _hyena_mixer(p_hy, short_w, short_b, spectra, hy_bias):
    z = _depthwise_conv(p_hy, short_w, short_b)
    x1, x2, v = jnp.split(z, HY_ORDER + 1, axis=-1)
    y = v
    for n, gate in enumerate((x1, x2)):
        y = gate * _long_conv(y, spectra[:, n], hy_bias[n])
    return y


def _gmlp_mixer(p_gm, ln_g, ln_b, ws, bs):
    B, L, _ = p_gm.shape
    z = jax.nn.gelu(p_gm)
    u, v = jnp.split(z, 2, axis=-1)
    v = _layer_norm(v, ln_g, ln_b).reshape(B, L // GM_CHUNK, GM_CHUNK, GM_HEADS, D_GROUP // GM_HEADS)
    s = jnp.einsum('gij,bcjgd->bcigd', ws.astype(v.dtype), v) + bs.T.astype(v.dtype)[None, None, :, :, None]
    return u * s.reshape(B, L, D_GROUP)


def _conv_mixer(p_cv, dw_w, dw_b, ln_g, ln_b):
    a, b = jnp.split(p_cv, 2, axis=-1)
    g = _depthwise_conv(a * jax.nn.sigmoid(b), dw_w, dw_b)
    B, L, _ = g.shape
    g = _layer_norm(g.reshape(B, L, CV_GROUPS, D_GROUP // CV_GROUPS),
                    ln_g.reshape(CV_GROUPS, -1), ln_b.reshape(CV_GROUPS, -1)).reshape(B, L, D_GROUP)
    return jax.nn.silu(g)


def _pack_w_in(w_in):
    D = w_in.shape[0]
    z = lambda n: jnp.zeros((D, n), w_in.dtype)
    return jnp.concatenate([
        w_in[:, :MQ_OFF],
        w_in[:, MQ_OFF:MKV_OFF], z(PQ_PAD - MQ_COLS),
        w_in[:, MKV_OFF:MKV_OFF + MLA_KV_RANK],
        w_in[:, MKV_OFF + MLA_KV_RANK:], z(LANES - MLA_ROPE),
    ], axis=1).astype(BF16)


def _pack_w_uq(w_uq):
    w = w_uq.reshape(MLA_Q_RANK, MLA_HEADS, MLA_NOPE + MLA_ROPE)
    w = jnp.pad(w, ((0, PQ_PAD - MLA_Q_RANK), (0, 0), (0, HEAD_SLOT - MLA_NOPE - MLA_ROPE)))
    return w.reshape(PQ_PAD, QK_COLS).astype(BF16)


def _pack_w_ukv(w_ukv):
    w = w_ukv.reshape(MLA_KV_RANK, MLA_HEADS, MLA_NOPE + MLA_V)
    k_part = jnp.pad(w[:, :, :MLA_NOPE], ((0, 0), (0, 0), (0, HEAD_SLOT - MLA_NOPE))).reshape(MLA_KV_RANK, QK_COLS)
    v_part = w[:, :, MLA_NOPE:].reshape(MLA_KV_RANK, V_COLS)
    top = jnp.concatenate([k_part, v_part], axis=1)
    eye = jnp.eye(LANES, dtype=w_ukv.dtype)[:, :MLA_ROPE]
    place = jnp.pad(eye, ((0, 0), (MLA_NOPE, HEAD_SLOT - MLA_NOPE - MLA_ROPE)))
    bot = jnp.concatenate([jnp.tile(place, (1, MLA_HEADS)), jnp.zeros((LANES, V_COLS), w_ukv.dtype)], axis=1)
    return jnp.concatenate([top, bot], axis=0).astype(BF16)


def _rope_slot_tables(n_lat, tile):
    cos, sin = _axial_rope_tables(n_lat)
    half = MLA_ROPE // 2
    ones = jnp.ones((n_lat, MLA_NOPE), F32)
    tail = jnp.ones((n_lat, HEAD_SLOT - MLA_NOPE - MLA_ROPE), F32)
    zeros = lambda n: jnp.zeros((n_lat, n), F32)
    cf = jnp.concatenate([ones, cos, cos, tail], axis=1)
    s_up = jnp.concatenate([zeros(MLA_NOPE + half), sin, zeros(HEAD_SLOT - MLA_NOPE - MLA_ROPE)], axis=1)
    s_dn = jnp.concatenate([zeros(MLA_NOPE), -sin, zeros(half + HEAD_SLOT - MLA_NOPE - MLA_ROPE)], axis=1)
    ident = jnp.ones((tile, HEAD_SLOT), F32)
    zt = jnp.zeros((tile, HEAD_SLOT), F32)
    return (jnp.concatenate([cf, ident], axis=0), jnp.concatenate([s_up, zt], axis=0),
            jnp.concatenate([s_dn, zt], axis=0))


def _rotate_slots(t, cf, s_up, s_dn):
    half = MLA_ROPE // 2
    outs = []
    for h in range(MLA_HEADS):
        tb = t[:, h * HEAD_SLOT:(h + 1) * HEAD_SLOT]
        outs.append(tb * cf + pltpu.roll(tb, half, 1) * s_up + pltpu.roll(tb, HEAD_SLOT - half, 1) * s_dn)
    return jnp.concatenate(outs, axis=1)


def _proj_kernel(x_ref, a_ref, sh_ref, win_ref, qg_ref, wuq_ref, kvg_ref, wkv_ref, cf_ref, su_ref, sd_ref,
                 hy_ref, gm_ref, cv_ref, q_ref, k_ref, v_ref):
    x = x_ref[...]
    ms = jnp.mean(x * x, axis=-1, keepdims=True)
    h = x * lax.rsqrt(ms + EPS) * a_ref[...] + sh_ref[...]
    p = jnp.dot(h.astype(BF16), win_ref[...], preferred_element_type=F32)
    hy_ref[...] = p[:, HY_OFF:GM_OFF]
    gm_ref[...] = p[:, GM_OFF:CV_OFF]
    cv_ref[...] = p[:, CV_OFF:MQ_OFF]

    cf, su, sd = cf_ref[...], su_ref[...], sd_ref[...]
    cq = p[:, PQ_OFF:PQ_OFF + PQ_PAD]
    qn = cq * lax.rsqrt(jnp.sum(cq * cq, axis=-1, keepdims=True) * (1.0 / MLA_Q_RANK) + EPS) * qg_ref[...]
    q = jnp.dot(qn.astype(BF16), wuq_ref[...], preferred_element_type=F32)
    q = _rotate_slots(q, cf, su, sd) * (1.0 / math.sqrt(MLA_NOPE + MLA_ROPE))
    q_ref[...] = q.astype(BF16)

    ckv = p[:, PKV_OFF:PKV_OFF + MLA_KV_RANK]
    kvn = ckv * lax.rsqrt(jnp.mean(ckv * ckv, axis=-1, keepdims=True) + EPS) * kvg_ref[...]
    kin = jnp.concatenate([kvn, p[:, PKR_OFF:PKR_OFF + LANES]], axis=1).astype(BF16)
    kv = jnp.dot(kin, wkv_ref[...], preferred_element_type=F32)
    k_ref[...] = _rotate_slots(kv[:, :QK_COLS], cf, su, sd).astype(BF16)
    v_ref[...] = kv[:, QK_COLS:].astype(BF16)


def _mod_row(tiles_per_seq, n_batch):
    return lambda i: (jnp.minimum(i // tiles_per_seq, n_batch), 0, 0)


def _proj_call(xt, a1, sh1, win, qg, wuq, kvg, wkv, rope_tabs, n_batch, n_lat):
    T, D = xt.shape
    tm = TOK_TILE
    tps = n_lat // tm
    n_lat_tiles = n_batch * tps
    mod = _mod_row(tps, n_batch)
    rope_idx = lambda i: (jnp.where(i < n_lat_tiles, i % tps, tps), 0)
    const = lambda i: (0, 0)
    tok = lambda i: (i, 0)
    out_cols = (HY_COLS, GM_COLS, CV_COLS, QK_COLS, QK_COLS, V_COLS)
    out_dtypes = (F32, F32, F32, BF16, BF16, BF16)
    return pl.pallas_call(
        _proj_kernel,
        grid=(T // tm,),
        in_specs=[
            pl.BlockSpec((tm, D), tok),
            pl.BlockSpec((None, 1, D), mod),
            pl.BlockSpec((None, 1, D), mod),
            pl.BlockSpec(win.shape, const),
            pl.BlockSpec(qg.shape, const),
            pl.BlockSpec(wuq.shape, const),
            pl.BlockSpec(kvg.shape, const),
            pl.BlockSpec(wkv.shape, const),
            pl.BlockSpec((tm, HEAD_SLOT), rope_idx),
            pl.BlockSpec((tm, HEAD_SLOT), rope_idx),
            pl.BlockSpec((tm, HEAD_SLOT), rope_idx),
        ],
        out_specs=[pl.BlockSpec((tm, n), tok) for n in out_cols],
        out_shape=[jax.ShapeDtypeStruct((T, n), dt) for n, dt in zip(out_cols, out_dtypes)],
        compiler_params=pltpu.CompilerParams(dimension_semantics=("arbitrary",), vmem_limit_bytes=VMEM_LIMIT),
        name="proj_qkv",
    )(xt, a1, sh1, win, qg, wuq, kvg, wkv, *rope_tabs)


def _attn_kernel(q_ref, kl_ref, kc_ref, vl_ref, vc_ref, o_ref):
    nt = (((1,), (1,)), ((), ()))
    lane = lax.broadcasted_iota(jnp.int32, (1, V_COLS), 1)
    vl = vl_ref[...]
    vc = vc_ref[...]
    acc = jnp.zeros(o_ref.shape, F32)
    for h in range(MLA_HEADS):
        sl_h = slice(h * HEAD_SLOT, (h + 1) * HEAD_SLOT)
        q = q_ref[:, sl_h]
        s_l = lax.dot_general(q, kl_ref[:, sl_h], nt, preferred_element_type=F32)
        s_c = lax.dot_general(q, kc_ref[:, sl_h], nt, preferred_element_type=F32)
        m = jnp.maximum(jnp.max(s_l, axis=-1, keepdims=True), jnp.max(s_c, axis=-1, keepdims=True))
        p_l = jnp.exp(s_l - m)
        p_c = jnp.exp(s_c - m)
        denom = jnp.sum(p_l, axis=-1, keepdims=True) + jnp.sum(p_c, axis=-1, keepdims=True)
        o = (jnp.dot(p_l.astype(BF16), vl, preferred_element_type=F32)
             + jnp.dot(p_c.astype(BF16), vc, preferred_element_type=F32))
        in_head = (lane >= h * MLA_V) & (lane < (h + 1) * MLA_V)
        acc = acc + jnp.where(in_head, o / denom, 0.0)
    o_ref[...] = acc


def _attn_call(q, k, v, n_batch, n_lat, n_ctx, ctx_keys_only=False):
    tq = ATT_TILE
    qt = n_lat // tq
    ctx_blk0 = (n_batch * n_lat) // n_ctx
    return pl.pallas_call(
        _attn_kernel,
        grid=(n_batch, qt),
        in_specs=[
            pl.BlockSpec((tq, QK_COLS), lambda b, j: (b * qt + j, 0)),
            pl.BlockSpec((n_lat, QK_COLS), lambda b, j: (b, 0)),
            pl.BlockSpec((n_ctx, QK_COLS), lambda b, j: (ctx_blk0 + b, 0)),
            pl.BlockSpec((n_lat, V_COLS), lambda b, j: (b, 0)),
            pl.BlockSpec((n_ctx, V_COLS), lambda b, j: (ctx_blk0 + b, 0)),
        ],
        out_specs=pl.BlockSpec((tq, V_COLS), lambda b, j: (b * qt + j, 0)),
        out_shape=jax.ShapeDtypeStruct((n_batch * n_lat, V_COLS), F32),
        compiler_params=pltpu.CompilerParams(dimension_semantics=("arbitrary", "arbitrary"),
                                             vmem_limit_bytes=VMEM_LIMIT),
        name="attn_latent",
    )(q, k, k, v, v)


def _attn_ctx_kernel(q_ref, k_ref, v_ref, o_ref):
    nt = (((1,), (1,)), ((), ()))
    lane = lax.broadcasted_iota(jnp.int32, (1, V_COLS), 1)
    v = v_ref[...]
    acc = jnp.zeros(o_ref.shape, F32)
    for h in range(MLA_HEADS):
        sl_h = slice(h * HEAD_SLOT, (h + 1) * HEAD_SLOT)
        s = lax.dot_general(q_ref[:, sl_h], k_ref[:, sl_h], nt, preferred_element_type=F32)
        m = jnp.max(s, axis=-1, keepdims=True)
        p = jnp.exp(s - m)
        denom = jnp.sum(p, axis=-1, keepdims=True)
        o = jnp.dot(p.astype(BF16), v, preferred_element_type=F32)
        in_head = (lane >= h * MLA_V) & (lane < (h + 1) * MLA_V)
        acc = acc + jnp.where(in_head, o / denom, 0.0)
    o_ref[...] = acc


def _attn_ctx_call(q, k, v, n_batch, n_lat, n_ctx):
    blk0 = (n_batch * n_lat) // n_ctx
    return pl.pallas_call(
        _attn_ctx_kernel,
        grid=(n_batch,),
        in_specs=[
            pl.BlockSpec((n_ctx, QK_COLS), lambda b: (blk0 + b, 0)),
            pl.BlockSpec((n_ctx, QK_COLS), lambda b: (blk0 + b, 0)),
            pl.BlockSpec((n_ctx, V_COLS), lambda b: (blk0 + b, 0)),
        ],
        out_specs=pl.BlockSpec((n_ctx, V_COLS), lambda b: (b, 0)),
        out_shape=jax.ShapeDtypeStruct((n_batch * n_ctx, V_COLS), F32),
        compiler_params=pltpu.CompilerParams(dimension_semantics=("arbitrary",), vmem_limit_bytes=VMEM_LIMIT),
        name="attn_context",
    )(q, k, v)


def _out_kernel(hy_ref, gm_ref, cv_ref, at_ref, x_ref, mg_ref, wo_ref, g1_ref, a2_ref, sh2_ref, wr_ref,
                xm_ref, hf_ref, lg_ref):
    o = None
    for g, y_ref in enumerate((hy_ref, gm_ref, cv_ref, at_ref)):
        y = y_ref[...]
        n = y * lax.rsqrt(jnp.mean(y * y, axis=-1, keepdims=True) + EPS) * mg_ref[:, g * D_GROUP:(g + 1) * D_GROUP]
        part = jnp.dot(n.astype(BF16), wo_ref[g * D_GROUP:(g + 1) * D_GROUP, :], preferred_element_type=F32)
        o = part if o is None else o + part
    xm = x_ref[...] + g1_ref[...] * o
    xm_ref[...] = xm
    hf = xm * lax.rsqrt(jnp.mean(xm * xm, axis=-1, keepdims=True) + EPS) * a2_ref[...] + sh2_ref[...]
    hf_ref[...] = hf.astype(BF16)
    lg_ref[...] = jnp.dot(hf, wr_ref[...], preferred_element_type=F32, precision=lax.Precision.HIGHEST)


def _out_call(ys, xt, mixg, wo, g1, a2, sh2, wr, n_batch, n_lat, n_rows):
    D = xt.shape[1]
    tm = TOK_TILE
    mod = _mod_row(n_lat // tm, n_batch)
    const = lambda i: (0, 0)
    tok = lambda i: (i, 0)
    return pl.pallas_call(
        _out_kernel,
        grid=(n_rows // tm,),
        in_specs=[pl.BlockSpec((tm, D_GROUP), tok)] * 4 + [
            pl.BlockSpec((tm, D), tok),
            pl.BlockSpec(mixg.shape, const),
            pl.BlockSpec(wo.shape, const),
            pl.BlockSpec((None, 1, D), mod),
            pl.BlockSpec((None, 1, D), mod),
            pl.BlockSpec((None, 1, D), mod),
            pl.BlockSpec(wr.shape, const),
        ],
        out_specs=[pl.BlockSpec((tm, D), tok), pl.BlockSpec((tm, D), tok), pl.BlockSpec((tm, LANES), tok)],
        out_shape=[jax.ShapeDtypeStruct((n_rows, D), F32), jax.ShapeDtypeStruct((n_rows, D), BF16),
                   jax.ShapeDtypeStruct((n_rows, LANES), F32)],
        compiler_params=pltpu.CompilerParams(dimension_semantics=("arbitrary",), vmem_limit_bytes=VMEM_LIMIT),
        name="mix_out_norm2",
    )(*ys, xt, mixg, wo, g1, a2, sh2, wr)


def _route(logits, router_bias):
    s = jax.nn.sigmoid(logits)
    sel = s + router_bias.astype(F32)
    grp = sel.reshape(sel.shape[0], N_EXPERT_GROUPS, EXPERTS_PER_GROUP)
    grp_score = jnp.sum(lax.top_k(grp, GROUP_SCORE_TOPK)[0], axis=-1)
    best = jnp.argmax(grp_score, axis=-1)
    in_group = (jnp.arange(N_EXPERTS) // EXPERTS_PER_GROUP) == best[..., None]
    _, idx = lax.top_k(jnp.where(in_group, sel, -jnp.inf), TOP_K)
    w = jnp.take_along_axis(s, idx, axis=-1)
    w = w / jnp.sum(w, axis=-1, keepdims=True)
    return idx.astype(jnp.int32), w


def _dispatch_plan(idx, w, tile):
    T = idx.shape[0]
    n_slots = T * TOP_K
    P = n_slots + N_EXPERTS * tile
    n_tiles = P // tile
    flat_e = idx.reshape(n_slots)
    order = jnp.argsort(flat_e, stable=True).astype(jnp.int32)
    onehot = (flat_e[:, None] == jnp.arange(N_EXPERTS, dtype=jnp.int32)[None, :]).astype(jnp.int32)
    csum = jnp.cumsum(onehot, axis=0)
    counts = csum[-1]
    rank = jnp.take_along_axis(csum, flat_e[:, None], axis=1)[:, 0] - 1
    padded = ((counts + tile - 1) // tile) * tile
    seg_end = jnp.cumsum(padded)
    seg_start = seg_end - padded
    src_start = jnp.cumsum(counts) - counts
    slot_row = (seg_start[flat_e] + rank).reshape(T, TOP_K)

    tile_first = jnp.arange(n_tiles, dtype=jnp.int32) * tile
    tile_expert = jnp.minimum(jnp.searchsorted(seg_end, tile_first, side='right'), N_EXPERTS - 1).astype(jnp.int32)
    tile_valid = (tile_first < seg_end[-1]).astype(jnp.int32)

    p = jnp.arange(P, dtype=jnp.int32)
    pe = tile_expert[p // tile]
    r = p - seg_start[pe]
    valid = (r < counts[pe]) & (p < seg_end[-1])
    src = jnp.clip(src_start[pe] + r, 0, n_slots - 1)
    slot = order[src]
    row_token = jnp.where(valid, slot // TOP_K, 0)
    row_gate = jnp.where(valid, w.reshape(n_slots)[slot], 0.0)
    return row_token, row_gate, tile_expert, tile_valid, slot_row


def _moe_tile_kernel(te_ref, tv_ref, x_ref, g_ref, wg_ref, wu_ref, wd_ref, o_ref):
    i = pl.program_id(0)

    @pl.when(tv_ref[i] != 0)
    def _():
        x = x_ref[...]
        hg = jnp.dot(x, wg_ref[...], preferred_element_type=F32)
        hu = jnp.dot(x, wu_ref[...], preferred_element_type=F32)
        a = (hg * jax.nn.sigmoid(hg) * hu).astype(BF16)
        y = jnp.dot(a, wd_ref[...], preferred_element_type=F32)
        o_ref[...] = y * g_ref[...]

    @pl.when(tv_ref[i] == 0)
    def _():
        o_ref[...] = jnp.zeros_like(o_ref)


def _moe_grouped(xs, gates, tile_expert, tile_valid, wg, wu, wd, tile):
    P, D = xs.shape
    n_tiles = P // tile
    F = wg.shape[-1]
    grid_spec = pltpu.PrefetchScalarGridSpec(
        num_scalar_prefetch=2,
        grid=(n_tiles,),
        in_specs=[
            pl.BlockSpec((tile, D), lambda i, te, tv: (i, 0)),
            pl.BlockSpec((tile, 1), lambda i, te, tv: (i, 0)),
            pl.BlockSpec((None, D, F), lambda i, te, tv: (te[i], 0, 0)),
            pl.BlockSpec((None, D, F), lambda i, te, tv: (te[i], 0, 0)),
            pl.BlockSpec((None, F, D), lambda i, te, tv: (te[i], 0, 0)),
        ],
        out_specs=pl.BlockSpec((tile, D), lambda i, te, tv: (i, 0)),
    )
    return pl.pallas_call(
        _moe_tile_kernel,
        grid_spec=grid_spec,
        out_shape=jax.ShapeDtypeStruct((P, D), F32),
        compiler_params=pltpu.CompilerParams(dimension_semantics=("arbitrary",), vmem_limit_bytes=VMEM_LIMIT),
        name="moe_grouped",
    )(tile_expert, tile_valid, xs, gates, wg, wu, wd)


def _moe_ffn(hf, logits, router_bias, wg, wu, wd):
    idx, w = _route(logits, router_bias)
    row_token, row_gate, tile_expert, tile_valid, slot_row = _dispatch_plan(idx, w, MOE_TILE)
    xs = jnp.take(hf, row_token, axis=0)
    ys = _moe_grouped(xs, row_gate[:, None], tile_expert, tile_valid, wg, wu, wd, MOE_TILE)
    return jnp.take(ys, slot_row[:, 0], axis=0) + jnp.take(ys, slot_row[:, 1], axis=0)


def kernel(x, c, ctx, c_ctx, ada_w, ada_b, norm1_g, norm2_g, w_in, hy_short_w, hy_short_b, hy_f_w1, hy_f_b1, hy_f_freq, hy_f_w2, hy_f_b2, hy_f_w3, hy_bias, gm_ln_g, gm_ln_b, gm_ws, gm_bs, cv_dw_w, cv_dw_b, cv_ln_g, cv_ln_b, mla_qa_norm, w_uq, mla_kva_norm, w_ukv, mix_norm_g, w_out, w_router, router_bias, exp_w_gate, exp_w_up, exp_w_down, final_norm_g):
    B, n_lat, D = x.shape
    n_ctx = ctx.shape[1]
    T_lat, T_ctx = B * n_lat, B * n_ctx
    assert n_lat % TOK_TILE == 0 and T_ctx % TOK_TILE == 0 and n_lat % ATT_TILE == 0 and T_lat % n_ctx == 0

    rope_tabs = _rope_slot_tables(n_lat, TOK_TILE)
    wr_pad = jnp.pad(w_router.astype(F32), ((0, 0), (0, LANES - N_EXPERTS)))
    cond = jnp.concatenate([c, c_ctx[None, :]], axis=0)
    xt = jnp.concatenate([x.reshape(T_lat, D), ctx.reshape(T_ctx, D)], axis=0)

    for l in range(DEPTH):
        last = l == DEPTH - 1
        m = jax.nn.silu(cond) @ ada_w[l] + ada_b[l]
        sh1, sc1, g1, sh2, sc2, g2 = [t[:, None, :] for t in jnp.split(m, 6, axis=-1)]
        a1 = norm1_g[l][None, None, :] * (1.0 + sc1)
        a2 = norm2_g[l][None, None, :] * (1.0 + sc2)
        qg = jnp.pad(mla_qa_norm[l], (0, PQ_PAD - MLA_Q_RANK))[None, :]
        kvg = mla_kva_norm[l][None, :]

        p_hy, p_gm, p_cv, q, k, v = _proj_call(
            xt, a1, sh1, _pack_w_in(w_in[l]), qg, _pack_w_uq(w_uq[l]), kvg, _pack_w_ukv(w_ukv[l]),
            rope_tabs, B, n_lat)

        y_at = _attn_call(q, k, v, B, n_lat, n_ctx)
        n_rows = T_lat if last else T_lat + T_ctx
        if not last:
            y_at = jnp.concatenate([y_at, _attn_ctx_call(q, k, v, B, n_lat, n_ctx)], axis=0)

        filt = (hy_f_w1[l], hy_f_b1[l], hy_f_freq[l], hy_f_w2[l], hy_f_b2[l], hy_f_w3[l])
        segs = [(0, T_lat, n_lat)] + ([] if last else [(T_lat, T_lat + T_ctx, n_ctx)])
        y_hy, y_gm, y_cv = [], [], []
        for lo, hi, L in segs:
            spectra = _hyena_filter_spectra(L, *filt)
            y_hy.append(_hyena_mixer(p_hy[lo:hi].reshape(B, L, HY_COLS), hy_short_w[l], hy_short_b[l],
                                     spectra, hy_bias[l]).reshape(hi - lo, D_GROUP))
            y_gm.append(_gmlp_mixer(p_gm[lo:hi].reshape(B, L, GM_COLS), gm_ln_g[l], gm_ln_b[l], gm_ws[l],
                                    gm_bs[l]).reshape(hi - lo, D_GROUP))
            y_cv.append(_conv_mixer(p_cv[lo:hi].reshape(B, L, CV_COLS), cv_dw_w[l], cv_dw_b[l], cv_ln_g[l],
                                    cv_ln_b[l]).reshape(hi - lo, D_GROUP))
        cat = lambda parts: parts[0] if len(parts) == 1 else jnp.concatenate(parts, axis=0)

        x_mid, hf, logits = _out_call(
            (cat(y_hy), cat(y_gm), cat(y_cv), y_at), xt, mix_norm_g[l][None, :], w_out[l].astype(BF16),
            g1, a2, sh2, wr_pad, B, n_lat, n_rows)

        y = _moe_ffn(hf, logits[:, :N_EXPERTS], router_bias,
                     exp_w_gate[l].astype(BF16), exp_w_up[l].astype(BF16), exp_w_down[l].astype(BF16))
        g2_rows = jnp.repeat(g2[:B, 0, :], n_lat, axis=0) if last else jnp.concatenate(
            [jnp.repeat(g2[:B, 0, :], n_lat, axis=0), jnp.broadcast_to(g2[B], (T_ctx, D))], axis=0)
        xt = x_mid + g2_rows * y
    return _rms_norm(xt[:T_lat], final_norm_g).reshape(B, n_lat, D)
```

```python
import functools
import math

import jax
import jax.numpy as jnp
import numpy as np
from jax import lax
from jax.experimental import pallas as pl
from jax.experimental.pallas import tpu as pltpu

F32 = jnp.float32
BF16 = jnp.bfloat16

D_MODEL = 1024
DEPTH = 2
GRID_W = 64
EPS = 1e-6

D_GROUP = 256
N_MIXERS = 4
HY_ORDER = 2
HY_SHORT = 3
HY_EMB = 33
HY_BANDS = (HY_EMB - 1) // 2
HY_TARGET = 1e-2
HY_FAST_PCT = 0.3
HY_SLOW_PCT = 1.5
GM_CHUNK = 128
GM_HEADS = 4
CV_WIDTH = 31
CV_GROUPS = 4
MLA_HEADS = 4
MLA_NOPE = 64
MLA_ROPE = 32
MLA_V = 64
MLA_Q_RANK = 192
MLA_KV_RANK = 128
ROPE_BASE = 10000.0
N_EXPERTS = 16
N_EXPERT_GROUPS = 4
EXPERTS_PER_GROUP = N_EXPERTS // N_EXPERT_GROUPS
TOP_K = 2
D_EXPERT = 512

HY_COLS = (HY_ORDER + 1) * D_GROUP
GM_COLS = 2 * D_GROUP
CV_COLS = 2 * D_GROUP
MQ_COLS = MLA_Q_RANK
MKV_COLS = MLA_KV_RANK + MLA_ROPE
HY_OFF = 0
GM_OFF = HY_OFF + HY_COLS
CV_OFF = GM_OFF + GM_COLS
MQ_OFF = CV_OFF + CV_COLS
MKV_OFF = MQ_OFF + MQ_COLS
IN_COLS = MKV_OFF + MKV_COLS

LANES = 128
SUBLANES = 8
HEAD_SLOT = LANES
QK_COLS = MLA_HEADS * HEAD_SLOT
V_COLS = MLA_HEADS * MLA_V
PQ_OFF = MQ_OFF
PQ_PAD = 256
PKV_OFF = PQ_OFF + PQ_PAD
PKR_OFF = PKV_OFF + MLA_KV_RANK
PROJ_COLS = PKR_OFF + LANES

PAIR_LO = (0, 0, 0, 1, 1, 2)
PAIR_HI = (1, 2, 3, 2, 3, 3)
N_PAIRS = len(PAIR_LO)
N_CLASSES = N_EXPERT_GROUPS * N_PAIRS

ATT_TILE = 512
MOE_TILE = 256
DFT_ROWS = 256
DFT_COLS = 1024
CONV_HALO = 16
VMEM_LIMIT = 56 * 1024 * 1024


def _cparams(n_axes):
    return pltpu.CompilerParams(dimension_semantics=("arbitrary",) * n_axes, vmem_limit_bytes=VMEM_LIMIT)


def _axial_rope_tables(n_lat):
    rows = n_lat // GRID_W
    row = jnp.repeat(jnp.arange(rows), GRID_W).astype(F32)
    col = jnp.tile(jnp.arange(GRID_W), rows).astype(F32)
    n_freq = MLA_ROPE // 4
    inv = ROPE_BASE ** (-jnp.arange(n_freq, dtype=F32) / n_freq)
    ang = jnp.concatenate([row[:, None] * inv, col[:, None] * inv], axis=-1)
    return jnp.cos(ang), jnp.sin(ang)


def _hyena_filter_spectra(L, w1, b1, freq, w2, b2, w3):
    w1, b1, freq, w2, b2, w3 = (a.astype(F32) for a in (w1, b1, freq, w2, b2, w3))
    t = jnp.linspace(0.0, 1.0, L, dtype=F32)[:, None]
    w = 2.0 * math.pi * jnp.arange(L, dtype=F32)[:, None] / L
    f = jnp.linspace(1e-4, HY_BANDS - 1, HY_BANDS, dtype=F32)[None, :]
    z = jnp.concatenate([t, jnp.cos(f * w), -jnp.sin(f * w)], axis=-1)
    h = jnp.sin(freq * (z @ w1 + b1))
    h = jnp.sin(freq * (h @ w2 + b2))
    h = (h @ w3).reshape(L, HY_ORDER, 2, D_GROUP)
    deltas = jnp.abs(jnp.linspace(math.log(HY_TARGET) / HY_SLOW_PCT,
                                  math.log(HY_TARGET) / HY_FAST_PCT, D_GROUP, dtype=F32))
    h = h * jnp.exp(-t * deltas)[:, None, None, :]
    kf, kb = h[:, :, 0], h[:, :, 1]
    k2 = jnp.concatenate([kf, jnp.zeros_like(kf[:1]), kb[1:][::-1]], axis=0)
    k2 = k2 / jnp.sum(jnp.abs(k2), axis=0, keepdims=True)
    return jnp.fft.rfft(k2, n=2 * L, axis=0)


def _pack_w_in(w_in):
    D = w_in.shape[0]
    z = lambda n: jnp.zeros((D, n), w_in.dtype)
    return jnp.concatenate([
        w_in[:, :MQ_OFF],
        w_in[:, MQ_OFF:MKV_OFF], z(PQ_PAD - MQ_COLS),
        w_in[:, MKV_OFF:MKV_OFF + MLA_KV_RANK],
        w_in[:, MKV_OFF + MLA_KV_RANK:], z(LANES - MLA_ROPE),
    ], axis=1).astype(BF16)


def _pack_w_uq(w_uq):
    w = w_uq.reshape(MLA_Q_RANK, MLA_HEADS, MLA_NOPE + MLA_ROPE)
    w = jnp.pad(w, ((0, PQ_PAD - MLA_Q_RANK), (0, 0), (0, HEAD_SLOT - MLA_NOPE - MLA_ROPE)))
    return w.reshape(PQ_PAD, QK_COLS).astype(BF16)


def _pack_w_ukv(w_ukv):
    w = w_ukv.reshape(MLA_KV_RANK, MLA_HEADS, MLA_NOPE + MLA_V)
    k_part = jnp.pad(w[:, :, :MLA_NOPE], ((0, 0), (0, 0), (0, HEAD_SLOT - MLA_NOPE))).reshape(MLA_KV_RANK, QK_COLS)
    v_part = w[:, :, MLA_NOPE:].reshape(MLA_KV_RANK, V_COLS)
    top = jnp.concatenate([k_part, v_part], axis=1)
    eye = jnp.eye(LANES, dtype=w_ukv.dtype)[:, :MLA_ROPE]
    place = jnp.pad(eye, ((0, 0), (MLA_NOPE, HEAD_SLOT - MLA_NOPE - MLA_ROPE)))
    bot = jnp.concatenate([jnp.tile(place, (1, MLA_HEADS)), jnp.zeros((LANES, V_COLS), w_ukv.dtype)], axis=1)
    return jnp.concatenate([top, bot], axis=0).astype(BF16)


def _rope_slot_tables(n_lat):
    cos, sin = _axial_rope_tables(n_lat)
    half = MLA_ROPE // 2
    tail_w = HEAD_SLOT - MLA_NOPE - MLA_ROPE
    ones = lambda n: jnp.ones((n_lat, n), F32)
    zeros = lambda n: jnp.zeros((n_lat, n), F32)
    cf = jnp.concatenate([ones(MLA_NOPE), cos, cos, ones(tail_w)], axis=1)
    s_up = jnp.concatenate([zeros(MLA_NOPE + half), sin, zeros(tail_w)], axis=1)
    s_dn = jnp.concatenate([zeros(MLA_NOPE), -sin, zeros(half + tail_w)], axis=1)
    return cf, s_up, s_dn


def _identity_slot_tables(rows):
    return (jnp.ones((rows, HEAD_SLOT), F32), jnp.zeros((rows, HEAD_SLOT), F32), jnp.zeros((rows, HEAD_SLOT), F32))


def _rotate_slots(t, cf, s_up, s_dn):
    half = MLA_ROPE // 2
    outs = []
    for h in range(MLA_HEADS):
        tb = t[:, h * HEAD_SLOT:(h + 1) * HEAD_SLOT]
        outs.append(tb * cf + pltpu.roll(tb, half, 1) * s_up + pltpu.roll(tb, HEAD_SLOT - half, 1) * s_dn)
    return jnp.concatenate(outs, axis=1)


def _proj_kernel(x_ref, a_ref, sh_ref, win_ref, qg_ref, wuq_ref, kvg_ref, wkv_ref, cf_ref, su_ref, sd_ref,
                 hy_ref, gm_ref, cv_ref, q_ref, k_ref, v_ref):
    x = x_ref[...]
    ms = jnp.mean(x * x, axis=-1, keepdims=True)
    h = x * lax.rsqrt(ms + EPS) * a_ref[...] + sh_ref[...]
    p = jnp.dot(h.astype(BF16), win_ref[...], preferred_element_type=F32)
    hy_ref[...] = p[:, HY_OFF:GM_OFF]
    gm_ref[...] = p[:, GM_OFF:CV_OFF]
    cv_ref[...] = p[:, CV_OFF:MQ_OFF]

    cf, su, sd = cf_ref[...], su_ref[...], sd_ref[...]
    cq = p[:, PQ_OFF:PQ_OFF + PQ_PAD]
    qn = cq * lax.rsqrt(jnp.sum(cq * cq, axis=-1, keepdims=True) * (1.0 / MLA_Q_RANK) + EPS) * qg_ref[...]
    q = jnp.dot(qn.astype(BF16), wuq_ref[...], preferred_element_type=F32)
    q = _rotate_slots(q, cf, su, sd) * (1.0 / math.sqrt(MLA_NOPE + MLA_ROPE))
    q_ref[...] = q.astype(BF16)

    ckv = p[:, PKV_OFF:PKV_OFF + MLA_KV_RANK]
    kvn = ckv * lax.rsqrt(jnp.mean(ckv * ckv, axis=-1, keepdims=True) + EPS) * kvg_ref[...]
    kin = jnp.concatenate([kvn, p[:, PKR_OFF:PKR_OFF + LANES]], axis=1).astype(BF16)
    kv = jnp.dot(kin, wkv_ref[...], preferred_element_type=F32)
    k_ref[...] = _rotate_slots(kv[:, :QK_COLS], cf, su, sd).astype(BF16)
    v_ref[...] = kv[:, QK_COLS:].astype(BF16)


def _proj_call(xt, mod_a, mod_sh, mod_map, win, qg, wuq, kvg, wkv, rope_tabs, rope_map, tm):
    T, D = xt.shape
    const = lambda i: (0, 0)
    tok = lambda i: (i, 0)
    out_cols = (HY_COLS, GM_COLS, CV_COLS, QK_COLS, QK_COLS, V_COLS)
    out_dtypes = (F32, F32, F32, BF16, BF16, BF16)
    return pl.pallas_call(
        _proj_kernel,
        grid=(T // tm,),
        in_specs=[
            pl.BlockSpec((tm, D), tok),
            pl.BlockSpec((None, 1, D), mod_map),
            pl.BlockSpec((None, 1, D), mod_map),
            pl.BlockSpec(win.shape, const),
            pl.BlockSpec(qg.shape, const),
            pl.BlockSpec(wuq.shape, const),
            pl.BlockSpec(kvg.shape, const),
            pl.BlockSpec(wkv.shape, const),
            pl.BlockSpec((tm, HEAD_SLOT), rope_map),
            pl.BlockSpec((tm, HEAD_SLOT), rope_map),
            pl.BlockSpec((tm, HEAD_SLOT), rope_map),
        ],
        out_specs=[pl.BlockSpec((tm, n), tok) for n in out_cols],
        out_shape=[jax.ShapeDtypeStruct((T, n), dt) for n, dt in zip(out_cols, out_dtypes)],
        compiler_params=_cparams(1),
        name="proj_qkv",
    )(xt, mod_a, mod_sh, win, qg, wuq, kvg, wkv, *rope_tabs)


def _attend_heads(q_ref, key_refs, val_refs, o_ref):
    nt = (((1,), (1,)), ((), ()))
    lane = lax.broadcasted_iota(jnp.int32, (1, V_COLS), 1)
    vals = [v_ref[...] for v_ref in val_refs]
    acc = jnp.zeros(o_ref.shape, F32)
    for h in range(MLA_HEADS):
        sl_h = slice(h * HEAD_SLOT, (h + 1) * HEAD_SLOT)
        q = q_ref[:, sl_h]
        scores = [lax.dot_general(q, k_ref[:, sl_h], nt, preferred_element_type=F32) for k_ref in key_refs]
        m = functools.reduce(jnp.maximum, [jnp.max(s, axis=-1, keepdims=True) for s in scores])
        probs = [jnp.exp(s - m) for s in scores]
        denom = functools.reduce(jnp.add, [jnp.sum(p, axis=-1, keepdims=True) for p in probs])
        o = functools.reduce(jnp.add, [jnp.dot(p.astype(BF16), v, preferred_element_type=F32)
                                       for p, v in zip(probs, vals)])
        in_head = (lane >= h * MLA_V) & (lane < (h + 1) * MLA_V)
        acc = acc + jnp.where(in_head, o / denom, 0.0)
    o_ref[...] = acc


def _attn_lat_kernel(q_ref, kl_ref, kc_ref, vl_ref, vc_ref, o_ref):
    _attend_heads(q_ref, (kl_ref, kc_ref), (vl_ref, vc_ref), o_ref)


def _attn_ctx_kernel(q_ref, k_ref, v_ref, o_ref):
    _attend_heads(q_ref, (k_ref,), (v_ref,), o_ref)


def _attn_lat_call(q, k, v, k_ctx, v_ctx, n_batch, n_lat, n_ctx):
    tq = ATT_TILE
    qt = n_lat // tq
    return pl.pallas_call(
        _attn_lat_kernel,
        grid=(n_batch, qt),
        in_specs=[
            pl.BlockSpec((tq, QK_COLS), lambda b, j: (b * qt + j, 0)),
            pl.BlockSpec((n_lat, QK_COLS), lambda b, j: (b, 0)),
            pl.BlockSpec((n_ctx, QK_COLS), lambda b, j: (b, 0)),
            pl.BlockSpec((n_lat, V_COLS), lambda b, j: (b, 0)),
            pl.BlockSpec((n_ctx, V_COLS), lambda b, j: (b, 0)),
        ],
        out_specs=pl.BlockSpec((tq, V_COLS), lambda b, j: (b * qt + j, 0)),
        out_shape=jax.ShapeDtypeStruct((n_batch * n_lat, V_COLS), F32),
        compiler_params=_cparams(2),
        name="attn_latent",
    )(q, k, k_ctx, v, v_ctx)


def _attn_ctx_call(q, k, v, n_batch, n_ctx):
    blk = lambda b: (b, 0)
    return pl.pallas_call(
        _attn_ctx_kernel,
        grid=(n_batch,),
        in_specs=[pl.BlockSpec((n_ctx, QK_COLS), blk), pl.BlockSpec((n_ctx, QK_COLS), blk),
                  pl.BlockSpec((n_ctx, V_COLS), blk)],
        out_specs=pl.BlockSpec((n_ctx, V_COLS), blk),
        out_shape=jax.ShapeDtypeStruct((n_batch * n_ctx, V_COLS), F32),
        compiler_params=_cparams(1),
        name="attn_context",
    )(q, k, v)


@functools.lru_cache(maxsize=None)
def _dft_matrices_np(L):
    idx = np.arange(L, dtype=np.int64)
    ang = (np.outer(idx, idx) % (2 * L)).astype(np.float64) * (np.pi / L)
    c = np.cos(ang)
    s = np.sin(ang)
    s[0, :] = np.where(idx % 2 == 0, 1.0, -1.0)
    to_bf16 = lambda a: a.astype(np.float32).astype(BF16)
    return to_bf16(c), to_bf16(s), to_bf16(s.T)


def _spectrum_tables(spec):
    L = spec.shape[0] - 1
    re, im = jnp.real(spec).astype(F32), jnp.imag(spec).astype(F32)
    first = (jnp.arange(L) == 0)[:, None]
    kp_r = jnp.where(first, 0.5, 1.0) * re[:L] / L
    kp_i = jnp.where(first, 0.0, im[:L] / L)
    kq_r = jnp.where(first, re[L:L + 1] / (2 * L), re[:L] / L)
    kq_i = jnp.where(first, 0.0, im[:L] / L)
    return kp_r, kp_i, kq_r, kq_i


def _hy_prep_kernel(p_ref, w_ref, b_ref, z_ref, v16_ref):
    j = pl.program_id(1)
    p = p_ref[...]
    L = p.shape[0]
    row = lax.broadcasted_iota(jnp.int32, (L, 1), 0)
    prev = jnp.where(row == 0, 0.0, pltpu.roll(p, 1, 0))
    nxt = jnp.where(row == L - 1, 0.0, pltpu.roll(p, L - 1, 0))
    z = prev * w_ref[0:1, :] + p * w_ref[1:2, :] + nxt * w_ref[2:3, :] + b_ref[...]
    z_ref[...] = z

    @pl.when(j == HY_ORDER)
    def _():
        v16_ref[...] = z.astype(BF16)


def _hy_prep_call(p_hy, short_w, short_b, n_batch, L):
    N = n_batch * D_GROUP
    return pl.pallas_call(
        _hy_prep_kernel,
        grid=(n_batch, HY_ORDER + 1),
        in_specs=[
            pl.BlockSpec((L, D_GROUP), lambda b, j: (b, j)),
            pl.BlockSpec((HY_SHORT, D_GROUP), lambda b, j: (0, j)),
            pl.BlockSpec((1, D_GROUP), lambda b, j: (0, j)),
        ],
        out_specs=[pl.BlockSpec((None, L, D_GROUP), lambda b, j: (j, 0, b)),
                   pl.BlockSpec((L, D_GROUP), lambda b, j: (0, b))],
        out_shape=[jax.ShapeDtypeStruct((HY_ORDER + 1, L, N), F32), jax.ShapeDtypeStruct((L, N), BF16)],
        compiler_params=_cparams(2),
        name="hyena_short_conv",
    )(p_hy, short_w, short_b[None, :])


def _dft_fwd_kernel(c_ref, s_ref, u_ref, kpr_ref, kpi_ref, kqr_ref, kqi_ref, p_ref, q_ref):
    u = u_ref[...]
    a = jnp.dot(c_ref[...], u, preferred_element_type=F32)
    b = jnp.dot(s_ref[...], u, preferred_element_type=F32)
    kpr, kpi, kqr, kqi = kpr_ref[...], kpi_ref[...], kqr_ref[...], kqi_ref[...]
    for g in range(u.shape[1] // D_GROUP):
        sl = slice(g * D_GROUP, (g + 1) * D_GROUP)
        p_ref[:, sl] = (kpr * a[:, sl] + kpi * b[:, sl]).astype(BF16)
        q_ref[:, sl] = (kqr * b[:, sl] - kqi * a[:, sl]).astype(BF16)


def _dft_fwd_call(cmat, smat, u16, tabs):
    L, N = u16.shape
    tr = min(DFT_ROWS, L)
    tn = min(DFT_COLS, N)
    row = lambda h, i: (i, 0)
    return pl.pallas_call(
        _dft_fwd_kernel,
        grid=(N // tn, L // tr),
        in_specs=[
            pl.BlockSpec((tr, L), row),
            pl.BlockSpec((tr, L), row),
            pl.BlockSpec((L, tn), lambda h, i: (0, h), pipeline_mode=pl.Buffered(1)),
        ] + [pl.BlockSpec((tr, D_GROUP), row)] * 4,
        out_specs=[pl.BlockSpec((tr, tn), lambda h, i: (i, h))] * 2,
        out_shape=[jax.ShapeDtypeStruct((L, N), BF16)] * 2,
        compiler_params=_cparams(2),
        name="hyena_dft_fwd",
    )(cmat, smat, u16, *tabs)


def _dft_inv_kernel(c_ref, st_ref, p_ref, q_ref, u_ref, g_ref, bias_ref, y_ref, y16_ref):
    conv = (jnp.dot(c_ref[...], p_ref[...], preferred_element_type=F32)
            + jnp.dot(st_ref[...], q_ref[...], preferred_element_type=F32))
    y = g_ref[...] * (conv + u_ref[...] * bias_ref[...])
    y_ref[...] = y
    y16_ref[...] = y.astype(BF16)


def _dft_inv_call(cmat, stmat, p16, q16, u_arr, u_sel, g_arr, g_sel, bias_row):
    L, N = p16.shape
    tr = min(DFT_ROWS, L)
    tn = min(DFT_COLS, N)
    row = lambda h, i: (i, 0)
    res = lambda h, i: (0, h)
    return pl.pallas_call(
        _dft_inv_kernel,
        grid=(N // tn, L // tr),
        in_specs=[
            pl.BlockSpec((tr, L), row),
            pl.BlockSpec((tr, L), row),
            pl.BlockSpec((L, tn), res, pipeline_mode=pl.Buffered(1)),
            pl.BlockSpec((L, tn), res, pipeline_mode=pl.Buffered(1)),
            pl.BlockSpec((None, tr, tn), lambda h, i: (u_sel, i, h)),
            pl.BlockSpec((None, tr, tn), lambda h, i: (g_sel, i, h)),
            pl.BlockSpec((1, tn), res),
        ],
        out_specs=[pl.BlockSpec((tr, tn), lambda h, i: (i, h))] * 2,
        out_shape=[jax.ShapeDtypeStruct((L, N), F32), jax.ShapeDtypeStruct((L, N), BF16)],
        compiler_params=_cparams(2),
        name="hyena_dft_inv",
    )(cmat, stmat, p16, q16, u_arr, g_arr, bias_row)


def _hyena_call(p_hy, short_w, short_b, spectra, hy_bias, n_batch, L):
    cmat, smat, stmat = (jnp.asarray(m) for m in _dft_matrices_np(L))
    z, u16 = _hy_prep_call(p_hy, short_w, short_b, n_batch, L)
    y_stack = z
    u_sel = HY_ORDER
    for n in range(HY_ORDER):
        tabs = _spectrum_tables(spectra[:, n])
        p16, q16 = _dft_fwd_call(cmat, smat, u16, tabs)
        bias_row = jnp.tile(hy_bias[n][None, :].astype(F32), (1, n_batch))
        y, u16 = _dft_inv_call(cmat, stmat, p16, q16, y_stack, u_sel, z, n, bias_row)
        y_stack, u_sel = y[None], 0
    return y


def _mixers_kernel(gm_ref, cv_ref, cvp_ref, cvn_ref, lng_ref, lnb_ref, ws_ref, bsf_ref, dww_ref, dwb_ref,
                   cg_ref, cb_ref, avg_ref, ygm_ref, ycv_ref, glu_ref, *, tiles_per_seq):
    i = pl.program_id(0)
    tm = gm_ref.shape[0]
    lane = lax.broadcasted_iota(jnp.int32, (1, D_GROUP), 1)

    z = jax.nn.gelu(gm_ref[...], approximate=True)
    u, v = z[:, :D_GROUP], z[:, D_GROUP:]
    mu = jnp.mean(v, axis=-1, keepdims=True)
    vc = v - mu
    var = jnp.mean(vc * vc, axis=-1, keepdims=True)
    vn = (vc * lax.rsqrt(var + EPS) * lng_ref[...] + lnb_ref[...]).astype(BF16)
    hd = D_GROUP // GM_HEADS
    for c in range(tm // GM_CHUNK):
        rows = slice(c * GM_CHUNK, (c + 1) * GM_CHUNK)
        s = bsf_ref[...]
        for g in range(GM_HEADS):
            sg = jnp.dot(ws_ref[g], vn[rows, :], preferred_element_type=F32)
            s = s + jnp.where((lane >= g * hd) & (lane < (g + 1) * hd), sg, 0.0)
        ygm_ref[rows, :] = u[rows, :] * s

    def glu(t):
        return t[:, :D_GROUP] * jax.nn.sigmoid(t[:, D_GROUP:])

    first = (i % tiles_per_seq) == 0
    last = (i % tiles_per_seq) == tiles_per_seq - 1
    glu_ref[0:CONV_HALO, :] = jnp.where(first, 0.0, glu(cvp_ref[...]))
    glu_ref[CONV_HALO:CONV_HALO + tm, :] = glu(cv_ref[...])
    glu_ref[CONV_HALO + tm:, :] = jnp.where(last, 0.0, glu(cvn_ref[...]))
    pad = (CV_WIDTH - 1) // 2
    rc = 128
    for c in range(tm // rc):
        acc = jnp.zeros((rc, D_GROUP), F32) + dwb_ref[...]
        for k in range(CV_WIDTH):
            start = c * rc + CONV_HALO - pad + k
            acc = acc + glu_ref[start:start + rc, :] * dww_ref[k:k + 1, :]
        gmean = jnp.dot(acc, avg_ref[...], preferred_element_type=F32, precision=lax.Precision.HIGHEST)
        d = acc - gmean
        gvar = jnp.dot(d * d, avg_ref[...], preferred_element_type=F32, precision=lax.Precision.HIGHEST)
        n = d * lax.rsqrt(gvar + EPS) * cg_ref[...] + cb_ref[...]
        ycv_ref[c * rc:(c + 1) * rc, :] = n * jax.nn.sigmoid(n)


def _mixers_call(p_gm, p_cv, gm_ln_g, gm_ln_b, gm_ws, gm_bs, cv_dw_w, cv_dw_b, cv_ln_g, cv_ln_b, L, tm):
    T = p_gm.shape[0]
    tps = L // tm
    hb = tm // CONV_HALO
    n_hblk = T // CONV_HALO
    const2 = lambda i: (0, 0)
    tok = lambda i: (i, 0)
    bs_full = jnp.repeat(gm_bs.T.astype(F32), D_GROUP // GM_HEADS, axis=1)
    gid = np.arange(D_GROUP) // (D_GROUP // CV_GROUPS)
    avg = jnp.asarray((gid[:, None] == gid[None, :]).astype(np.float32) / (D_GROUP // CV_GROUPS))
    row = lambda a: a[None, :].astype(F32)
    return pl.pallas_call(
        functools.partial(_mixers_kernel, tiles_per_seq=tps),
        grid=(T // tm,),
        in_specs=[
            pl.BlockSpec((tm, GM_COLS), tok),
            pl.BlockSpec((tm, CV_COLS), tok),
            pl.BlockSpec((CONV_HALO, CV_COLS), lambda i: (jnp.maximum(i * hb - 1, 0), 0)),
            pl.BlockSpec((CONV_HALO, CV_COLS), lambda i: (jnp.minimum((i + 1) * hb, n_hblk - 1), 0)),
            pl.BlockSpec((1, D_GROUP), const2),
            pl.BlockSpec((1, D_GROUP), const2),
            pl.BlockSpec((GM_HEADS, GM_CHUNK, GM_CHUNK), lambda i: (0, 0, 0)),
            pl.BlockSpec((GM_CHUNK, D_GROUP), const2),
            pl.BlockSpec((CV_WIDTH, D_GROUP), const2),
            pl.BlockSpec((1, D_GROUP), const2),
            pl.BlockSpec((1, D_GROUP), const2),
            pl.BlockSpec((1, D_GROUP), const2),
            pl.BlockSpec((D_GROUP, D_GROUP), const2),
        ],
        out_specs=[pl.BlockSpec((tm, D_GROUP), tok)] * 2,
        out_shape=[jax.ShapeDtypeStruct((T, D_GROUP), F32)] * 2,
        scratch_shapes=[pltpu.VMEM((tm + 2 * CONV_HALO, D_GROUP), F32)],
        compiler_params=_cparams(1),
        name="gmlp_conv_mixers",
    )(p_gm, p_cv, p_cv, p_cv, row(gm_ln_g), row(gm_ln_b), gm_ws.astype(BF16), bs_full, cv_dw_w.astype(F32),
      row(cv_dw_b), row(cv_ln_g), row(cv_ln_b), avg)


def _out_kernel(hy_ref, gm_ref, cv_ref, at_ref, x_ref, mg_ref, wo_ref, g1_ref, a2_ref, sh2_ref, wr_ref,
                xm_ref, hf_ref, lg_ref):
    o = None
    for g, y_ref in enumerate((hy_ref, gm_ref, cv_ref, at_ref)):
        y = y_ref[...]
        n = y * lax.rsqrt(jnp.mean(y * y, axis=-1, keepdims=True) + EPS) * mg_ref[:, g * D_GROUP:(g + 1) * D_GROUP]
        part = jnp.dot(n.astype(BF16), wo_ref[g * D_GROUP:(g + 1) * D_GROUP, :], preferred_element_type=F32)
        o = part if o is None else o + part
    xm = x_ref[...] + g1_ref[...] * o
    xm_ref[...] = xm
    hf = xm * lax.rsqrt(jnp.mean(xm * xm, axis=-1, keepdims=True) + EPS) * a2_ref[...] + sh2_ref[...]
    hf_ref[...] = hf.astype(BF16)
    lg_ref[...] = jnp.dot(hf, wr_ref[...], preferred_element_type=F32, precision=lax.Precision.HIGHEST)


def _out_call(y_hy_t, y_gm, y_cv, y_at, xt, mixg, wo, g1, a2, sh2, mod_map, wr, L, tm):
    T, D = xt.shape
    tps = L // tm
    const = lambda i: (0, 0)
    tok = lambda i: (i, 0)
    return pl.pallas_call(
        _out_kernel,
        grid=(T // tm,),
        in_specs=[pl.BlockSpec((tm, D_GROUP), lambda i: (i % tps, i // tps))] + [pl.BlockSpec((tm, D_GROUP), tok)] * 3 + [
            pl.BlockSpec((tm, D), tok),
            pl.BlockSpec(mixg.shape, const),
            pl.BlockSpec(wo.shape, const),
            pl.BlockSpec((None, 1, D), mod_map),
            pl.BlockSpec((None, 1, D), mod_map),
            pl.BlockSpec((None, 1, D), mod_map),
            pl.BlockSpec(wr.shape, const),
        ],
        out_specs=[pl.BlockSpec((tm, D), tok), pl.BlockSpec((tm, D), tok), pl.BlockSpec((tm, LANES), tok)],
        out_shape=[jax.ShapeDtypeStruct((T, D), F32), jax.ShapeDtypeStruct((T, D), BF16),
                   jax.ShapeDtypeStruct((T, LANES), F32)],
        compiler_params=_cparams(1),
        name="mix_out_norm2",
    )(y_hy_t, y_gm, y_cv, y_at, xt, mixg, wo, g1, a2, sh2, wr)


def _first_max_flags(vals):
    m = functools.reduce(jnp.maximum, vals)
    flags, taken = [], None
    for v in vals:
        f = v >= m
        if taken is not None:
            f = f & jnp.logical_not(taken)
        flags.append(f)
        taken = f if taken is None else taken | f
    return flags, m


def _pick(flags, vals):
    out = vals[-1]
    for f, v in zip(flags[-2::-1], vals[-2::-1]):
        out = jnp.where(f, v, out)
    return out


def _router_kernel(lg_ref, bias_ref, cls_ref, gate_ref):
    lt = jnp.transpose(lg_ref[...])
    s_all = jax.nn.sigmoid(lt[:N_EXPERTS, :])
    sel_all = s_all + bias_ref[...]
    s = [s_all[e:e + 1, :] for e in range(N_EXPERTS)]
    sel = [sel_all[e:e + 1, :] for e in range(N_EXPERTS)]
    neg = -jnp.inf
    E = EXPERTS_PER_GROUP

    def top2(vals):
        f1, m1 = _first_max_flags(vals)
        rest = [jnp.where(f, neg, v) for f, v in zip(f1, vals)]
        f2, m2 = _first_max_flags(rest)
        return f1, m1, f2, m2

    scores = []
    for g in range(N_EXPERT_GROUPS):
        _, m1, _, m2 = top2(sel[g * E:(g + 1) * E])
        scores.append(m1 + m2)
    gflags, _ = _first_max_flags(scores)
    bsel = [_pick(gflags, [sel[g * E + j] for g in range(N_EXPERT_GROUPS)]) for j in range(E)]
    bs = [_pick(gflags, [s[g * E + j] for g in range(N_EXPERT_GROUPS)]) for j in range(E)]
    f1, _, f2, _ = top2(bsel)
    zero = jnp.zeros_like(bs[0])
    w1 = functools.reduce(jnp.add, [jnp.where(f, v, zero) for f, v in zip(f1, bs)])
    w2 = functools.reduce(jnp.add, [jnp.where(f, v, zero) for f, v in zip(f2, bs)])
    izero = jnp.zeros(w1.shape, jnp.int32)
    j1 = functools.reduce(jnp.add, [jnp.where(f, j, izero) for j, f in enumerate(f1)])
    j2 = functools.reduce(jnp.add, [jnp.where(f, j, izero) for j, f in enumerate(f2)])
    gi = functools.reduce(jnp.add, [jnp.where(f, g, izero) for g, f in enumerate(gflags)])
    lo, hi = jnp.minimum(j1, j2), jnp.maximum(j1, j2)
    pair = jnp.where(lo == 0, 0, jnp.where(lo == 1, 3, 5)) + hi - lo - 1
    cls_ref[...] = gi * N_PAIRS + pair
    tot = w1 + w2
    first_is_lo = j1 < j2
    gate_ref[0:1, :] = jnp.where(first_is_lo, w1, w2) / tot
    gate_ref[1:2, :] = jnp.where(first_is_lo, w2, w1) / tot


def _router_call(logits, router_bias, tm):
    T = logits.shape[0]
    bias_col = router_bias.astype(F32)[:, None]
    return pl.pallas_call(
        _router_kernel,
        grid=(T // tm,),
        in_specs=[pl.BlockSpec((tm, LANES), lambda i: (i, 0)), pl.BlockSpec((N_EXPERTS, 1), lambda i: (0, 0))],
        out_specs=[pl.BlockSpec((1, tm), lambda i: (0, i)), pl.BlockSpec((TOP_K, tm), lambda i: (0, i))],
        out_shape=[jax.ShapeDtypeStruct((1, T), jnp.int32), jax.ShapeDtypeStruct((TOP_K, T), F32)],
        compiler_params=_cparams(1),
        name="router_top2",
    )(logits, bias_col)


def _class_plan(cls, tile):
    T = cls.shape[0]
    P = T + N_CLASSES * tile
    n_tiles = P // tile
    order = jnp.argsort(cls, stable=True).astype(jnp.int32)
    counts = jnp.sum((cls[:, None] == jnp.arange(N_CLASSES, dtype=jnp.int32)[None, :]).astype(jnp.int32), axis=0)
    padded = ((counts + tile - 1) // tile) * tile
    seg_end = jnp.cumsum(padded)
    seg_start = seg_end - padded
    src_start = jnp.cumsum(counts) - counts

    tile_first = jnp.arange(n_tiles, dtype=jnp.int32) * tile
    tile_class = jnp.minimum(jnp.searchsorted(seg_end, tile_first, side='right'), N_CLASSES - 1).astype(jnp.int32)
    tile_valid = (tile_first < seg_end[-1]).astype(jnp.int32)
    grp = tile_class // N_PAIRS
    pair = tile_class % N_PAIRS
    tile_lo = grp * EXPERTS_PER_GROUP + jnp.asarray(PAIR_LO, jnp.int32)[pair]
    tile_hi = grp * EXPERTS_PER_GROUP + jnp.asarray(PAIR_HI, jnp.int32)[pair]

    p = jnp.arange(P, dtype=jnp.int32)
    pc = tile_class[p // tile]
    r = p - seg_start[pc]
    row_valid = (r < counts[pc]) & (p < seg_end[-1])
    row_token = jnp.where(row_valid, order[jnp.clip(src_start[pc] + r, 0, T - 1)], 0)

    sorted_pos = jnp.zeros((T,), jnp.int32).at[order].set(jnp.arange(T, dtype=jnp.int32), unique_indices=True)
    tok_row = seg_start[cls] + sorted_pos - src_start[cls]
    return row_token, row_valid, tile_lo, tile_hi, tile_valid, tok_row


def _moe_pair_kernel(lo_ref, hi_ref, tv_ref, x_ref, g_ref, wga_ref, wua_ref, wda_ref, wgb_ref, wub_ref, wdb_ref,
                     o_ref):
    i = pl.program_id(0)

    @pl.when(tv_ref[i] != 0)
    def _():
        x = x_ref[...]
        g = g_ref[...]

        def hidden(wg_ref, wu_ref, gate):
            hg = jnp.dot(x, wg_ref[...], preferred_element_type=F32)
            hu = jnp.dot(x, wu_ref[...], preferred_element_type=F32)
            return (hg * jax.nn.sigmoid(hg) * hu * gate).astype(BF16)

        o_ref[...] = (jnp.dot(hidden(wga_ref, wua_ref, g[:, 0:1]), wda_ref[...], preferred_element_type=F32)
                      + jnp.dot(hidden(wgb_ref, wub_ref, g[:, 1:2]), wdb_ref[...], preferred_element_type=F32))

    @pl.when(tv_ref[i] == 0)
    def _():
        o_ref[...] = jnp.zeros_like(o_ref)


def _moe_pair_call(xs, gates, tile_lo, tile_hi, tile_valid, wg, wu, wd, tile):
    P, D = xs.shape
    F = wg.shape[-1]
    lo = lambda i, tl, th, tv: (tl[i], 0, 0)
    hi = lambda i, tl, th, tv: (th[i], 0, 0)
    tok = lambda i, tl, th, tv: (i, 0)
    grid_spec = pltpu.PrefetchScalarGridSpec(
        num_scalar_prefetch=3,
        grid=(P // tile,),
        in_specs=[
            pl.BlockSpec((tile, D), tok),
            pl.BlockSpec((tile, TOP_K), tok),
            pl.BlockSpec((None, D, F), lo), pl.BlockSpec((None, D, F), lo), pl.BlockSpec((None, F, D), lo),
            pl.BlockSpec((None, D, F), hi), pl.BlockSpec((None, D, F), hi), pl.BlockSpec((None, F, D), hi),
        ],
        out_specs=pl.BlockSpec((tile, D), tok),
    )
    return pl.pallas_call(
        _moe_pair_kernel,
        grid_spec=grid_spec,
        out_shape=jax.ShapeDtypeStruct((P, D), F32),
        compiler_params=_cparams(1),
        name="moe_pair_grouped",
    )(tile_lo, tile_hi, tile_valid, xs, gates, wg, wu, wd, wg, wu, wd)


def _moe_ffn(hf, logits, router_bias, wg, wu, wd, tm):
    cls, gates = _router_call(logits, router_bias, tm)
    row_token, row_valid, tile_lo, tile_hi, tile_valid, tok_row = _class_plan(cls[0], MOE_TILE)
    xs = jnp.take(hf, row_token, axis=0)
    gs = jnp.where(row_valid[:, None], jnp.take(gates, row_token, axis=1).T, 0.0)
    ys = _moe_pair_call(xs, gs, tile_lo, tile_hi, tile_valid, wg, wu, wd, MOE_TILE)
    return jnp.take(ys, tok_row, axis=0)


def _final_norm_kernel(x_ref, g_ref, o_ref):
    x = x_ref[...]
    o_ref[...] = x * lax.rsqrt(jnp.mean(x * x, axis=-1, keepdims=True) + EPS) * g_ref[...]


def _final_norm_call(xt, g, tm):
    T, D = xt.shape
    return pl.pallas_call(
        _final_norm_kernel,
        grid=(T // tm,),
        in_specs=[pl.BlockSpec((tm, D), lambda i: (i, 0)), pl.BlockSpec((1, D), lambda i: (0, 0))],
        out_specs=pl.BlockSpec((tm, D), lambda i: (i, 0)),
        out_shape=jax.ShapeDtypeStruct((T, D), F32),
        compiler_params=_cparams(1),
        name="final_rms_norm",
    )(xt, g[None, :].astype(F32))


def kernel(x, c, ctx, c_ctx, ada_w, ada_b, norm1_g, norm2_g, w_in, hy_short_w, hy_short_b, hy_f_w1, hy_f_b1, hy_f_freq, hy_f_w2, hy_f_b2, hy_f_w3, hy_bias, gm_ln_g, gm_ln_b, gm_ws, gm_bs, cv_dw_w, cv_dw_b, cv_ln_g, cv_ln_b, mla_qa_norm, w_uq, mla_kva_norm, w_ukv, mix_norm_g, w_out, w_router, router_bias, exp_w_gate, exp_w_up, exp_w_down, final_norm_g):
    B, n_lat, D = x.shape
    n_ctx = ctx.shape[1]
    T_lat, T_ctx = B * n_lat, B * n_ctx
    tm_lat = min(512, n_lat)
    tm_ctx = min(256, n_ctx)
    assert n_lat % tm_lat == 0 and n_ctx % tm_ctx == 0 and n_lat % ATT_TILE == 0
    assert tm_lat % GM_CHUNK == 0 and tm_ctx % GM_CHUNK == 0 and T_lat % MOE_TILE == 0 and T_ctx % MOE_TILE == 0

    tps_lat, tps_ctx = n_lat // tm_lat, n_ctx // tm_ctx
    rope_lat = _rope_slot_tables(n_lat)
    rope_ctx = _identity_slot_tables(tm_ctx)
    wr_pad = jnp.pad(w_router.astype(F32), ((0, 0), (0, LANES - N_EXPERTS)))
    cond = jnp.concatenate([c, c_ctx[None, :]], axis=0)
    streams = {
        'lat': dict(x=x.reshape(T_lat, D), L=n_lat, tm=tm_lat, mod_map=lambda i: (i // tps_lat, 0, 0),
                    rope=rope_lat, rope_map=lambda i: (i % tps_lat, 0)),
        'ctx': dict(x=ctx.reshape(T_ctx, D), L=n_ctx, tm=tm_ctx, mod_map=lambda i: (B, 0, 0),
                    rope=rope_ctx, rope_map=lambda i: (0, 0)),
    }

    for l in range(DEPTH):
        last = l == DEPTH - 1
        m = jax.nn.silu(cond) @ ada_w[l] + ada_b[l]
        sh1, sc1, g1, sh2, sc2, g2 = [t[:, None, :] for t in jnp.split(m, 6, axis=-1)]
        a1 = norm1_g[l][None, None, :] * (1.0 + sc1)
        a2 = norm2_g[l][None, None, :] * (1.0 + sc2)
        qg = jnp.pad(mla_qa_norm[l], (0, PQ_PAD - MLA_Q_RANK))[None, :].astype(F32)
        kvg = mla_kva_norm[l][None, :].astype(F32)
        win, wuq, wkv = _pack_w_in(w_in[l]), _pack_w_uq(w_uq[l]), _pack_w_ukv(w_ukv[l])
        filt = (hy_f_w1[l], hy_f_b1[l], hy_f_freq[l], hy_f_w2[l], hy_f_b2[l], hy_f_w3[l])
        wg, wu, wd = exp_w_gate[l].astype(BF16), exp_w_up[l].astype(BF16), exp_w_down[l].astype(BF16)

        proj = {name: _proj_call(s['x'], a1, sh1, s['mod_map'], win, qg, wuq, kvg, wkv, s['rope'], s['rope_map'],
                                 s['tm']) for name, s in streams.items()}
        k_ctx, v_ctx = proj['ctx'][4], proj['ctx'][5]

        active = ('lat',) if last else ('lat', 'ctx')
        x_mid, hf, logits = {}, {}, {}
        for name in active:
            s = streams[name]
            p_hy, p_gm, p_cv, q, k, v = proj[name]
            if name == 'lat':
                y_at = _attn_lat_call(q, k, v, k_ctx, v_ctx, B, n_lat, n_ctx)
            else:
                y_at = _attn_ctx_call(q, k, v, B, n_ctx)
            spectra = _hyena_filter_spectra(s['L'], *filt)
            y_hy_t = _hyena_call(p_hy, hy_short_w[l], hy_short_b[l], spectra, hy_bias[l], B, s['L'])
            y_gm, y_cv = _mixers_call(p_gm, p_cv, gm_ln_g[l], gm_ln_b[l], gm_ws[l], gm_bs[l], cv_dw_w[l],
                                      cv_dw_b[l], cv_ln_g[l], cv_ln_b[l], s['L'], s['tm'])
            x_mid[name], hf[name], logits[name] = _out_call(
                y_hy_t, y_gm, y_cv, y_at, s['x'], mix_norm_g[l][None, :].astype(F32), w_out[l].astype(BF16),
                g1, a2, sh2, s['mod_map'], wr_pad, s['L'], s['tm'])

        if last:
            y = _moe_ffn(hf['lat'], logits['lat'], router_bias, wg, wu, wd, tm_lat)
            g2_rows = jnp.repeat(g2[:B, 0, :], n_lat, axis=0)
            streams['lat']['x'] = x_mid['lat'] + g2_rows * y
        else:
            hf_all = jnp.concatenate([hf['lat'], hf['ctx']], axis=0)
            lg_all = jnp.concatenate([logits['lat'], logits['ctx']], axis=0)
            y = _moe_ffn(hf_all, lg_all, router_bias, wg, wu, wd, tm_ctx)
            g2_rows = jnp.repeat(g2[:B, 0, :], n_lat, axis=0)
            streams['lat']['x'] = x_mid['lat'] + g2_rows * y[:T_lat]
            streams['ctx']['x'] = x_mid['ctx'] + g2[B] * y[T_lat:]
    return _final_norm_call(streams['lat']['x'], final_norm_g, tm_lat).reshape(B, n_lat, D)
```

```python
import functools
import math

import jax
import jax.numpy as jnp
import numpy as np
from jax import lax
from jax.experimental import pallas as pl
from jax.experimental.pallas import tpu as pltpu

F32 = jnp.float32
BF16 = jnp.bfloat16

D_MODEL = 1024
DEPTH = 2
GRID_W = 64
EPS = 1e-6

D_GROUP = 256
N_MIXERS = 4
HY_ORDER = 2
HY_SHORT = 3
HY_EMB = 33
HY_BANDS = (HY_EMB - 1) // 2
HY_TARGET = 1e-2
HY_FAST_PCT = 0.3
HY_SLOW_PCT = 1.5
GM_CHUNK = 128
GM_HEADS = 4
CV_WIDTH = 31
CV_GROUPS = 4
MLA_HEADS = 4
MLA_NOPE = 64
MLA_ROPE = 32
MLA_V = 64
MLA_Q_RANK = 192
MLA_KV_RANK = 128
ROPE_BASE = 10000.0
N_EXPERTS = 16
N_EXPERT_GROUPS = 4
EXPERTS_PER_GROUP = N_EXPERTS // N_EXPERT_GROUPS
TOP_K = 2
D_EXPERT = 512

HY_COLS = (HY_ORDER + 1) * D_GROUP
GM_COLS = 2 * D_GROUP
CV_COLS = 2 * D_GROUP
MQ_COLS = MLA_Q_RANK
MKV_COLS = MLA_KV_RANK + MLA_ROPE
HY_OFF = 0
GM_OFF = HY_OFF + HY_COLS
CV_OFF = GM_OFF + GM_COLS
MQ_OFF = CV_OFF + CV_COLS
MKV_OFF = MQ_OFF + MQ_COLS
IN_COLS = MKV_OFF + MKV_COLS

LANES = 128
SUBLANES = 8
HEAD_SLOT = LANES
QK_COLS = MLA_HEADS * HEAD_SLOT
V_COLS = MLA_HEADS * MLA_V
PQ_OFF = MQ_OFF
PQ_PAD = 256
PKV_OFF = PQ_OFF + PQ_PAD
PKR_OFF = PKV_OFF + MLA_KV_RANK
PROJ_COLS = PKR_OFF + LANES

PAIR_LO = (0, 0, 0, 1, 1, 2)
PAIR_HI = (1, 2, 3, 2, 3, 3)
N_PAIRS = len(PAIR_LO)
N_CLASSES = N_EXPERT_GROUPS * N_PAIRS
CLASS_ROWS = -(-N_CLASSES // SUBLANES) * SUBLANES

ATT_TILE = 512
MOE_TILE = 256
DFT_ROWS = 256
DFT_COLS = 1024
CONV_HALO = 16
VMEM_LIMIT = 56 * 1024 * 1024


def _cparams(n_axes):
    return pltpu.CompilerParams(dimension_semantics=("arbitrary",) * n_axes, vmem_limit_bytes=VMEM_LIMIT)


def _axial_rope_tables(n_lat):
    rows = n_lat // GRID_W
    row = jnp.repeat(jnp.arange(rows), GRID_W).astype(F32)
    col = jnp.tile(jnp.arange(GRID_W), rows).astype(F32)
    n_freq = MLA_ROPE // 4
    inv = ROPE_BASE ** (-jnp.arange(n_freq, dtype=F32) / n_freq)
    ang = jnp.concatenate([row[:, None] * inv, col[:, None] * inv], axis=-1)
    return jnp.cos(ang), jnp.sin(ang)


def _hyena_filter_spectra(L, w1, b1, freq, w2, b2, w3):
    w1, b1, freq, w2, b2, w3 = (a.astype(F32) for a in (w1, b1, freq, w2, b2, w3))
    t = jnp.linspace(0.0, 1.0, L, dtype=F32)[:, None]
    w = 2.0 * math.pi * jnp.arange(L, dtype=F32)[:, None] / L
    f = jnp.linspace(1e-4, HY_BANDS - 1, HY_BANDS, dtype=F32)[None, :]
    z = jnp.concatenate([t, jnp.cos(f * w), -jnp.sin(f * w)], axis=-1)
    h = jnp.sin(freq * (z @ w1 + b1))
    h = jnp.sin(freq * (h @ w2 + b2))
    h = (h @ w3).reshape(L, HY_ORDER, 2, D_GROUP)
    deltas = jnp.abs(jnp.linspace(math.log(HY_TARGET) / HY_SLOW_PCT,
                                  math.log(HY_TARGET) / HY_FAST_PCT, D_GROUP, dtype=F32))
    h = h * jnp.exp(-t * deltas)[:, None, None, :]
    kf, kb = h[:, :, 0], h[:, :, 1]
    k2 = jnp.concatenate([kf, jnp.zeros_like(kf[:1]), kb[1:][::-1]], axis=0)
    k2 = k2 / jnp.sum(jnp.abs(k2), axis=0, keepdims=True)
    return jnp.fft.rfft(k2, n=2 * L, axis=0)


def _pack_w_in(w_in):
    D = w_in.shape[0]
    z = lambda n: jnp.zeros((D, n), w_in.dtype)
    return jnp.concatenate([
        w_in[:, :MQ_OFF],
        w_in[:, MQ_OFF:MKV_OFF], z(PQ_PAD - MQ_COLS),
        w_in[:, MKV_OFF:MKV_OFF + MLA_KV_RANK],
        w_in[:, MKV_OFF + MLA_KV_RANK:], z(LANES - MLA_ROPE),
    ], axis=1).astype(BF16)


def _pack_w_uq(w_uq):
    w = w_uq.reshape(MLA_Q_RANK, MLA_HEADS, MLA_NOPE + MLA_ROPE)
    w = jnp.pad(w, ((0, PQ_PAD - MLA_Q_RANK), (0, 0), (0, HEAD_SLOT - MLA_NOPE - MLA_ROPE)))
    return w.reshape(PQ_PAD, QK_COLS).astype(BF16)


def _pack_w_ukv(w_ukv):
    w = w_ukv.reshape(MLA_KV_RANK, MLA_HEADS, MLA_NOPE + MLA_V)
    k_part = jnp.pad(w[:, :, :MLA_NOPE], ((0, 0), (0, 0), (0, HEAD_SLOT - MLA_NOPE))).reshape(MLA_KV_RANK, QK_COLS)
    v_part = w[:, :, MLA_NOPE:].reshape(MLA_KV_RANK, V_COLS)
    top = jnp.concatenate([k_part, v_part], axis=1)
    eye = jnp.eye(LANES, dtype=w_ukv.dtype)[:, :MLA_ROPE]
    place = jnp.pad(eye, ((0, 0), (MLA_NOPE, HEAD_SLOT - MLA_NOPE - MLA_ROPE)))
    bot = jnp.concatenate([jnp.tile(place, (1, MLA_HEADS)), jnp.zeros((LANES, V_COLS), w_ukv.dtype)], axis=1)
    return jnp.concatenate([top, bot], axis=0).astype(BF16)


def _rope_slot_tables(n_lat):
    cos, sin = _axial_rope_tables(n_lat)
    half = MLA_ROPE // 2
    tail_w = HEAD_SLOT - MLA_NOPE - MLA_ROPE
    ones = lambda n: jnp.ones((n_lat, n), F32)
    zeros = lambda n: jnp.zeros((n_lat, n), F32)
    cf = jnp.concatenate([ones(MLA_NOPE), cos, cos, ones(tail_w)], axis=1)
    s_up = jnp.concatenate([zeros(MLA_NOPE + half), sin, zeros(tail_w)], axis=1)
    s_dn = jnp.concatenate([zeros(MLA_NOPE), -sin, zeros(half + tail_w)], axis=1)
    return cf, s_up, s_dn


def _identity_slot_tables(rows):
    return (jnp.ones((rows, HEAD_SLOT), F32), jnp.zeros((rows, HEAD_SLOT), F32), jnp.zeros((rows, HEAD_SLOT), F32))


def _rotate_slots(t, cf, s_up, s_dn):
    half = MLA_ROPE // 2
    outs = []
    for h in range(MLA_HEADS):
        tb = t[:, h * HEAD_SLOT:(h + 1) * HEAD_SLOT]
        outs.append(tb * cf + pltpu.roll(tb, half, 1) * s_up + pltpu.roll(tb, HEAD_SLOT - half, 1) * s_dn)
    return jnp.concatenate(outs, axis=1)


def _pack_bf16_pairs(a, b):
    ua = lax.bitcast_convert_type(a.astype(BF16).astype(F32), jnp.uint32)
    ub = lax.bitcast_convert_type(b.astype(BF16).astype(F32), jnp.uint32)
    return ua | (ub >> 16)


def _unpack_bf16_pairs(w):
    a = lax.bitcast_convert_type(w & jnp.uint32(0xFFFF0000), F32)
    b = lax.bitcast_convert_type(w << 16, F32)
    return a, b


def _row_gather_copy(src_hbm, row, buf_ref, slot, j, sem_ref):
    return pltpu.make_async_copy(src_hbm.at[pl.ds(row, 1)], buf_ref.at[slot, pl.ds(j, 1)], sem_ref.at[slot])


def _start_row_gather(src_hbm, row_of, n_rows, buf_ref, slot, sem_ref):
    for j in range(n_rows):
        _row_gather_copy(src_hbm, row_of(j), buf_ref, slot, j, sem_ref).start()


def _wait_row_gather(src_hbm, n_rows, buf_ref, slot, sem_ref):
    pltpu.make_async_copy(src_hbm.at[pl.ds(0, n_rows)], buf_ref.at[slot], sem_ref.at[slot]).wait()


def _gathered_rows(idx_ref, ys_hbm, buf_ref, sem_ref, tm):
    i = pl.program_id(0)
    n = pl.num_programs(0)
    slot = i % 2

    @pl.when(i == 0)
    def _():
        _start_row_gather(ys_hbm, lambda j: idx_ref[j], tm, buf_ref, 0, sem_ref)

    nxt = jnp.minimum(i + 1, n - 1)
    _start_row_gather(ys_hbm, lambda j: idx_ref[nxt * tm + j], tm, buf_ref, 1 - slot, sem_ref)
    _wait_row_gather(ys_hbm, tm, buf_ref, slot, sem_ref)
    return buf_ref[slot]


def _drain_gathered_rows(ys_hbm, buf_ref, sem_ref, tm):
    i = pl.program_id(0)

    @pl.when(i == pl.num_programs(0) - 1)
    def _():
        _wait_row_gather(ys_hbm, tm, buf_ref, 1 - i % 2, sem_ref)


def _proj_gather_kernel(idx_ref, xm_ref, g2_ref, ys_hbm, a_ref, sh_ref, win_ref, qg_ref, wuq_ref, kvg_ref, wkv_ref,
                        cf_ref, su_ref, sd_ref, hy_ref, gm_ref, cv_ref, q_ref, k_ref, v_ref, xn_ref, buf_ref, sem_ref):
    ya, yb = _unpack_bf16_pairs(_gathered_rows(idx_ref, ys_hbm, buf_ref, sem_ref, xm_ref.shape[0]))
    xn_ref[...] = xm_ref[...] + g2_ref[...] * jnp.concatenate([ya, yb], axis=1)
    _proj_kernel(xn_ref, a_ref, sh_ref, win_ref, qg_ref, wuq_ref, kvg_ref, wkv_ref, cf_ref, su_ref, sd_ref,
                 hy_ref, gm_ref, cv_ref, q_ref, k_ref, v_ref)
    _drain_gathered_rows(ys_hbm, buf_ref, sem_ref, xm_ref.shape[0])


def _proj_kernel(x_ref, a_ref, sh_ref, win_ref, qg_ref, wuq_ref, kvg_ref, wkv_ref, cf_ref, su_ref, sd_ref,
                 hy_ref, gm_ref, cv_ref, q_ref, k_ref, v_ref):
    x = x_ref[...]
    ms = jnp.mean(x * x, axis=-1, keepdims=True)
    h = x * lax.rsqrt(ms + EPS) * a_ref[...] + sh_ref[...]
    p = jnp.dot(h.astype(BF16), win_ref[...], preferred_element_type=F32)
    hy_ref[...] = p[:, HY_OFF:GM_OFF]
    gm_ref[...] = p[:, GM_OFF:CV_OFF]
    cv_ref[...] = p[:, CV_OFF:MQ_OFF]

    cf, su, sd = cf_ref[...], su_ref[...], sd_ref[...]
    cq = p[:, PQ_OFF:PQ_OFF + PQ_PAD]
    qn = cq * lax.rsqrt(jnp.sum(cq * cq, axis=-1, keepdims=True) * (1.0 / MLA_Q_RANK) + EPS) * qg_ref[...]
    q = jnp.dot(qn.astype(BF16), wuq_ref[...], preferred_element_type=F32)
    q = _rotate_slots(q, cf, su, sd) * (1.0 / math.sqrt(MLA_NOPE + MLA_ROPE))
    q_ref[...] = q.astype(BF16)

    ckv = p[:, PKV_OFF:PKV_OFF + MLA_KV_RANK]
    kvn = ckv * lax.rsqrt(jnp.mean(ckv * ckv, axis=-1, keepdims=True) + EPS) * kvg_ref[...]
    kin = jnp.concatenate([kvn, p[:, PKR_OFF:PKR_OFF + LANES]], axis=1).astype(BF16)
    kv = jnp.dot(kin, wkv_ref[...], preferred_element_type=F32)
    k_ref[...] = _rotate_slots(kv[:, :QK_COLS], cf, su, sd).astype(BF16)
    v_ref[...] = kv[:, QK_COLS:].astype(BF16)


def _proj_call(xt, mod_a, mod_sh, mod_map, win, qg, wuq, kvg, wkv, rope_tabs, rope_map, tm, moe_in=None):
    T, D = xt.shape
    const = lambda i, *_: (0, 0)
    tok = lambda i, *_: (i, 0)
    mod = lambda i, *_: mod_map(i)
    rope = lambda i, *_: rope_map(i)
    out_cols = (HY_COLS, GM_COLS, CV_COLS, QK_COLS, QK_COLS, V_COLS)
    out_dtypes = (F32, F32, F32, BF16, BF16, BF16)
    in_specs = [
        pl.BlockSpec((None, 1, D), mod),
        pl.BlockSpec((None, 1, D), mod),
        pl.BlockSpec(win.shape, const),
        pl.BlockSpec(qg.shape, const),
        pl.BlockSpec(wuq.shape, const),
        pl.BlockSpec(kvg.shape, const),
        pl.BlockSpec(wkv.shape, const),
        pl.BlockSpec((tm, HEAD_SLOT), rope),
        pl.BlockSpec((tm, HEAD_SLOT), rope),
        pl.BlockSpec((tm, HEAD_SLOT), rope),
    ]
    out_specs = [pl.BlockSpec((tm, n), tok) for n in out_cols]
    out_shape = [jax.ShapeDtypeStruct((T, n), dt) for n, dt in zip(out_cols, out_dtypes)]
    shared = (mod_a, mod_sh, win, qg, wuq, kvg, wkv, *rope_tabs)
    if moe_in is None:
        return pl.pallas_call(
            _proj_kernel,
            grid=(T // tm,),
            in_specs=[pl.BlockSpec((tm, D), tok)] + in_specs,
            out_specs=out_specs,
            out_shape=out_shape,
            compiler_params=_cparams(1),
            name="proj_qkv",
        )(xt, *shared)
    tok_row, g2, ys = moe_in
    grid_spec = pltpu.PrefetchScalarGridSpec(
        num_scalar_prefetch=1,
        grid=(T // tm,),
        in_specs=[pl.BlockSpec((tm, D), tok), pl.BlockSpec((None, 1, D), mod),
                  pl.BlockSpec(memory_space=pl.ANY)] + in_specs,
        out_specs=out_specs + [pl.BlockSpec((tm, D), tok)],
        scratch_shapes=[pltpu.VMEM((2, tm, ys.shape[1]), ys.dtype), pltpu.SemaphoreType.DMA((2,))],
    )
    return pl.pallas_call(
        _proj_gather_kernel,
        grid_spec=grid_spec,
        out_shape=out_shape + [jax.ShapeDtypeStruct((T, D), F32)],
        compiler_params=_cparams(1),
        name="moe_residual_proj_qkv",
    )(tok_row, xt, g2, ys, *shared)


def _attend_heads(q_ref, key_refs, val_refs, o_ref):
    nt = (((1,), (1,)), ((), ()))
    lane = lax.broadcasted_iota(jnp.int32, (1, V_COLS), 1)
    vals = [v_ref[...] for v_ref in val_refs]
    acc = jnp.zeros(o_ref.shape, F32)
    for h in range(MLA_HEADS):
        sl_h = slice(h * HEAD_SLOT, (h + 1) * HEAD_SLOT)
        q = q_ref[:, sl_h]
        scores = [lax.dot_general(q, k_ref[:, sl_h], nt, preferred_element_type=F32) for k_ref in key_refs]
        m = functools.reduce(jnp.maximum, [jnp.max(s, axis=-1, keepdims=True) for s in scores])
        probs = [jnp.exp(s - m) for s in scores]
        denom = functools.reduce(jnp.add, [jnp.sum(p, axis=-1, keepdims=True) for p in probs])
        o = functools.reduce(jnp.add, [jnp.dot(p.astype(BF16), v, preferred_element_type=F32)
                                       for p, v in zip(probs, vals)])
        in_head = (lane >= h * MLA_V) & (lane < (h + 1) * MLA_V)
        acc = acc + jnp.where(in_head, o / denom, 0.0)
    o_ref[...] = acc


def _attn_lat_kernel(q_ref, kl_ref, kc_ref, vl_ref, vc_ref, o_ref):
    _attend_heads(q_ref, (kl_ref, kc_ref), (vl_ref, vc_ref), o_ref)


def _attn_ctx_kernel(q_ref, k_ref, v_ref, o_ref):
    _attend_heads(q_ref, (k_ref,), (v_ref,), o_ref)


def _attn_lat_call(q, k, v, k_ctx, v_ctx, n_batch, n_lat, n_ctx):
    tq = ATT_TILE
    qt = n_lat // tq
    return pl.pallas_call(
        _attn_lat_kernel,
        grid=(n_batch, qt),
        in_specs=[
            pl.BlockSpec((tq, QK_COLS), lambda b, j: (b * qt + j, 0)),
            pl.BlockSpec((n_lat, QK_COLS), lambda b, j: (b, 0)),
            pl.BlockSpec((n_ctx, QK_COLS), lambda b, j: (b, 0)),
            pl.BlockSpec((n_lat, V_COLS), lambda b, j: (b, 0)),
            pl.BlockSpec((n_ctx, V_COLS), lambda b, j: (b, 0)),
        ],
        out_specs=pl.BlockSpec((tq, V_COLS), lambda b, j: (b * qt + j, 0)),
        out_shape=jax.ShapeDtypeStruct((n_batch * n_lat, V_COLS), F32),
        compiler_params=_cparams(2),
        name="attn_latent",
    )(q, k, k_ctx, v, v_ctx)


def _attn_ctx_call(q, k, v, n_batch, n_ctx):
    blk = lambda b: (b, 0)
    return pl.pallas_call(
        _attn_ctx_kernel,
        grid=(n_batch,),
        in_specs=[pl.BlockSpec((n_ctx, QK_COLS), blk), pl.BlockSpec((n_ctx, QK_COLS), blk),
                  pl.BlockSpec((n_ctx, V_COLS), blk)],
        out_specs=pl.BlockSpec((n_ctx, V_COLS), blk),
        out_shape=jax.ShapeDtypeStruct((n_batch * n_ctx, V_COLS), F32),
        compiler_params=_cparams(1),
        name="attn_context",
    )(q, k, v)


@functools.lru_cache(maxsize=None)
def _dft_matrices_np(L):
    idx = np.arange(L, dtype=np.int64)
    ang = (np.outer(idx, idx) % (2 * L)).astype(np.float64) * (np.pi / L)
    c = np.cos(ang)
    s = np.sin(ang)
    s[0, :] = np.where(idx % 2 == 0, 1.0, -1.0)
    to_bf16 = lambda a: a.astype(np.float32).astype(BF16)
    return to_bf16(c), to_bf16(s), to_bf16(s.T)


def _spectrum_tables(spec):
    L = spec.shape[0] - 1
    re, im = jnp.real(spec).astype(F32), jnp.imag(spec).astype(F32)
    first = (jnp.arange(L) == 0)[:, None]
    kp_r = jnp.where(first, 0.5, 1.0) * re[:L] / L
    kp_i = jnp.where(first, 0.0, im[:L] / L)
    kq_r = jnp.where(first, re[L:L + 1] / (2 * L), re[:L] / L)
    kq_i = jnp.where(first, 0.0, im[:L] / L)
    return kp_r, kp_i, kq_r, kq_i


def _hy_prep_kernel(p_ref, w_ref, b_ref, z_ref, v16_ref):
    j = pl.program_id(1)
    p = p_ref[...]
    L = p.shape[0]
    row = lax.broadcasted_iota(jnp.int32, (L, 1), 0)
    prev = jnp.where(row == 0, 0.0, pltpu.roll(p, 1, 0))
    nxt = jnp.where(row == L - 1, 0.0, pltpu.roll(p, L - 1, 0))
    z = prev * w_ref[0:1, :] + p * w_ref[1:2, :] + nxt * w_ref[2:3, :] + b_ref[...]
    z_ref[...] = z

    @pl.when(j == HY_ORDER)
    def _():
        v16_ref[...] = z.astype(BF16)


def _hy_prep_call(p_hy, short_w, short_b, n_batch, L):
    N = n_batch * D_GROUP
    return pl.pallas_call(
        _hy_prep_kernel,
        grid=(n_batch, HY_ORDER + 1),
        in_specs=[
            pl.BlockSpec((L, D_GROUP), lambda b, j: (b, j)),
            pl.BlockSpec((HY_SHORT, D_GROUP), lambda b, j: (0, j)),
            pl.BlockSpec((1, D_GROUP), lambda b, j: (0, j)),
        ],
        out_specs=[pl.BlockSpec((None, L, D_GROUP), lambda b, j: (j, 0, b)),
                   pl.BlockSpec((L, D_GROUP), lambda b, j: (0, b))],
        out_shape=[jax.ShapeDtypeStruct((HY_ORDER + 1, L, N), F32), jax.ShapeDtypeStruct((L, N), BF16)],
        compiler_params=_cparams(2),
        name="hyena_short_conv",
    )(p_hy, short_w, short_b[None, :])


def _dft_fwd_kernel(c_ref, s_ref, u_ref, kpr_ref, kpi_ref, kqr_ref, kqi_ref, p_ref, q_ref):
    u = u_ref[...]
    a = jnp.dot(c_ref[...], u, preferred_element_type=F32)
    b = jnp.dot(s_ref[...], u, preferred_element_type=F32)
    kpr, kpi, kqr, kqi = kpr_ref[...], kpi_ref[...], kqr_ref[...], kqi_ref[...]
    for g in range(u.shape[1] // D_GROUP):
        sl = slice(g * D_GROUP, (g + 1) * D_GROUP)
        p_ref[:, sl] = (kpr * a[:, sl] + kpi * b[:, sl]).astype(BF16)
        q_ref[:, sl] = (kqr * b[:, sl] - kqi * a[:, sl]).astype(BF16)


def _dft_fwd_call(cmat, smat, u16, tabs):
    L, N = u16.shape
    tr = min(DFT_ROWS, L)
    tn = min(DFT_COLS, N)
    row = lambda h, i: (i, 0)
    return pl.pallas_call(
        _dft_fwd_kernel,
        grid=(N // tn, L // tr),
        in_specs=[
            pl.BlockSpec((tr, L), row),
            pl.BlockSpec((tr, L), row),
            pl.BlockSpec((L, tn), lambda h, i: (0, h), pipeline_mode=pl.Buffered(1)),
        ] + [pl.BlockSpec((tr, D_GROUP), row)] * 4,
        out_specs=[pl.BlockSpec((tr, tn), lambda h, i: (i, h))] * 2,
        out_shape=[jax.ShapeDtypeStruct((L, N), BF16)] * 2,
        compiler_params=_cparams(2),
        name="hyena_dft_fwd",
    )(cmat, smat, u16, *tabs)


def _dft_inv_kernel(c_ref, st_ref, p_ref, q_ref, u_ref, g_ref, bias_ref, y_ref, y16_ref):
    conv = (jnp.dot(c_ref[...], p_ref[...], preferred_element_type=F32)
            + jnp.dot(st_ref[...], q_ref[...], preferred_element_type=F32))
    y = g_ref[...] * (conv + u_ref[...] * bias_ref[...])
    y_ref[...] = y
    y16_ref[...] = y.astype(BF16)


def _dft_inv_call(cmat, stmat, p16, q16, u_arr, u_sel, g_arr, g_sel, bias_row):
    L, N = p16.shape
    tr = min(DFT_ROWS, L)
    tn = min(DFT_COLS, N)
    row = lambda h, i: (i, 0)
    res = lambda h, i: (0, h)
    return pl.pallas_call(
        _dft_inv_kernel,
        grid=(N // tn, L // tr),
        in_specs=[
            pl.BlockSpec((tr, L), row),
            pl.BlockSpec((tr, L), row),
            pl.BlockSpec((L, tn), res, pipeline_mode=pl.Buffered(1)),
            pl.BlockSpec((L, tn), res, pipeline_mode=pl.Buffered(1)),
            pl.BlockSpec((None, tr, tn), lambda h, i: (u_sel, i, h)),
            pl.BlockSpec((None, tr, tn), lambda h, i: (g_sel, i, h)),
            pl.BlockSpec((1, tn), res),
        ],
        out_specs=[pl.BlockSpec((tr, tn), lambda h, i: (i, h))] * 2,
        out_shape=[jax.ShapeDtypeStruct((L, N), F32), jax.ShapeDtypeStruct((L, N), BF16)],
        compiler_params=_cparams(2),
        name="hyena_dft_inv",
    )(cmat, stmat, p16, q16, u_arr, g_arr, bias_row)


def _hyena_call(p_hy, short_w, short_b, spectra, hy_bias, n_batch, L):
    cmat, smat, stmat = (jnp.asarray(m) for m in _dft_matrices_np(L))
    z, u16 = _hy_prep_call(p_hy, short_w, short_b, n_batch, L)
    y_stack = z
    u_sel = HY_ORDER
    for n in range(HY_ORDER):
        tabs = _spectrum_tables(spectra[:, n])
        p16, q16 = _dft_fwd_call(cmat, smat, u16, tabs)
        bias_row = jnp.tile(hy_bias[n][None, :].astype(F32), (1, n_batch))
        y, u16 = _dft_inv_call(cmat, stmat, p16, q16, y_stack, u_sel, z, n, bias_row)
        y_stack, u_sel = y[None], 0
    return y


def _mixers_kernel(gm_ref, cv_ref, cvp_ref, cvn_ref, lng_ref, lnb_ref, ws_ref, bsf_ref, dww_ref, dwb_ref,
                   cg_ref, cb_ref, avg_ref, ygm_ref, ycv_ref, glu_ref, *, tiles_per_seq):
    i = pl.program_id(0)
    tm = gm_ref.shape[0]
    lane = lax.broadcasted_iota(jnp.int32, (1, D_GROUP), 1)

    z = jax.nn.gelu(gm_ref[...], approximate=True)
    u, v = z[:, :D_GROUP], z[:, D_GROUP:]
    mu = jnp.mean(v, axis=-1, keepdims=True)
    vc = v - mu
    var = jnp.mean(vc * vc, axis=-1, keepdims=True)
    vn = (vc * lax.rsqrt(var + EPS) * lng_ref[...] + lnb_ref[...]).astype(BF16)
    hd = D_GROUP // GM_HEADS
    for c in range(tm // GM_CHUNK):
        rows = slice(c * GM_CHUNK, (c + 1) * GM_CHUNK)
        s = bsf_ref[...]
        for g in range(GM_HEADS):
            sg = jnp.dot(ws_ref[g], vn[rows, :], preferred_element_type=F32)
            s = s + jnp.where((lane >= g * hd) & (lane < (g + 1) * hd), sg, 0.0)
        ygm_ref[rows, :] = u[rows, :] * s

    def glu(t):
        return t[:, :D_GROUP] * jax.nn.sigmoid(t[:, D_GROUP:])

    first = (i % tiles_per_seq) == 0
    last = (i % tiles_per_seq) == tiles_per_seq - 1
    glu_ref[0:CONV_HALO, :] = jnp.where(first, 0.0, glu(cvp_ref[...]))
    glu_ref[CONV_HALO:CONV_HALO + tm, :] = glu(cv_ref[...])
    glu_ref[CONV_HALO + tm:, :] = jnp.where(last, 0.0, glu(cvn_ref[...]))
    pad = (CV_WIDTH - 1) // 2
    rc = 128
    for c in range(tm // rc):
        acc = jnp.zeros((rc, D_GROUP), F32) + dwb_ref[...]
        for k in range(CV_WIDTH):
            start = c * rc + CONV_HALO - pad + k
            acc = acc + glu_ref[start:start + rc, :] * dww_ref[k:k + 1, :]
        gmean = jnp.dot(acc, avg_ref[...], preferred_element_type=F32, precision=lax.Precision.HIGHEST)
        d = acc - gmean
        gvar = jnp.dot(d * d, avg_ref[...], preferred_element_type=F32, precision=lax.Precision.HIGHEST)
        n = d * lax.rsqrt(gvar + EPS) * cg_ref[...] + cb_ref[...]
        ycv_ref[c * rc:(c + 1) * rc, :] = n * jax.nn.sigmoid(n)


def _mixers_call(p_gm, p_cv, gm_ln_g, gm_ln_b, gm_ws, gm_bs, cv_dw_w, cv_dw_b, cv_ln_g, cv_ln_b, L, tm):
    T = p_gm.shape[0]
    tps = L // tm
    hb = tm // CONV_HALO
    n_hblk = T // CONV_HALO
    const2 = lambda i: (0, 0)
    tok = lambda i: (i, 0)
    bs_full = jnp.repeat(gm_bs.T.astype(F32), D_GROUP // GM_HEADS, axis=1)
    gid = np.arange(D_GROUP) // (D_GROUP // CV_GROUPS)
    avg = jnp.asarray((gid[:, None] == gid[None, :]).astype(np.float32) / (D_GROUP // CV_GROUPS))
    row = lambda a: a[None, :].astype(F32)
    return pl.pallas_call(
        functools.partial(_mixers_kernel, tiles_per_seq=tps),
        grid=(T // tm,),
        in_specs=[
            pl.BlockSpec((tm, GM_COLS), tok),
            pl.BlockSpec((tm, CV_COLS), tok),
            pl.BlockSpec((CONV_HALO, CV_COLS), lambda i: (jnp.maximum(i * hb - 1, 0), 0)),
            pl.BlockSpec((CONV_HALO, CV_COLS), lambda i: (jnp.minimum((i + 1) * hb, n_hblk - 1), 0)),
            pl.BlockSpec((1, D_GROUP), const2),
            pl.BlockSpec((1, D_GROUP), const2),
            pl.BlockSpec((GM_HEADS, GM_CHUNK, GM_CHUNK), lambda i: (0, 0, 0)),
            pl.BlockSpec((GM_CHUNK, D_GROUP), const2),
            pl.BlockSpec((CV_WIDTH, D_GROUP), const2),
            pl.BlockSpec((1, D_GROUP), const2),
            pl.BlockSpec((1, D_GROUP), const2),
            pl.BlockSpec((1, D_GROUP), const2),
            pl.BlockSpec((D_GROUP, D_GROUP), const2),
        ],
        out_specs=[pl.BlockSpec((tm, D_GROUP), tok)] * 2,
        out_shape=[jax.ShapeDtypeStruct((T, D_GROUP), F32)] * 2,
        scratch_shapes=[pltpu.VMEM((tm + 2 * CONV_HALO, D_GROUP), F32)],
        compiler_params=_cparams(1),
        name="gmlp_conv_mixers",
    )(p_gm, p_cv, p_cv, p_cv, row(gm_ln_g), row(gm_ln_b), gm_ws.astype(BF16), bs_full, cv_dw_w.astype(F32),
      row(cv_dw_b), row(cv_ln_g), row(cv_ln_b), avg)


def _out_kernel(hy_ref, gm_ref, cv_ref, at_ref, x_ref, mg_ref, wo_ref, g1_ref, a2_ref, sh2_ref, wrh_ref, wrl_ref,
                rb_ref, tri_ref, cin_ref, xm_ref, hfx_ref, cls_ref, rank_ref, cnt_ref, carry_ref):
    i = pl.program_id(0)
    tm = x_ref.shape[0]
    o = None
    for g, y_ref in enumerate((hy_ref, gm_ref, cv_ref, at_ref)):
        y = y_ref[...]
        n = y * lax.rsqrt(jnp.mean(y * y, axis=-1, keepdims=True) + EPS) * mg_ref[:, g * D_GROUP:(g + 1) * D_GROUP]
        part = jnp.dot(n.astype(BF16), wo_ref[g * D_GROUP:(g + 1) * D_GROUP, :], preferred_element_type=F32)
        o = part if o is None else o + part
    xm = x_ref[...] + g1_ref[...] * o
    xm_ref[...] = xm
    hf = xm * lax.rsqrt(jnp.mean(xm * xm, axis=-1, keepdims=True) + EPS) * a2_ref[...] + sh2_ref[...]

    hf_hi = hf.astype(BF16)
    hf_lo = (hf - hf_hi.astype(F32)).astype(BF16)
    logits = (jnp.dot(hf_hi, wrh_ref[...], preferred_element_type=F32)
              + jnp.dot(hf_lo, wrh_ref[...], preferred_element_type=F32)
              + jnp.dot(hf_hi, wrl_ref[...], preferred_element_type=F32))
    cls, g_lo, g_hi = _route_top2(jnp.transpose(logits), rb_ref[...])
    cls_ref[...] = cls

    half = hf.shape[1] // 2
    hfx_ref[:, :half] = _pack_bf16_pairs(hf[:, :half], hf[:, half:])
    gate_rows = jnp.concatenate([g_lo, g_hi, jnp.zeros((LANES - TOP_K, tm), F32)], axis=0)
    hfx_ref[:, half:] = lax.bitcast_convert_type(jnp.transpose(gate_rows), jnp.uint32)

    @pl.when(i == 0)
    def _():
        carry_ref[...] = cin_ref[...]

    sub = lax.broadcasted_iota(jnp.int32, (CLASS_ROWS, tm), 0)
    onehot = sub == cls
    prefix = jnp.dot(onehot.astype(BF16), tri_ref[...], preferred_element_type=F32)
    carry = carry_ref[...]
    rank = jnp.sum(jnp.where(onehot, prefix + carry[:, 0:1], 0.0), axis=0, keepdims=True)
    rank_ref[...] = rank.astype(jnp.int32)
    carry = carry + jnp.sum(onehot.astype(F32), axis=1, keepdims=True)
    carry_ref[...] = carry
    cnt_ref[...] = carry


def _out_alias_kernel(*refs):
    _out_kernel(*refs[1:])


def _out_call(y_hy_t, y_gm, y_cv, y_at, xt, mixg, wo, g1, a2, sh2, mod_map, wr_hi, wr_lo, rbias, counts_in, L, tm,
              hfx_prev, t_total, row0):
    T, D = xt.shape
    tps = L // tm
    blk0 = row0 // tm
    const = lambda i: (0, 0)
    tok = lambda i: (i, 0)
    lane_tok = lambda i: (0, i)
    tri = jnp.asarray(np.triu(np.ones((tm, tm), np.float32), k=1)).astype(BF16)
    W = D // 2 + LANES
    in_specs = [pl.BlockSpec((tm, D_GROUP), lambda i: (i % tps, i // tps))] + [pl.BlockSpec((tm, D_GROUP), tok)] * 3 + [
        pl.BlockSpec((tm, D), tok),
        pl.BlockSpec(mixg.shape, const),
        pl.BlockSpec(wo.shape, const),
        pl.BlockSpec((None, 1, D), mod_map),
        pl.BlockSpec((None, 1, D), mod_map),
        pl.BlockSpec((None, 1, D), mod_map),
        pl.BlockSpec(wr_hi.shape, const),
        pl.BlockSpec(wr_lo.shape, const),
        pl.BlockSpec(rbias.shape, const),
        pl.BlockSpec((tm, tm), const),
        pl.BlockSpec((CLASS_ROWS, LANES), const),
    ]
    args = (y_hy_t, y_gm, y_cv, y_at, xt, mixg, wo, g1, a2, sh2, wr_hi, wr_lo, rbias, tri, counts_in)
    aliased = hfx_prev is not None
    return pl.pallas_call(
        _out_alias_kernel if aliased else _out_kernel,
        grid=(T // tm,),
        in_specs=([pl.BlockSpec(memory_space=pl.ANY)] if aliased else []) + in_specs,
        out_specs=[pl.BlockSpec((tm, D), tok), pl.BlockSpec((tm, W), lambda i: (blk0 + i, 0)),
                   pl.BlockSpec((1, tm), lane_tok), pl.BlockSpec((1, tm), lane_tok),
                   pl.BlockSpec((CLASS_ROWS, LANES), const)],
        out_shape=[jax.ShapeDtypeStruct((T, D), F32), jax.ShapeDtypeStruct((t_total, W), jnp.uint32),
                   jax.ShapeDtypeStruct((1, T), jnp.int32), jax.ShapeDtypeStruct((1, T), jnp.int32),
                   jax.ShapeDtypeStruct((CLASS_ROWS, LANES), F32)],
        scratch_shapes=[pltpu.VMEM((CLASS_ROWS, LANES), F32)],
        input_output_aliases={0: 1} if aliased else {},
        compiler_params=_cparams(1),
        name="mix_out_norm2_route",
    )(*(((hfx_prev,) if aliased else ()) + args))


def _first_max_flags(vals):
    m = functools.reduce(jnp.maximum, vals)
    flags, taken = [], None
    for v in vals:
        f = v >= m
        if taken is not None:
            f = f & jnp.logical_not(taken)
        flags.append(f)
        taken = f if taken is None else taken | f
    return flags, m


def _pick(flags, vals):
    out = vals[-1]
    for f, v in zip(flags[-2::-1], vals[-2::-1]):
        out = jnp.where(f, v, out)
    return out


def _route_top2(lt, bias_col):
    s_all = jax.nn.sigmoid(lt[:N_EXPERTS, :])
    sel_all = s_all + bias_col
    s = [s_all[e:e + 1, :] for e in range(N_EXPERTS)]
    sel = [sel_all[e:e + 1, :] for e in range(N_EXPERTS)]
    neg = -jnp.inf
    E = EXPERTS_PER_GROUP

    def top2(vals):
        f1, m1 = _first_max_flags(vals)
        rest = [jnp.where(f, neg, v) for f, v in zip(f1, vals)]
        f2, m2 = _first_max_flags(rest)
        return f1, m1, f2, m2

    scores = []
    for g in range(N_EXPERT_GROUPS):
        _, m1, _, m2 = top2(sel[g * E:(g + 1) * E])
        scores.append(m1 + m2)
    gflags, _ = _first_max_flags(scores)
    bsel = [_pick(gflags, [sel[g * E + j] for g in range(N_EXPERT_GROUPS)]) for j in range(E)]
    bs = [_pick(gflags, [s[g * E + j] for g in range(N_EXPERT_GROUPS)]) for j in range(E)]
    f1, _, f2, _ = top2(bsel)
    zero = jnp.zeros_like(bs[0])
    w1 = functools.reduce(jnp.add, [jnp.where(f, v, zero) for f, v in zip(f1, bs)])
    w2 = functools.reduce(jnp.add, [jnp.where(f, v, zero) for f, v in zip(f2, bs)])
    izero = jnp.zeros(w1.shape, jnp.int32)
    j1 = functools.reduce(jnp.add, [jnp.where(f, j, izero) for j, f in enumerate(f1)])
    j2 = functools.reduce(jnp.add, [jnp.where(f, j, izero) for j, f in enumerate(f2)])
    gi = functools.reduce(jnp.add, [jnp.where(f, g, izero) for g, f in enumerate(gflags)])
    lo, hi = jnp.minimum(j1, j2), jnp.maximum(j1, j2)
    pair = jnp.where(lo == 0, 0, jnp.where(lo == 1, 3, 5)) + hi - lo - 1
    tot = w1 + w2
    first_is_lo = j1 < j2
    return gi * N_PAIRS + pair, jnp.where(first_is_lo, w1, w2) / tot, jnp.where(first_is_lo, w2, w1) / tot


def _table_lookup(table, idx):
    n = table.shape[0]
    hit = idx[..., None] == jnp.arange(n, dtype=jnp.int32)
    return jnp.sum(jnp.where(hit, table, 0), axis=-1)


def _class_plan(cls, rank, counts, tile):
    T = cls.shape[0]
    n_tiles = (T + N_CLASSES * tile) // tile
    order = jnp.argsort(cls, stable=True).astype(jnp.int32)
    padded = ((counts + tile - 1) // tile) * tile
    seg_end = jnp.cumsum(padded)
    seg_start = seg_end - padded
    src_start = jnp.cumsum(counts) - counts
    tok_row = _table_lookup(seg_start, cls) + rank

    tile_first = jnp.arange(n_tiles, dtype=jnp.int32) * tile
    tile_class = jnp.minimum(jnp.sum((seg_end[None, :] <= tile_first[:, None]).astype(jnp.int32), axis=1),
                             N_CLASSES - 1)
    off = tile_first - _table_lookup(seg_start, tile_class)
    tile_cnt = jnp.clip(_table_lookup(counts, tile_class) - off, 0, tile)
    tile_src = jnp.clip(_table_lookup(src_start, tile_class) + off, 0, T - 1)
    grp = tile_class // N_PAIRS
    pair = tile_class % N_PAIRS
    tile_lo = grp * EXPERTS_PER_GROUP + _table_lookup(jnp.asarray(PAIR_LO, jnp.int32), pair)
    tile_hi = grp * EXPERTS_PER_GROUP + _table_lookup(jnp.asarray(PAIR_HI, jnp.int32), pair)
    return order, tile_lo, tile_hi, tile_src, tile_cnt, tok_row


def _moe_pair_kernel(order_ref, lo_ref, hi_ref, src_ref, cnt_ref, hfx_hbm, wga_ref, wua_ref, wda_ref, wgb_ref,
                     wub_ref, wdb_ref, o_ref, buf_ref, sem_ref):
    i = pl.program_id(0)
    n = pl.num_programs(0)
    tile = o_ref.shape[0]
    n_tok = order_ref.shape[0]
    slot = i % 2

    def start(t, s):
        base = src_ref[t]
        _start_row_gather(hfx_hbm, lambda j: order_ref[jnp.minimum(base + j, n_tok - 1)], tile, buf_ref, s, sem_ref)

    @pl.when((i == 0) & (cnt_ref[0] > 0))
    def _():
        start(0, 0)

    @pl.when((cnt_ref[i] <= 0) & (i > 0) & (cnt_ref[jnp.maximum(i - 1, 0)] > 0))
    def _():
        _wait_row_gather(hfx_hbm, tile, buf_ref, slot, sem_ref)

    @pl.when(cnt_ref[i] > 0)
    def _():
        start(jnp.minimum(i + 1, n - 1), 1 - slot)
        _wait_row_gather(hfx_hbm, tile, buf_ref, slot, sem_ref)
        rows = buf_ref[slot]
        half = rows.shape[1] - LANES
        xa, xb = _unpack_bf16_pairs(rows[:, :half])
        x = jnp.concatenate([xa, xb], axis=1).astype(BF16)
        live = lax.broadcasted_iota(jnp.int32, (tile, 1), 0) < cnt_ref[i]
        gates = lax.bitcast_convert_type(rows[:, half:], F32)
        g_lo = jnp.where(live, gates[:, 0:1], 0.0)
        g_hi = jnp.where(live, gates[:, 1:2], 0.0)

        def hidden(wg_ref, wu_ref, gate):
            hg = jnp.dot(x, wg_ref[...], preferred_element_type=F32)
            hu = jnp.dot(x, wu_ref[...], preferred_element_type=F32)
            return (hg * jax.nn.sigmoid(hg) * hu * gate).astype(BF16)

        y = (jnp.dot(hidden(wga_ref, wua_ref, g_lo), wda_ref[...], preferred_element_type=F32)
             + jnp.dot(hidden(wgb_ref, wub_ref, g_hi), wdb_ref[...], preferred_element_type=F32))
        o_ref[...] = _pack_bf16_pairs(y[:, :half], y[:, half:])

    @pl.when(cnt_ref[i] <= 0)
    def _():
        o_ref[...] = jnp.zeros_like(o_ref)


def _moe_pair_call(hfx, order, tile_lo, tile_hi, tile_src, tile_cnt, wg, wu, wd, tile):
    T, W = hfx.shape
    D = 2 * (W - LANES)
    F = wg.shape[-1]
    n_tiles = tile_lo.shape[0]
    lo = lambda i, o, tl, th, ts, tc: (tl[i], 0, 0)
    hi = lambda i, o, tl, th, ts, tc: (th[i], 0, 0)
    grid_spec = pltpu.PrefetchScalarGridSpec(
        num_scalar_prefetch=5,
        grid=(n_tiles,),
        in_specs=[
            pl.BlockSpec(memory_space=pl.ANY),
            pl.BlockSpec((None, D, F), lo), pl.BlockSpec((None, D, F), lo), pl.BlockSpec((None, F, D), lo),
            pl.BlockSpec((None, D, F), hi), pl.BlockSpec((None, D, F), hi), pl.BlockSpec((None, F, D), hi),
        ],
        out_specs=pl.BlockSpec((tile, D // 2), lambda i, o, tl, th, ts, tc: (i, 0)),
        scratch_shapes=[pltpu.VMEM((2, tile, W), jnp.uint32), pltpu.SemaphoreType.DMA((2,))],
    )
    return pl.pallas_call(
        _moe_pair_kernel,
        grid_spec=grid_spec,
        out_shape=jax.ShapeDtypeStruct((n_tiles * tile, D // 2), jnp.uint32),
        compiler_params=_cparams(1),
        name="moe_pair_grouped",
    )(order, tile_lo, tile_hi, tile_src, tile_cnt, hfx, wg, wu, wd, wg, wu, wd)


def _final_kernel(idx_ref, xm_ref, g2_ref, ys_hbm, g_ref, o_ref, buf_ref, sem_ref):
    ya, yb = _unpack_bf16_pairs(_gathered_rows(idx_ref, ys_hbm, buf_ref, sem_ref, xm_ref.shape[0]))
    x = xm_ref[...] + g2_ref[...] * jnp.concatenate([ya, yb], axis=1)
    o_ref[...] = x * lax.rsqrt(jnp.mean(x * x, axis=-1, keepdims=True) + EPS) * g_ref[...]
    _drain_gathered_rows(ys_hbm, buf_ref, sem_ref, xm_ref.shape[0])


def _final_call(x_mid, tok_row, g2, mod_map, ys, g, tm):
    T, D = x_mid.shape
    tok = lambda i, *_: (i, 0)
    grid_spec = pltpu.PrefetchScalarGridSpec(
        num_scalar_prefetch=1,
        grid=(T // tm,),
        in_specs=[pl.BlockSpec((tm, D), tok), pl.BlockSpec((None, 1, D), lambda i, *_: mod_map(i)),
                  pl.BlockSpec(memory_space=pl.ANY), pl.BlockSpec((1, D), lambda i, *_: (0, 0))],
        out_specs=pl.BlockSpec((tm, D), tok),
        scratch_shapes=[pltpu.VMEM((2, tm, ys.shape[1]), ys.dtype), pltpu.SemaphoreType.DMA((2,))],
    )
    return pl.pallas_call(
        _final_kernel,
        grid_spec=grid_spec,
        out_shape=jax.ShapeDtypeStruct((T, D), F32),
        compiler_params=_cparams(1),
        name="moe_residual_final_norm",
    )(tok_row, x_mid, g2, ys, g[None, :].astype(F32))


def kernel(x, c, ctx, c_ctx, ada_w, ada_b, norm1_g, norm2_g, w_in, hy_short_w, hy_short_b, hy_f_w1, hy_f_b1, hy_f_freq, hy_f_w2, hy_f_b2, hy_f_w3, hy_bias, gm_ln_g, gm_ln_b, gm_ws, gm_bs, cv_dw_w, cv_dw_b, cv_ln_g, cv_ln_b, mla_qa_norm, w_uq, mla_kva_norm, w_ukv, mix_norm_g, w_out, w_router, router_bias, exp_w_gate, exp_w_up, exp_w_down, final_norm_g):
    B, n_lat, D = x.shape
    n_ctx = ctx.shape[1]
    T_lat, T_ctx = B * n_lat, B * n_ctx
    tm_lat = min(512, n_lat)
    tm_ctx = min(256, n_ctx)
    assert n_lat % tm_lat == 0 and n_ctx % tm_ctx == 0 and n_lat % ATT_TILE == 0
    assert tm_lat % GM_CHUNK == 0 and tm_ctx % GM_CHUNK == 0 and T_lat % MOE_TILE == 0 and T_ctx % MOE_TILE == 0

    tps_lat, tps_ctx = n_lat // tm_lat, n_ctx // tm_ctx
    rope_lat = _rope_slot_tables(n_lat)
    rope_ctx = _identity_slot_tables(tm_ctx)
    wr_pad = jnp.pad(w_router.astype(F32), ((0, 0), (0, LANES - N_EXPERTS)))
    wr_hi = wr_pad.astype(BF16)
    wr_lo = (wr_pad - wr_hi.astype(F32)).astype(BF16)
    rbias = router_bias.astype(F32)[:, None]
    cond = jnp.concatenate([c, c_ctx[None, :]], axis=0)
    streams = {
        'lat': dict(x=x.reshape(T_lat, D), L=n_lat, tm=tm_lat, mod_map=lambda i: (i // tps_lat, 0, 0),
                    rope=rope_lat, rope_map=lambda i: (i % tps_lat, 0), row0=0, moe_in=None),
        'ctx': dict(x=ctx.reshape(T_ctx, D), L=n_ctx, tm=tm_ctx, mod_map=lambda i: (B, 0, 0),
                    rope=rope_ctx, rope_map=lambda i: (0, 0), row0=T_lat, moe_in=None),
    }

    for l in range(DEPTH):
        last = l == DEPTH - 1
        m = jax.nn.silu(cond) @ ada_w[l] + ada_b[l]
        sh1, sc1, g1, sh2, sc2, g2 = [t[:, None, :] for t in jnp.split(m, 6, axis=-1)]
        a1 = norm1_g[l][None, None, :] * (1.0 + sc1)
        a2 = norm2_g[l][None, None, :] * (1.0 + sc2)
        qg = jnp.pad(mla_qa_norm[l], (0, PQ_PAD - MLA_Q_RANK))[None, :].astype(F32)
        kvg = mla_kva_norm[l][None, :].astype(F32)
        win, wuq, wkv = _pack_w_in(w_in[l]), _pack_w_uq(w_uq[l]), _pack_w_ukv(w_ukv[l])
        filt = (hy_f_w1[l], hy_f_b1[l], hy_f_freq[l], hy_f_w2[l], hy_f_b2[l], hy_f_w3[l])
        wg, wu, wd = exp_w_gate[l].astype(BF16), exp_w_up[l].astype(BF16), exp_w_down[l].astype(BF16)

        proj = {name: _proj_call(s['x'], a1, sh1, s['mod_map'], win, qg, wuq, kvg, wkv, s['rope'], s['rope_map'],
                                 s['tm'], s['moe_in']) for name, s in streams.items()}
        for name, s in streams.items():
            if s['moe_in'] is not None:
                s['x'] = proj[name][6]
        k_ctx, v_ctx = proj['ctx'][4], proj['ctx'][5]

        active = ('lat',) if last else ('lat', 'ctx')
        x_mid, cls, rank = {}, {}, {}
        t_moe = T_lat if last else T_lat + T_ctx
        hfx = None if last else jnp.zeros((t_moe, D // 2 + LANES), jnp.uint32)
        counts = jnp.zeros((CLASS_ROWS, LANES), F32)
        for name in active:
            s = streams[name]
            p_hy, p_gm, p_cv, q, k, v = proj[name][:6]
            if name == 'lat':
                y_at = _attn_lat_call(q, k, v, k_ctx, v_ctx, B, n_lat, n_ctx)
            else:
                y_at = _attn_ctx_call(q, k, v, B, n_ctx)
            spectra = _hyena_filter_spectra(s['L'], *filt)
            y_hy_t = _hyena_call(p_hy, hy_short_w[l], hy_short_b[l], spectra, hy_bias[l], B, s['L'])
            y_gm, y_cv = _mixers_call(p_gm, p_cv, gm_ln_g[l], gm_ln_b[l], gm_ws[l], gm_bs[l], cv_dw_w[l],
                                      cv_dw_b[l], cv_ln_g[l], cv_ln_b[l], s['L'], s['tm'])
            x_mid[name], hfx, cls[name], rank[name], counts = _out_call(
                y_hy_t, y_gm, y_cv, y_at, s['x'], mix_norm_g[l][None, :].astype(F32), w_out[l].astype(BF16),
                g1, a2, sh2, s['mod_map'], wr_hi, wr_lo, rbias, counts, s['L'], s['tm'],
                hfx, t_moe, s['row0'])

        cat = lambda d: jnp.concatenate([d[name][0] for name in active], axis=0)
        order, tile_lo, tile_hi, tile_src, tile_cnt, tok_row = _class_plan(
            cat(cls), cat(rank), counts[:N_CLASSES, 0].astype(jnp.int32), MOE_TILE)
        ys = _moe_pair_call(hfx, order, tile_lo, tile_hi, tile_src, tile_cnt, wg, wu, wd, MOE_TILE)
        for name in active:
            s = streams[name]
            s['x'] = x_mid[name]
            s['moe_in'] = (lax.slice_in_dim(tok_row, s['row0'], s['row0'] + x_mid[name].shape[0]), g2, ys)

    s = streams['lat']
    tok_row, g2, ys = s['moe_in']
    return _final_call(s['x'], tok_row, g2, s['mod_map'], ys, final_norm_g, s['tm']).reshape(B, n_lat, D)
```

```python
import functools
import math

import jax
import jax.numpy as jnp
import numpy as np
from jax import lax
from jax.experimental import pallas as pl
from jax.experimental.pallas import tpu as pltpu

F32 = jnp.float32
BF16 = jnp.bfloat16

D_MODEL = 1024
DEPTH = 2
GRID_W = 64
EPS = 1e-6

D_GROUP = 256
N_MIXERS = 4
HY_ORDER = 2
HY_SHORT = 3
HY_EMB = 33
HY_BANDS = (HY_EMB - 1) // 2
HY_TARGET = 1e-2
HY_FAST_PCT = 0.3
HY_SLOW_PCT = 1.5
GM_CHUNK = 128
GM_HEADS = 4
CV_WIDTH = 31
CV_GROUPS = 4
MLA_HEADS = 4
MLA_NOPE = 64
MLA_ROPE = 32
MLA_V = 64
MLA_Q_RANK = 192
MLA_KV_RANK = 128
ROPE_BASE = 10000.0
N_EXPERTS = 16
N_EXPERT_GROUPS = 4
EXPERTS_PER_GROUP = N_EXPERTS // N_EXPERT_GROUPS
TOP_K = 2
D_EXPERT = 512

HY_COLS = (HY_ORDER + 1) * D_GROUP
GM_COLS = 2 * D_GROUP
CV_COLS = 2 * D_GROUP
MQ_COLS = MLA_Q_RANK
MKV_COLS = MLA_KV_RANK + MLA_ROPE
HY_OFF = 0
GM_OFF = HY_OFF + HY_COLS
CV_OFF = GM_OFF + GM_COLS
MQ_OFF = CV_OFF + CV_COLS
MKV_OFF = MQ_OFF + MQ_COLS
IN_COLS = MKV_OFF + MKV_COLS

LANES = 128
SUBLANES = 8
HEAD_SLOT = LANES
QK_COLS = MLA_HEADS * HEAD_SLOT
V_COLS = MLA_HEADS * MLA_V
PQ_OFF = MQ_OFF
PQ_PAD = 256
PKV_OFF = PQ_OFF + PQ_PAD
PKR_OFF = PKV_OFF + MLA_KV_RANK
PROJ_COLS = PKR_OFF + LANES

PAIR_LO = (0, 0, 0, 1, 1, 2)
PAIR_HI = (1, 2, 3, 2, 3, 3)
N_PAIRS = len(PAIR_LO)
N_CLASSES = N_EXPERT_GROUPS * N_PAIRS
CLASS_ROWS = -(-N_CLASSES // SUBLANES) * SUBLANES

ATT_TILE = 512
ATT_SUB = 256
MOE_TILE = 256
DFT_ROWS = 256
DFT_COLS = 1024
CONV_HALO = 16
VMEM_LIMIT = 56 * 1024 * 1024


def _cparams(n_axes):
    return pltpu.CompilerParams(dimension_semantics=("arbitrary",) * n_axes, vmem_limit_bytes=VMEM_LIMIT)


def _axial_rope_tables(n_lat):
    rows = n_lat // GRID_W
    row = jnp.repeat(jnp.arange(rows), GRID_W).astype(F32)
    col = jnp.tile(jnp.arange(GRID_W), rows).astype(F32)
    n_freq = MLA_ROPE // 4
    inv = ROPE_BASE ** (-jnp.arange(n_freq, dtype=F32) / n_freq)
    ang = jnp.concatenate([row[:, None] * inv, col[:, None] * inv], axis=-1)
    return jnp.cos(ang), jnp.sin(ang)


def _hyena_filters(L, w1, b1, freq, w2, b2, w3):
    w1, b1, freq, w2, b2, w3 = (a.astype(F32) for a in (w1, b1, freq, w2, b2, w3))
    t_all = jnp.linspace(0.0, 1.0, L, dtype=F32)
    f = jnp.linspace(1e-4, HY_BANDS - 1, HY_BANDS, dtype=F32)[None, :]
    deltas = jnp.abs(jnp.linspace(math.log(HY_TARGET) / HY_SLOW_PCT,
                                  math.log(HY_TARGET) / HY_FAST_PCT, D_GROUP, dtype=F32))

    def taps(pos, direction):
        t = t_all[pos][:, None]
        w = 2.0 * math.pi * pos.astype(F32)[:, None] / L
        z = jnp.concatenate([t, jnp.cos(f * w), -jnp.sin(f * w)], axis=-1)
        h = jnp.sin(freq * (z @ w1 + b1))
        h = jnp.sin(freq * (h @ w2 + b2))
        h = (h @ w3).reshape(L, HY_ORDER, 2, D_GROUP)[:, :, direction]
        return h * jnp.exp(-t * deltas)[:, None, :]

    m = jnp.arange(L, dtype=jnp.int32)
    kf = taps(m, 0)
    kb = jnp.where((m > 0)[:, None, None], taps((L - m) % L, 1), 0.0)
    norm = jnp.sum(jnp.abs(kf), axis=0, keepdims=True) + jnp.sum(jnp.abs(kb), axis=0, keepdims=True)
    w_cols = HY_ORDER * D_GROUP
    return jnp.concatenate([(kf / norm).reshape(L, w_cols), (kb / norm).reshape(L, w_cols)], axis=1)


def _pack_w_in(w_in):
    D = w_in.shape[0]
    z = lambda n: jnp.zeros((D, n), w_in.dtype)
    return jnp.concatenate([
        w_in[:, :MQ_OFF],
        w_in[:, MQ_OFF:MKV_OFF], z(PQ_PAD - MQ_COLS),
        w_in[:, MKV_OFF:MKV_OFF + MLA_KV_RANK],
        w_in[:, MKV_OFF + MLA_KV_RANK:], z(LANES - MLA_ROPE),
    ], axis=1).astype(BF16)


def _pack_w_uq(w_uq):
    w = w_uq.reshape(MLA_Q_RANK, MLA_HEADS, MLA_NOPE + MLA_ROPE)
    w = jnp.pad(w, ((0, PQ_PAD - MLA_Q_RANK), (0, 0), (0, HEAD_SLOT - MLA_NOPE - MLA_ROPE)))
    return w.reshape(PQ_PAD, QK_COLS).astype(BF16)


def _pack_w_ukv(w_ukv):
    w = w_ukv.reshape(MLA_KV_RANK, MLA_HEADS, MLA_NOPE + MLA_V)
    k_part = jnp.pad(w[:, :, :MLA_NOPE], ((0, 0), (0, 0), (0, HEAD_SLOT - MLA_NOPE))).reshape(MLA_KV_RANK, QK_COLS)
    v_part = w[:, :, MLA_NOPE:].reshape(MLA_KV_RANK, V_COLS)
    top = jnp.concatenate([k_part, v_part], axis=1)
    eye = jnp.eye(LANES, dtype=w_ukv.dtype)[:, :MLA_ROPE]
    place = jnp.pad(eye, ((0, 0), (MLA_NOPE, HEAD_SLOT - MLA_NOPE - MLA_ROPE)))
    bot = jnp.concatenate([jnp.tile(place, (1, MLA_HEADS)), jnp.zeros((LANES, V_COLS), w_ukv.dtype)], axis=1)
    return jnp.concatenate([top, bot], axis=0).astype(BF16)


def _rope_slot_tables(n_lat):
    cos, sin = _axial_rope_tables(n_lat)
    half = MLA_ROPE // 2
    tail_w = HEAD_SLOT - MLA_NOPE - MLA_ROPE
    ones = lambda n: jnp.ones((n_lat, n), F32)
    zeros = lambda n: jnp.zeros((n_lat, n), F32)
    cf = jnp.concatenate([ones(MLA_NOPE), cos, cos, ones(tail_w)], axis=1)
    s_up = jnp.concatenate([zeros(MLA_NOPE + half), sin, zeros(tail_w)], axis=1)
    s_dn = jnp.concatenate([zeros(MLA_NOPE), -sin, zeros(half + tail_w)], axis=1)
    return cf, s_up, s_dn


def _identity_slot_tables(rows):
    return (jnp.ones((rows, HEAD_SLOT), F32), jnp.zeros((rows, HEAD_SLOT), F32), jnp.zeros((rows, HEAD_SLOT), F32))


def _rotate_slots(t, cf, s_up, s_dn):
    half = MLA_ROPE // 2
    outs = []
    for h in range(MLA_HEADS):
        tb = t[:, h * HEAD_SLOT:(h + 1) * HEAD_SLOT]
        outs.append(tb * cf + pltpu.roll(tb, half, 1) * s_up + pltpu.roll(tb, HEAD_SLOT - half, 1) * s_dn)
    return jnp.concatenate(outs, axis=1)


def _row_gather_copy(src_hbm, row, buf_ref, slot, j, sem_ref):
    return pltpu.make_async_copy(src_hbm.at[pl.ds(row, 1)], buf_ref.at[slot, pl.ds(j, 1)], sem_ref.at[slot])


def _start_row_gather(src_hbm, row_of, n_rows, buf_ref, slot, sem_ref):
    for j in range(n_rows):
        _row_gather_copy(src_hbm, row_of(j), buf_ref, slot, j, sem_ref).start()


def _wait_row_gather(src_hbm, n_rows, buf_ref, slot, sem_ref):
    pltpu.make_async_copy(src_hbm.at[pl.ds(0, n_rows)], buf_ref.at[slot], sem_ref.at[slot]).wait()


def _gathered_rows(idx_ref, ys_hbm, buf_ref, sem_ref, tm):
    i = pl.program_id(0)
    n = pl.num_programs(0)
    slot = i % 2

    @pl.when(i == 0)
    def _():
        _start_row_gather(ys_hbm, lambda j: idx_ref[j], tm, buf_ref, 0, sem_ref)

    nxt = jnp.minimum(i + 1, n - 1)
    _start_row_gather(ys_hbm, lambda j: idx_ref[nxt * tm + j], tm, buf_ref, 1 - slot, sem_ref)
    _wait_row_gather(ys_hbm, tm, buf_ref, slot, sem_ref)
    return buf_ref[slot]


def _drain_gathered_rows(ys_hbm, buf_ref, sem_ref, tm):
    i = pl.program_id(0)

    @pl.when(i == pl.num_programs(0) - 1)
    def _():
        _wait_row_gather(ys_hbm, tm, buf_ref, 1 - i % 2, sem_ref)


def _proj_gather_kernel(idx_ref, xm_ref, g2_ref, ys_hbm, a_ref, sh_ref, win_ref, qg_ref, wuq_ref, kvg_ref, wkv_ref,
                        cf_ref, su_ref, sd_ref, hy_ref, gm_ref, cv_ref, q_ref, k_ref, v_ref, xn_ref, buf_ref, sem_ref):
    y = _gathered_rows(idx_ref, ys_hbm, buf_ref, sem_ref, xm_ref.shape[0])
    xn_ref[...] = xm_ref[...] + g2_ref[...] * y
    _proj_kernel(xn_ref, a_ref, sh_ref, win_ref, qg_ref, wuq_ref, kvg_ref, wkv_ref, cf_ref, su_ref, sd_ref,
                 hy_ref, gm_ref, cv_ref, q_ref, k_ref, v_ref)
    _drain_gathered_rows(ys_hbm, buf_ref, sem_ref, xm_ref.shape[0])


def _proj_kernel(x_ref, a_ref, sh_ref, win_ref, qg_ref, wuq_ref, kvg_ref, wkv_ref, cf_ref, su_ref, sd_ref,
                 hy_ref, gm_ref, cv_ref, q_ref, k_ref, v_ref):
    x = x_ref[...]
    ms = jnp.mean(x * x, axis=-1, keepdims=True)
    h = x * lax.rsqrt(ms + EPS) * a_ref[...] + sh_ref[...]
    p = jnp.dot(h.astype(BF16), win_ref[...], preferred_element_type=F32)
    hy_ref[...] = p[:, HY_OFF:GM_OFF]
    gm_ref[...] = p[:, GM_OFF:CV_OFF]
    cv_ref[...] = p[:, CV_OFF:MQ_OFF]

    cf, su, sd = cf_ref[...], su_ref[...], sd_ref[...]
    cq = p[:, PQ_OFF:PQ_OFF + PQ_PAD]
    qn = cq * lax.rsqrt(jnp.sum(cq * cq, axis=-1, keepdims=True) * (1.0 / MLA_Q_RANK) + EPS) * qg_ref[...]
    q = jnp.dot(qn.astype(BF16), wuq_ref[...], preferred_element_type=F32)
    q = _rotate_slots(q, cf, su, sd) * (1.0 / math.sqrt(MLA_NOPE + MLA_ROPE))
    q_ref[...] = q.astype(BF16)

    ckv = p[:, PKV_OFF:PKV_OFF + MLA_KV_RANK]
    kvn = ckv * lax.rsqrt(jnp.mean(ckv * ckv, axis=-1, keepdims=True) + EPS) * kvg_ref[...]
    kin = jnp.concatenate([kvn, p[:, PKR_OFF:PKR_OFF + LANES]], axis=1).astype(BF16)
    kv = jnp.dot(kin, wkv_ref[...], preferred_element_type=F32)
    k_ref[...] = _rotate_slots(kv[:, :QK_COLS], cf, su, sd).astype(BF16)
    v_ref[...] = kv[:, QK_COLS:].astype(BF16)


def _proj_call(xt, mod_a, mod_sh, mod_map, win, qg, wuq, kvg, wkv, rope_tabs, rope_map, tm, moe_in=None):
    T, D = xt.shape
    const = lambda i, *_: (0, 0)
    tok = lambda i, *_: (i, 0)
    mod = lambda i, *_: mod_map(i)
    rope = lambda i, *_: rope_map(i)
    out_cols = (HY_COLS, GM_COLS, CV_COLS, QK_COLS, QK_COLS, V_COLS)
    out_dtypes = (F32, F32, F32, BF16, BF16, BF16)
    in_specs = [
        pl.BlockSpec((None, 1, D), mod),
        pl.BlockSpec((None, 1, D), mod),
        pl.BlockSpec(win.shape, const),
        pl.BlockSpec(qg.shape, const),
        pl.BlockSpec(wuq.shape, const),
        pl.BlockSpec(kvg.shape, const),
        pl.BlockSpec(wkv.shape, const),
        pl.BlockSpec((tm, HEAD_SLOT), rope),
        pl.BlockSpec((tm, HEAD_SLOT), rope),
        pl.BlockSpec((tm, HEAD_SLOT), rope),
    ]
    out_specs = [pl.BlockSpec((tm, n), tok) for n in out_cols]
    out_shape = [jax.ShapeDtypeStruct((T, n), dt) for n, dt in zip(out_cols, out_dtypes)]
    shared = (mod_a, mod_sh, win, qg, wuq, kvg, wkv, *rope_tabs)
    if moe_in is None:
        return pl.pallas_call(
            _proj_kernel,
            grid=(T // tm,),
            in_specs=[pl.BlockSpec((tm, D), tok)] + in_specs,
            out_specs=out_specs,
            out_shape=out_shape,
            compiler_params=_cparams(1),
            name="proj_qkv",
        )(xt, *shared)
    tok_row, g2, ys = moe_in
    grid_spec = pltpu.PrefetchScalarGridSpec(
        num_scalar_prefetch=1,
        grid=(T // tm,),
        in_specs=[pl.BlockSpec((tm, D), tok), pl.BlockSpec((None, 1, D), mod),
                  pl.BlockSpec(memory_space=pl.ANY)] + in_specs,
        out_specs=out_specs + [pl.BlockSpec((tm, D), tok)],
        scratch_shapes=[pltpu.VMEM((2, tm, ys.shape[1]), ys.dtype), pltpu.SemaphoreType.DMA((2,))],
    )
    return pl.pallas_call(
        _proj_gather_kernel,
        grid_spec=grid_spec,
        out_shape=out_shape + [jax.ShapeDtypeStruct((T, D), F32)],
        compiler_params=_cparams(1),
        name="moe_residual_proj_qkv",
    )(tok_row, xt, g2, ys, *shared)


def _attend_heads(q_ref, key_refs, val_refs, o_ref):
    nt = (((1,), (1,)), ((), ()))
    lane = lax.broadcasted_iota(jnp.int32, (1, V_COLS), 1)
    vals = [v_ref[...] for v_ref in val_refs]
    sub = min(ATT_SUB, q_ref.shape[0])
    for r in range(q_ref.shape[0] // sub):
        rows = slice(r * sub, (r + 1) * sub)
        acc = jnp.zeros((sub, V_COLS), F32)
        for h in range(MLA_HEADS):
            sl_h = slice(h * HEAD_SLOT, (h + 1) * HEAD_SLOT)
            q = q_ref[rows, sl_h]
            scores = [lax.dot_general(q, k_ref[:, sl_h], nt, preferred_element_type=F32) for k_ref in key_refs]
            m = functools.reduce(jnp.maximum, [jnp.max(s, axis=-1, keepdims=True) for s in scores])
            probs = [jnp.exp(s - m) for s in scores]
            denom = functools.reduce(jnp.add, [jnp.sum(p, axis=-1, keepdims=True) for p in probs])
            o = functools.reduce(jnp.add, [jnp.dot(p.astype(BF16), v, preferred_element_type=F32)
                                           for p, v in zip(probs, vals)])
            in_head = (lane >= h * MLA_V) & (lane < (h + 1) * MLA_V)
            acc = acc + jnp.where(in_head, o / denom, 0.0)
        o_ref[rows, :] = acc


def _attn_lat_kernel(q_ref, kl_ref, kc_ref, vl_ref, vc_ref, o_ref):
    _attend_heads(q_ref, (kl_ref, kc_ref), (vl_ref, vc_ref), o_ref)


def _attn_ctx_kernel(q_ref, k_ref, v_ref, o_ref):
    _attend_heads(q_ref, (k_ref,), (v_ref,), o_ref)


def _attn_lat_call(q, k, v, k_ctx, v_ctx, n_batch, n_lat, n_ctx):
    tq = ATT_TILE
    qt = n_lat // tq
    return pl.pallas_call(
        _attn_lat_kernel,
        grid=(n_batch, qt),
        in_specs=[
            pl.BlockSpec((tq, QK_COLS), lambda b, j: (b * qt + j, 0)),
            pl.BlockSpec((n_lat, QK_COLS), lambda b, j: (b, 0)),
            pl.BlockSpec((n_ctx, QK_COLS), lambda b, j: (b, 0)),
            pl.BlockSpec((n_lat, V_COLS), lambda b, j: (b, 0)),
            pl.BlockSpec((n_ctx, V_COLS), lambda b, j: (b, 0)),
        ],
        out_specs=pl.BlockSpec((tq, V_COLS), lambda b, j: (b * qt + j, 0)),
        out_shape=jax.ShapeDtypeStruct((n_batch * n_lat, V_COLS), F32),
        compiler_params=_cparams(2),
        name="attn_latent",
    )(q, k, k_ctx, v, v_ctx)


def _attn_ctx_call(q, k, v, n_batch, n_ctx):
    blk = lambda b: (b, 0)
    return pl.pallas_call(
        _attn_ctx_kernel,
        grid=(n_batch,),
        in_specs=[pl.BlockSpec((n_ctx, QK_COLS), blk), pl.BlockSpec((n_ctx, QK_COLS), blk),
                  pl.BlockSpec((n_ctx, V_COLS), blk)],
        out_specs=pl.BlockSpec((n_ctx, V_COLS), blk),
        out_shape=jax.ShapeDtypeStruct((n_batch * n_ctx, V_COLS), F32),
        compiler_params=_cparams(1),
        name="attn_context",
    )(q, k, v)


@functools.lru_cache(maxsize=None)
def _dft_factor_tables_np(L):
    f = np.arange(L, dtype=np.int64)[:, None]
    def trig(t):
        ang = ((f * t[None, :]) % (2 * L)).astype(np.float64) * (np.pi / L)
        return np.cos(ang).astype(np.float32), np.sin(ang).astype(np.float32)
    return trig(np.arange(L // LANES, dtype=np.int64) * LANES) + trig(np.arange(LANES, dtype=np.int64))


def _dft_matrices(L):
    assert L % LANES == 0
    c1, s1, c0, s0 = (jnp.asarray(a) for a in _dft_factor_tables_np(L))
    c1, s1, c0, s0 = c1[:, :, None], s1[:, :, None], c0[:, None, :], s0[:, None, :]
    c = (c1 * c0 - s1 * s0).reshape(L, L)
    s = (s1 * c0 + c1 * s0).reshape(L, L)
    alt = jnp.where(jnp.arange(L) % 2 == 0, 1.0, -1.0).astype(F32)
    row0 = (jnp.arange(L) == 0)
    s_fwd = jnp.where(row0[:, None], alt[None, :], s)
    s_inv = jnp.where(row0[None, :], alt[:, None], s)
    return c.astype(BF16), s_fwd.astype(BF16), s_inv.astype(BF16)


def _spectrum_kernel(c_ref, s_ref, k_ref, kpr_ref, kpi_ref, kqr_ref, kqi_ref):
    tr = c_ref.shape[0]
    L = c_ref.shape[1]
    w = k_ref.shape[1] // 2
    k = k_ref[...]
    a = jnp.dot(c_ref[...], k, preferred_element_type=F32)
    b = jnp.dot(s_ref[...], k, preferred_element_type=F32)
    f = pl.program_id(0) * tr + lax.broadcasted_iota(jnp.int32, (tr, 1), 0)
    sign = jnp.where(f % 2 == 0, 1.0, -1.0)
    first = f == 0
    re = (a[:, :w] + sign * a[:, w:]) * (1.0 / L)
    im = -(b[:, :w] + sign * b[:, w:]) * (1.0 / L)
    nyq = (b[:, :w] + b[:, w:]) * (0.5 / L)
    kpr_ref[...] = jnp.where(first, 0.5 * re, re)
    kpi_ref[...] = jnp.where(first, 0.0, im)
    kqr_ref[...] = jnp.where(first, nyq, re)
    kqi_ref[...] = jnp.where(first, 0.0, im)


def _spectrum_call(cmat, smat, filters):
    L = filters.shape[0]
    w = filters.shape[1] // 2
    halves = filters.astype(BF16)
    tr = min(DFT_ROWS, L)
    row = lambda i: (i, 0)
    return pl.pallas_call(
        _spectrum_kernel,
        grid=(L // tr,),
        in_specs=[pl.BlockSpec((tr, L), row), pl.BlockSpec((tr, L), row),
                  pl.BlockSpec((L, 2 * w), lambda i: (0, 0), pipeline_mode=pl.Buffered(1))],
        out_specs=[pl.BlockSpec((tr, w), row)] * 4,
        out_shape=[jax.ShapeDtypeStruct((L, w), F32)] * 4,
        compiler_params=_cparams(1),
        name="hyena_filter_spectrum",
    )(cmat, smat, halves)


def _hy_prep_kernel(p_ref, w_ref, b_ref, z_ref, v16_ref):
    j = pl.program_id(1)
    p = p_ref[...]
    L = p.shape[0]
    row = lax.broadcasted_iota(jnp.int32, (L, 1), 0)
    prev = jnp.where(row == 0, 0.0, pltpu.roll(p, 1, 0))
    nxt = jnp.where(row == L - 1, 0.0, pltpu.roll(p, L - 1, 0))
    z = prev * w_ref[0:1, :] + p * w_ref[1:2, :] + nxt * w_ref[2:3, :] + b_ref[...]
    z_ref[...] = z

    @pl.when(j == HY_ORDER)
    def _():
        v16_ref[...] = z.astype(BF16)


def _hy_prep_call(p_hy, short_w, short_b, n_batch, L):
    N = n_batch * D_GROUP
    return pl.pallas_call(
        _hy_prep_kernel,
        grid=(n_batch, HY_ORDER + 1),
        in_specs=[
            pl.BlockSpec((L, D_GROUP), lambda b, j: (b, j)),
            pl.BlockSpec((HY_SHORT, D_GROUP), lambda b, j: (0, j)),
            pl.BlockSpec((1, D_GROUP), lambda b, j: (0, j)),
        ],
        out_specs=[pl.BlockSpec((None, L, D_GROUP), lambda b, j: (j, 0, b)),
                   pl.BlockSpec((L, D_GROUP), lambda b, j: (0, b))],
        out_shape=[jax.ShapeDtypeStruct((HY_ORDER + 1, L, N), F32), jax.ShapeDtypeStruct((L, N), BF16)],
        compiler_params=_cparams(2),
        name="hyena_short_conv",
    )(p_hy, short_w, short_b[None, :])


def _dft_fwd_kernel(c_ref, s_ref, u_ref, kpr_ref, kpi_ref, kqr_ref, kqi_ref, p_ref, q_ref):
    u = u_ref[...]
    a = jnp.dot(c_ref[...], u, preferred_element_type=F32)
    b = jnp.dot(s_ref[...], u, preferred_element_type=F32)
    kpr, kpi, kqr, kqi = kpr_ref[...], kpi_ref[...], kqr_ref[...], kqi_ref[...]
    for g in range(u.shape[1] // D_GROUP):
        sl = slice(g * D_GROUP, (g + 1) * D_GROUP)
        p_ref[:, sl] = (kpr * a[:, sl] + kpi * b[:, sl]).astype(BF16)
        q_ref[:, sl] = (kqr * b[:, sl] - kqi * a[:, sl]).astype(BF16)


def _dft_fwd_call(cmat, smat, u16, tabs, order):
    L, N = u16.shape
    tr = min(DFT_ROWS, L)
    tn = min(DFT_COLS, N)
    row = lambda h, i: (i, 0)
    return pl.pallas_call(
        _dft_fwd_kernel,
        grid=(N // tn, L // tr),
        in_specs=[
            pl.BlockSpec((tr, L), row),
            pl.BlockSpec((tr, L), row),
            pl.BlockSpec((L, tn), lambda h, i: (0, h), pipeline_mode=pl.Buffered(1)),
        ] + [pl.BlockSpec((tr, D_GROUP), lambda h, i: (i, order))] * 4,
        out_specs=[pl.BlockSpec((tr, tn), lambda h, i: (i, h))] * 2,
        out_shape=[jax.ShapeDtypeStruct((L, N), BF16)] * 2,
        compiler_params=_cparams(2),
        name="hyena_dft_fwd",
    )(cmat, smat, u16, *tabs)


def _dft_inv_kernel(c_ref, st_ref, p_ref, q_ref, u_ref, g_ref, bias_ref, y_ref, y16_ref):
    conv = (jnp.dot(c_ref[...], p_ref[...], preferred_element_type=F32)
            + jnp.dot(st_ref[...], q_ref[...], preferred_element_type=F32))
    y = g_ref[...] * (conv + u_ref[...] * bias_ref[...])
    y_ref[...] = y
    y16_ref[...] = y.astype(BF16)


def _dft_inv_call(cmat, stmat, p16, q16, u_arr, u_sel, g_arr, g_sel, bias_row):
    L, N = p16.shape
    tr = min(DFT_ROWS, L)
    tn = min(DFT_COLS, N)
    row = lambda h, i: (i, 0)
    res = lambda h, i: (0, h)
    return pl.pallas_call(
        _dft_inv_kernel,
        grid=(N // tn, L // tr),
        in_specs=[
            pl.BlockSpec((tr, L), row),
            pl.BlockSpec((tr, L), row),
            pl.BlockSpec((L, tn), res, pipeline_mode=pl.Buffered(1)),
            pl.BlockSpec((L, tn), res, pipeline_mode=pl.Buffered(1)),
            pl.BlockSpec((None, tr, tn), lambda h, i: (u_sel, i, h)),
            pl.BlockSpec((None, tr, tn), lambda h, i: (g_sel, i, h)),
            pl.BlockSpec((1, tn), res),
        ],
        out_specs=[pl.BlockSpec((tr, tn), lambda h, i: (i, h))] * 2,
        out_shape=[jax.ShapeDtypeStruct((L, N), F32), jax.ShapeDtypeStruct((L, N), BF16)],
        compiler_params=_cparams(2),
        name="hyena_dft_inv",
    )(cmat, stmat, p16, q16, u_arr, g_arr, bias_row)


def _hyena_call(p_hy, short_w, short_b, filters, hy_bias, n_batch, L, dft):
    assert L % 2 == 0
    cmat, smat, stmat = dft
    tabs = _spectrum_call(cmat, smat, filters)
    z, u16 = _hy_prep_call(p_hy, short_w, short_b, n_batch, L)
    y_stack = z
    u_sel = HY_ORDER
    for n in range(HY_ORDER):
        p16, q16 = _dft_fwd_call(cmat, smat, u16, tabs, n)
        bias_row = jnp.tile(hy_bias[n][None, :].astype(F32), (1, n_batch))
        y, u16 = _dft_inv_call(cmat, stmat, p16, q16, y_stack, u_sel, z, n, bias_row)
        y_stack, u_sel = y[None], 0
    return y


def _mixers_kernel(gm_ref, cv_ref, cvp_ref, cvn_ref, lng_ref, lnb_ref, ws_ref, bsf_ref, dww_ref, dwb_ref,
                   cg_ref, cb_ref, avg_ref, ygm_ref, ycv_ref, glu_ref, *, tiles_per_seq):
    i = pl.program_id(0)
    tm = gm_ref.shape[0]
    lane = lax.broadcasted_iota(jnp.int32, (1, D_GROUP), 1)

    z = jax.nn.gelu(gm_ref[...], approximate=True)
    u, v = z[:, :D_GROUP], z[:, D_GROUP:]
    mu = jnp.mean(v, axis=-1, keepdims=True)
    vc = v - mu
    var = jnp.mean(vc * vc, axis=-1, keepdims=True)
    vn = (vc * lax.rsqrt(var + EPS) * lng_ref[...] + lnb_ref[...]).astype(BF16)
    hd = D_GROUP // GM_HEADS
    for c in range(tm // GM_CHUNK):
        rows = slice(c * GM_CHUNK, (c + 1) * GM_CHUNK)
        s = bsf_ref[...]
        for g in range(GM_HEADS):
            sg = jnp.dot(ws_ref[g], vn[rows, :], preferred_element_type=F32)
            s = s + jnp.where((lane >= g * hd) & (lane < (g + 1) * hd), sg, 0.0)
        ygm_ref[rows, :] = u[rows, :] * s

    def glu(t):
        return t[:, :D_GROUP] * jax.nn.sigmoid(t[:, D_GROUP:])

    first = (i % tiles_per_seq) == 0
    last = (i % tiles_per_seq) == tiles_per_seq - 1
    span = tm + 2 * CONV_HALO
    glu_ref[0, 0:CONV_HALO, :] = jnp.where(first, 0.0, glu(cvp_ref[...]))
    glu_ref[0, CONV_HALO:CONV_HALO + tm, :] = glu(cv_ref[...])
    glu_ref[0, CONV_HALO + tm:span, :] = jnp.where(last, 0.0, glu(cvn_ref[...]))
    glu_ref[0, span:, :] = jnp.zeros((SUBLANES, D_GROUP), F32)
    for b in range(1, SUBLANES):
        glu_ref[b, 0:span, :] = glu_ref[0, b:b + span, :]
    pad = (CV_WIDTH - 1) // 2
    rc = 128

    def group_mean(t):
        hi = t.astype(BF16)
        lo = (t - hi.astype(F32)).astype(BF16)
        return (jnp.dot(hi, avg_ref[...], preferred_element_type=F32)
                + jnp.dot(lo, avg_ref[...], preferred_element_type=F32))

    for c in range(tm // rc):
        acc = jnp.zeros((rc, D_GROUP), F32) + dwb_ref[...]
        for k in range(CV_WIDTH):
            start = c * rc + CONV_HALO - pad + k
            b = start % SUBLANES
            acc = acc + glu_ref[b, start - b:start - b + rc, :] * dww_ref[k:k + 1, :]
        d = acc - group_mean(acc)
        gvar = group_mean(d * d)
        n = d * lax.rsqrt(gvar + EPS) * cg_ref[...] + cb_ref[...]
        ycv_ref[c * rc:(c + 1) * rc, :] = n * jax.nn.sigmoid(n)


def _mixers_call(p_gm, p_cv, gm_ln_g, gm_ln_b, gm_ws, gm_bs, cv_dw_w, cv_dw_b, cv_ln_g, cv_ln_b, L, tm):
    T = p_gm.shape[0]
    tps = L // tm
    hb = tm // CONV_HALO
    n_hblk = T // CONV_HALO
    const2 = lambda i: (0, 0)
    tok = lambda i: (i, 0)
    bs_full = jnp.repeat(gm_bs.T.astype(F32), D_GROUP // GM_HEADS, axis=1)
    gid = np.arange(D_GROUP) // (D_GROUP // CV_GROUPS)
    avg = jnp.asarray((gid[:, None] == gid[None, :]).astype(np.float32) / (D_GROUP // CV_GROUPS)).astype(BF16)
    row = lambda a: a[None, :].astype(F32)
    return pl.pallas_call(
        functools.partial(_mixers_kernel, tiles_per_seq=tps),
        grid=(T // tm,),
        in_specs=[
            pl.BlockSpec((tm, GM_COLS), tok),
            pl.BlockSpec((tm, CV_COLS), tok),
            pl.BlockSpec((CONV_HALO, CV_COLS), lambda i: (jnp.maximum(i * hb - 1, 0), 0)),
            pl.BlockSpec((CONV_HALO, CV_COLS), lambda i: (jnp.minimum((i + 1) * hb, n_hblk - 1), 0)),
            pl.BlockSpec((1, D_GROUP), const2),
            pl.BlockSpec((1, D_GROUP), const2),
            pl.BlockSpec((GM_HEADS, GM_CHUNK, GM_CHUNK), lambda i: (0, 0, 0)),
            pl.BlockSpec((GM_CHUNK, D_GROUP), const2),
            pl.BlockSpec((CV_WIDTH, D_GROUP), const2),
            pl.BlockSpec((1, D_GROUP), const2),
            pl.BlockSpec((1, D_GROUP), const2),
            pl.BlockSpec((1, D_GROUP), const2),
            pl.BlockSpec((D_GROUP, D_GROUP), const2),
        ],
        out_specs=[pl.BlockSpec((tm, D_GROUP), tok)] * 2,
        out_shape=[jax.ShapeDtypeStruct((T, D_GROUP), F32)] * 2,
        scratch_shapes=[pltpu.VMEM((SUBLANES, tm + 2 * CONV_HALO + SUBLANES, D_GROUP), F32)],
        compiler_params=_cparams(1),
        name="gmlp_conv_mixers",
    )(p_gm, p_cv, p_cv, p_cv, row(gm_ln_g), row(gm_ln_b), gm_ws.astype(BF16), bs_full, cv_dw_w.astype(F32),
      row(cv_dw_b), row(cv_ln_g), row(cv_ln_b), avg)


def _out_kernel(hy_ref, gm_ref, cv_ref, at_ref, x_ref, mg_ref, wo_ref, g1_ref, a2_ref, sh2_ref, wrh_ref, wrl_ref,
                rb_ref, tri_ref, cin_ref, xm_ref, hfx_ref, cls_ref, rank_ref, cnt_ref, carry_ref):
    i = pl.program_id(0)
    tm = x_ref.shape[0]
    o = None
    for g, y_ref in enumerate((hy_ref, gm_ref, cv_ref, at_ref)):
        y = y_ref[...]
        n = y * lax.rsqrt(jnp.mean(y * y, axis=-1, keepdims=True) + EPS) * mg_ref[:, g * D_GROUP:(g + 1) * D_GROUP]
        part = jnp.dot(n.astype(BF16), wo_ref[g * D_GROUP:(g + 1) * D_GROUP, :], preferred_element_type=F32)
        o = part if o is None else o + part
    xm = x_ref[...] + g1_ref[...] * o
    xm_ref[...] = xm
    hf = xm * lax.rsqrt(jnp.mean(xm * xm, axis=-1, keepdims=True) + EPS) * a2_ref[...] + sh2_ref[...]

    hf_hi = hf.astype(BF16)
    hf_lo = (hf - hf_hi.astype(F32)).astype(BF16)
    logits = (jnp.dot(hf_hi, wrh_ref[...], preferred_element_type=F32)
              + jnp.dot(hf_lo, wrh_ref[...], preferred_element_type=F32)
              + jnp.dot(hf_hi, wrl_ref[...], preferred_element_type=F32))
    cls, g_lo, g_hi = _route_top2(jnp.transpose(logits), rb_ref[...])
    cls_ref[...] = cls

    d_model = hf.shape[1]
    hfx_ref[:, :d_model] = hf
    gate_rows = jnp.concatenate([g_lo, g_hi, jnp.zeros((LANES - TOP_K, tm), F32)], axis=0)
    hfx_ref[:, d_model:] = jnp.transpose(gate_rows)

    @pl.when(i == 0)
    def _():
        carry_ref[...] = cin_ref[...]

    sub = lax.broadcasted_iota(jnp.int32, (CLASS_ROWS, tm), 0)
    onehot = sub == cls
    prefix = jnp.dot(onehot.astype(BF16), tri_ref[...], preferred_element_type=F32)
    carry = carry_ref[...]
    rank = jnp.sum(jnp.where(onehot, prefix + carry[:, 0:1], 0.0), axis=0, keepdims=True)
    rank_ref[...] = rank.astype(jnp.int32)
    carry = carry + jnp.sum(onehot.astype(F32), axis=1, keepdims=True)
    carry_ref[...] = carry
    cnt_ref[...] = carry


def _out_alias_kernel(*refs):
    _out_kernel(*refs[1:])


def _out_call(y_hy_t, y_gm, y_cv, y_at, xt, mixg, wo, g1, a2, sh2, mod_map, wr_hi, wr_lo, rbias, counts_in, L, tm,
              hfx_prev, t_total, row0):
    T, D = xt.shape
    tps = L // tm
    blk0 = row0 // tm
    const = lambda i: (0, 0)
    tok = lambda i: (i, 0)
    lane_tok = lambda i: (0, i)
    tri = jnp.asarray(np.triu(np.ones((tm, tm), np.float32), k=1)).astype(BF16)
    W = D + LANES
    in_specs = [pl.BlockSpec((tm, D_GROUP), lambda i: (i % tps, i // tps))] + [pl.BlockSpec((tm, D_GROUP), tok)] * 3 + [
        pl.BlockSpec((tm, D), tok),
        pl.BlockSpec(mixg.shape, const),
        pl.BlockSpec(wo.shape, const),
        pl.BlockSpec((None, 1, D), mod_map),
        pl.BlockSpec((None, 1, D), mod_map),
        pl.BlockSpec((None, 1, D), mod_map),
        pl.BlockSpec(wr_hi.shape, const),
        pl.BlockSpec(wr_lo.shape, const),
        pl.BlockSpec(rbias.shape, const),
        pl.BlockSpec((tm, tm), const),
        pl.BlockSpec((CLASS_ROWS, LANES), const),
    ]
    args = (y_hy_t, y_gm, y_cv, y_at, xt, mixg, wo, g1, a2, sh2, wr_hi, wr_lo, rbias, tri, counts_in)
    aliased = hfx_prev is not None
    return pl.pallas_call(
        _out_alias_kernel if aliased else _out_kernel,
        grid=(T // tm,),
        in_specs=([pl.BlockSpec(memory_space=pl.ANY)] if aliased else []) + in_specs,
        out_specs=[pl.BlockSpec((tm, D), tok), pl.BlockSpec((tm, W), lambda i: (blk0 + i, 0)),
                   pl.BlockSpec((1, tm), lane_tok), pl.BlockSpec((1, tm), lane_tok),
                   pl.BlockSpec((CLASS_ROWS, LANES), const)],
        out_shape=[jax.ShapeDtypeStruct((T, D), F32), jax.ShapeDtypeStruct((t_total, W), F32),
                   jax.ShapeDtypeStruct((1, T), jnp.int32), jax.ShapeDtypeStruct((1, T), jnp.int32),
                   jax.ShapeDtypeStruct((CLASS_ROWS, LANES), F32)],
        scratch_shapes=[pltpu.VMEM((CLASS_ROWS, LANES), F32)],
        input_output_aliases={0: 1} if aliased else {},
        compiler_params=_cparams(1),
        name="mix_out_norm2_route",
    )(*(((hfx_prev,) if aliased else ()) + args))


def _first_max_flags(vals):
    m = functools.reduce(jnp.maximum, vals)
    flags, taken = [], None
    for v in vals:
        f = v >= m
        if taken is not None:
            f = f & jnp.logical_not(taken)
        flags.append(f)
        taken = f if taken is None else taken | f
    return flags, m


def _pick(flags, vals):
    out = vals[-1]
    for f, v in zip(flags[-2::-1], vals[-2::-1]):
        out = jnp.where(f, v, out)
    return out


def _route_top2(lt, bias_col):
    s_all = jax.nn.sigmoid(lt[:N_EXPERTS, :])
    sel_all = s_all + bias_col
    s = [s_all[e:e + 1, :] for e in range(N_EXPERTS)]
    sel = [sel_all[e:e + 1, :] for e in range(N_EXPERTS)]
    neg = -jnp.inf
    E = EXPERTS_PER_GROUP

    def top2(vals):
        f1, m1 = _first_max_flags(vals)
        rest = [jnp.where(f, neg, v) for f, v in zip(f1, vals)]
        f2, m2 = _first_max_flags(rest)
        return f1, m1, f2, m2

    scores = []
    for g in range(N_EXPERT_GROUPS):
        _, m1, _, m2 = top2(sel[g * E:(g + 1) * E])
        scores.append(m1 + m2)
    gflags, _ = _first_max_flags(scores)
    bsel = [_pick(gflags, [sel[g * E + j] for g in range(N_EXPERT_GROUPS)]) for j in range(E)]
    bs = [_pick(gflags, [s[g * E + j] for g in range(N_EXPERT_GROUPS)]) for j in range(E)]
    f1, _, f2, _ = top2(bsel)
    zero = jnp.zeros_like(bs[0])
    w1 = functools.reduce(jnp.add, [jnp.where(f, v, zero) for f, v in zip(f1, bs)])
    w2 = functools.reduce(jnp.add, [jnp.where(f, v, zero) for f, v in zip(f2, bs)])
    izero = jnp.zeros(w1.shape, jnp.int32)
    j1 = functools.reduce(jnp.add, [jnp.where(f, j, izero) for j, f in enumerate(f1)])
    j2 = functools.reduce(jnp.add, [jnp.where(f, j, izero) for j, f in enumerate(f2)])
    gi = functools.reduce(jnp.add, [jnp.where(f, g, izero) for g, f in enumerate(gflags)])
    lo, hi = jnp.minimum(j1, j2), jnp.maximum(j1, j2)
    pair = jnp.where(lo == 0, 0, jnp.where(lo == 1, 3, 5)) + hi - lo - 1
    tot = w1 + w2
    first_is_lo = j1 < j2
    return gi * N_PAIRS + pair, jnp.where(first_is_lo, w1, w2) / tot, jnp.where(first_is_lo, w2, w1) / tot


def _table_lookup(table, idx):
    n = table.shape[0]
    hit = idx[..., None] == jnp.arange(n, dtype=jnp.int32)
    return jnp.sum(jnp.where(hit, table, 0), axis=-1)


def _class_plan(cls, rank, counts, tile):
    T = cls.shape[0]
    n_tiles = (T + N_CLASSES * tile) // tile
    order = jnp.argsort(cls, stable=True).astype(jnp.int32)
    padded = ((counts + tile - 1) // tile) * tile
    seg_end = jnp.cumsum(padded)
    seg_start = seg_end - padded
    src_start = jnp.cumsum(counts) - counts
    tok_row = _table_lookup(seg_start, cls) + rank

    tile_first = jnp.arange(n_tiles, dtype=jnp.int32) * tile
    tile_class = jnp.minimum(jnp.sum((seg_end[None, :] <= tile_first[:, None]).astype(jnp.int32), axis=1),
                             N_CLASSES - 1)
    off = tile_first - _table_lookup(seg_start, tile_class)
    tile_cnt = jnp.clip(_table_lookup(counts, tile_class) - off, 0, tile)
    tile_src = jnp.clip(_table_lookup(src_start, tile_class) + off, 0, T - 1)
    grp = tile_class // N_PAIRS
    pair = tile_class % N_PAIRS
    tile_lo = grp * EXPERTS_PER_GROUP + _table_lookup(jnp.asarray(PAIR_LO, jnp.int32), pair)
    tile_hi = grp * EXPERTS_PER_GROUP + _table_lookup(jnp.asarray(PAIR_HI, jnp.int32), pair)
    return order, tile_lo, tile_hi, tile_src, tile_cnt, tok_row


def _moe_pair_kernel(order_ref, lo_ref, hi_ref, src_ref, cnt_ref, hfx_hbm, wga_ref, wua_ref, wda_ref, wgb_ref,
                     wub_ref, wdb_ref, o_ref, buf_ref, sem_ref):
    i = pl.program_id(0)
    n = pl.num_programs(0)
    tile = o_ref.shape[0]
    n_tok = order_ref.shape[0]
    slot = i % 2

    def start(t, s):
        base = src_ref[t]
        _start_row_gather(hfx_hbm, lambda j: order_ref[jnp.minimum(base + j, n_tok - 1)], tile, buf_ref, s, sem_ref)

    @pl.when((i == 0) & (cnt_ref[0] > 0))
    def _():
        start(0, 0)

    @pl.when((cnt_ref[i] <= 0) & (i > 0) & (cnt_ref[jnp.maximum(i - 1, 0)] > 0))
    def _():
        _wait_row_gather(hfx_hbm, tile, buf_ref, slot, sem_ref)

    @pl.when(cnt_ref[i] > 0)
    def _():
        start(jnp.minimum(i + 1, n - 1), 1 - slot)
        _wait_row_gather(hfx_hbm, tile, buf_ref, slot, sem_ref)
        d_model = buf_ref.shape[2] - LANES
        x = buf_ref[slot, :, :d_model].astype(BF16)
        live = lax.broadcasted_iota(jnp.int32, (tile, 1), 0) < cnt_ref[i]
        gates = buf_ref[slot, :, d_model:]
        g_lo = jnp.where(live, gates[:, 0:1], 0.0)
        g_hi = jnp.where(live, gates[:, 1:2], 0.0)

        def hidden(wg_ref, wu_ref, gate):
            hg = jnp.dot(x, wg_ref[...], preferred_element_type=F32)
            hu = jnp.dot(x, wu_ref[...], preferred_element_type=F32)
            return (hg * jax.nn.sigmoid(hg) * hu * gate).astype(BF16)

        o_ref[...] = (jnp.dot(hidden(wga_ref, wua_ref, g_lo), wda_ref[...], preferred_element_type=F32)
                      + jnp.dot(hidden(wgb_ref, wub_ref, g_hi), wdb_ref[...], preferred_element_type=F32))

    @pl.when(cnt_ref[i] <= 0)
    def _():
        o_ref[...] = jnp.zeros_like(o_ref)


def _moe_pair_call(hfx, order, tile_lo, tile_hi, tile_src, tile_cnt, wg, wu, wd, tile):
    T, W = hfx.shape
    D = W - LANES
    F = wg.shape[-1]
    n_tiles = tile_lo.shape[0]
    lo = lambda i, o, tl, th, ts, tc: (tl[i], 0, 0)
    hi = lambda i, o, tl, th, ts, tc: (th[i], 0, 0)
    grid_spec = pltpu.PrefetchScalarGridSpec(
        num_scalar_prefetch=5,
        grid=(n_tiles,),
        in_specs=[
            pl.BlockSpec(memory_space=pl.ANY),
            pl.BlockSpec((None, D, F), lo), pl.BlockSpec((None, D, F), lo), pl.BlockSpec((None, F, D), lo),
            pl.BlockSpec((None, D, F), hi), pl.BlockSpec((None, D, F), hi), pl.BlockSpec((None, F, D), hi),
        ],
        out_specs=pl.BlockSpec((tile, D), lambda i, o, tl, th, ts, tc: (i, 0)),
        scratch_shapes=[pltpu.VMEM((2, tile, W), F32), pltpu.SemaphoreType.DMA((2,))],
    )
    return pl.pallas_call(
        _moe_pair_kernel,
        grid_spec=grid_spec,
        out_shape=jax.ShapeDtypeStruct((n_tiles * tile, D), F32),
        compiler_params=_cparams(1),
        name="moe_pair_grouped",
    )(order, tile_lo, tile_hi, tile_src, tile_cnt, hfx, wg, wu, wd, wg, wu, wd)


def _final_kernel(idx_ref, xm_ref, g2_ref, ys_hbm, g_ref, o_ref, buf_ref, sem_ref):
    x = xm_ref[...] + g2_ref[...] * _gathered_rows(idx_ref, ys_hbm, buf_ref, sem_ref, xm_ref.shape[0])
    o_ref[...] = x * lax.rsqrt(jnp.mean(x * x, axis=-1, keepdims=True) + EPS) * g_ref[...]
    _drain_gathered_rows(ys_hbm, buf_ref, sem_ref, xm_ref.shape[0])


def _final_call(x_mid, tok_row, g2, mod_map, ys, g, tm):
    T, D = x_mid.shape
    tok = lambda i, *_: (i, 0)
    grid_spec = pltpu.PrefetchScalarGridSpec(
        num_scalar_prefetch=1,
        grid=(T // tm,),
        in_specs=[pl.BlockSpec((tm, D), tok), pl.BlockSpec((None, 1, D), lambda i, *_: mod_map(i)),
                  pl.BlockSpec(memory_space=pl.ANY), pl.BlockSpec((1, D), lambda i, *_: (0, 0))],
        out_specs=pl.BlockSpec((tm, D), tok),
        scratch_shapes=[pltpu.VMEM((2, tm, ys.shape[1]), ys.dtype), pltpu.SemaphoreType.DMA((2,))],
    )
    return pl.pallas_call(
        _final_kernel,
        grid_spec=grid_spec,
        out_shape=jax.ShapeDtypeStruct((T, D), F32),
        compiler_params=_cparams(1),
        name="moe_residual_final_norm",
    )(tok_row, x_mid, g2, ys, g[None, :].astype(F32))


def kernel(x, c, ctx, c_ctx, ada_w, ada_b, norm1_g, norm2_g, w_in, hy_short_w, hy_short_b, hy_f_w1, hy_f_b1, hy_f_freq, hy_f_w2, hy_f_b2, hy_f_w3, hy_bias, gm_ln_g, gm_ln_b, gm_ws, gm_bs, cv_dw_w, cv_dw_b, cv_ln_g, cv_ln_b, mla_qa_norm, w_uq, mla_kva_norm, w_ukv, mix_norm_g, w_out, w_router, router_bias, exp_w_gate, exp_w_up, exp_w_down, final_norm_g):
    B, n_lat, D = x.shape
    n_ctx = ctx.shape[1]
    T_lat, T_ctx = B * n_lat, B * n_ctx
    tm_lat = min(512, n_lat)
    tm_ctx = min(256, n_ctx)
    assert n_lat % tm_lat == 0 and n_ctx % tm_ctx == 0 and n_lat % ATT_TILE == 0
    assert tm_lat % GM_CHUNK == 0 and tm_ctx % GM_CHUNK == 0 and T_lat % MOE_TILE == 0 and T_ctx % MOE_TILE == 0

    tps_lat, tps_ctx = n_lat // tm_lat, n_ctx // tm_ctx
    dft = {L: _dft_matrices(L) for L in {n_lat, n_ctx}}
    rope_lat = _rope_slot_tables(n_lat)
    rope_ctx = _identity_slot_tables(tm_ctx)
    wr_pad = jnp.pad(w_router.astype(F32), ((0, 0), (0, LANES - N_EXPERTS)))
    wr_hi = wr_pad.astype(BF16)
    wr_lo = (wr_pad - wr_hi.astype(F32)).astype(BF16)
    rbias = router_bias.astype(F32)[:, None]
    cond = jnp.concatenate([c, c_ctx[None, :]], axis=0)
    streams = {
        'lat': dict(x=x.reshape(T_lat, D), L=n_lat, tm=tm_lat, mod_map=lambda i: (i // tps_lat, 0, 0),
                    rope=rope_lat, rope_map=lambda i: (i % tps_lat, 0), row0=0, moe_in=None),
        'ctx': dict(x=ctx.reshape(T_ctx, D), L=n_ctx, tm=tm_ctx, mod_map=lambda i: (B, 0, 0),
                    rope=rope_ctx, rope_map=lambda i: (0, 0), row0=T_lat, moe_in=None),
    }

    for l in range(DEPTH):
        last = l == DEPTH - 1
        m = jax.nn.silu(cond) @ ada_w[l] + ada_b[l]
        sh1, sc1, g1, sh2, sc2, g2 = [t[:, None, :] for t in jnp.split(m, 6, axis=-1)]
        a1 = norm1_g[l][None, None, :] * (1.0 + sc1)
        a2 = norm2_g[l][None, None, :] * (1.0 + sc2)
        qg = jnp.pad(mla_qa_norm[l], (0, PQ_PAD - MLA_Q_RANK))[None, :].astype(F32)
        kvg = mla_kva_norm[l][None, :].astype(F32)
        win, wuq, wkv = _pack_w_in(w_in[l]), _pack_w_uq(w_uq[l]), _pack_w_ukv(w_ukv[l])
        filt = (hy_f_w1[l], hy_f_b1[l], hy_f_freq[l], hy_f_w2[l], hy_f_b2[l], hy_f_w3[l])
        wg, wu, wd = exp_w_gate[l].astype(BF16), exp_w_up[l].astype(BF16), exp_w_down[l].astype(BF16)

        proj = {name: _proj_call(s['x'], a1, sh1, s['mod_map'], win, qg, wuq, kvg, wkv, s['rope'], s['rope_map'],
                                 s['tm'], s['moe_in']) for name, s in streams.items()}
        for name, s in streams.items():
            if s['moe_in'] is not None:
                s['x'] = proj[name][6]
        k_ctx, v_ctx = proj['ctx'][4], proj['ctx'][5]

        active = ('lat',) if last else ('lat', 'ctx')
        x_mid, cls, rank = {}, {}, {}
        t_moe = T_lat if last else T_lat + T_ctx
        hfx = None if last else jnp.zeros((t_moe, D + LANES), F32)
        counts = jnp.zeros((CLASS_ROWS, LANES), F32)
        for name in active:
            s = streams[name]
            p_hy, p_gm, p_cv, q, k, v = proj[name][:6]
            if name == 'lat':
                y_at = _attn_lat_call(q, k, v, k_ctx, v_ctx, B, n_lat, n_ctx)
            else:
                y_at = _attn_ctx_call(q, k, v, B, n_ctx)
            filters = _hyena_filters(s['L'], *filt)
            y_hy_t = _hyena_call(p_hy, hy_short_w[l], hy_short_b[l], filters, hy_bias[l], B, s['L'], dft[s['L']])
            y_gm, y_cv = _mixers_call(p_gm, p_cv, gm_ln_g[l], gm_ln_b[l], gm_ws[l], gm_bs[l], cv_dw_w[l],
                                      cv_dw_b[l], cv_ln_g[l], cv_ln_b[l], s['L'], s['tm'])
            x_mid[name], hfx, cls[name], rank[name], counts = _out_call(
                y_hy_t, y_gm, y_cv, y_at, s['x'], mix_norm_g[l][None, :].astype(F32), w_out[l].astype(BF16),
                g1, a2, sh2, s['mod_map'], wr_hi, wr_lo, rbias, counts, s['L'], s['tm'],
                hfx, t_moe, s['row0'])

        cat = lambda d: jnp.concatenate([d[name][0] for name in active], axis=0)
        order, tile_lo, tile_hi, tile_src, tile_cnt, tok_row = _class_plan(
            cat(cls), cat(rank), counts[:N_CLASSES, 0].astype(jnp.int32), MOE_TILE)
        ys = _moe_pair_call(hfx, order, tile_lo, tile_hi, tile_src, tile_cnt, wg, wu, wd, MOE_TILE)
        for name in active:
            s = streams[name]
            s['x'] = x_mid[name]
            s['moe_in'] = (lax.slice_in_dim(tok_row, s['row0'], s['row0'] + x_mid[name].shape[0]), g2, ys)

    s = streams['lat']
    tok_row, g2, ys = s['moe_in']
    return _final_call(s['x'], tok_row, g2, s['mod_map'], ys, final_norm_g, s['tm']).reshape(B, n_lat, D)
```

```python
import functools
import math

import jax
import jax.numpy as jnp
import numpy as np
from jax import lax
from jax.experimental import pallas as pl
from jax.experimental.pallas import tpu as pltpu

F32 = jnp.float32
BF16 = jnp.bfloat16

D_MODEL = 1024
DEPTH = 2
GRID_W = 64
EPS = 1e-6

D_GROUP = 256
N_MIXERS = 4
HY_ORDER = 2
HY_SHORT = 3
HY_EMB = 33
HY_BANDS = (HY_EMB - 1) // 2
HY_TARGET = 1e-2
HY_FAST_PCT = 0.3
HY_SLOW_PCT = 1.5
GM_CHUNK = 128
GM_HEADS = 4
CV_WIDTH = 31
CV_GROUPS = 4
MLA_HEADS = 4
MLA_NOPE = 64
MLA_ROPE = 32
MLA_V = 64
MLA_Q_RANK = 192
MLA_KV_RANK = 128
ROPE_BASE = 10000.0
N_EXPERTS = 16
N_EXPERT_GROUPS = 4
EXPERTS_PER_GROUP = N_EXPERTS // N_EXPERT_GROUPS
TOP_K = 2
D_EXPERT = 512

HY_COLS = (HY_ORDER + 1) * D_GROUP
GM_COLS = 2 * D_GROUP
CV_COLS = 2 * D_GROUP
MQ_COLS = MLA_Q_RANK
MKV_COLS = MLA_KV_RANK + MLA_ROPE
HY_OFF = 0
GM_OFF = HY_OFF + HY_COLS
CV_OFF = GM_OFF + GM_COLS
MQ_OFF = CV_OFF + CV_COLS
MKV_OFF = MQ_OFF + MQ_COLS
IN_COLS = MKV_OFF + MKV_COLS

LANES = 128
SUBLANES = 8
HEAD_SLOT = LANES
QK_COLS = MLA_HEADS * HEAD_SLOT
V_COLS = MLA_HEADS * MLA_V
PQ_OFF = MQ_OFF
PQ_PAD = 256
PKV_OFF = PQ_OFF + PQ_PAD
PKR_OFF = PKV_OFF + MLA_KV_RANK
PROJ_COLS = PKR_OFF + LANES

PAIR_LO = (0, 0, 0, 1, 1, 2)
PAIR_HI = (1, 2, 3, 2, 3, 3)
N_PAIRS = len(PAIR_LO)
N_CLASSES = N_EXPERT_GROUPS * N_PAIRS
CLASS_ROWS = -(-N_CLASSES // SUBLANES) * SUBLANES

ATT_TILE = 512
ATT_SUB = 256
MOE_TILE = 256
DFT_ROWS = 256
DFT_COLS = 1024
CONV_HALO = 16
VMEM_LIMIT = 56 * 1024 * 1024


def _cparams(n_axes):
    return pltpu.CompilerParams(dimension_semantics=("arbitrary",) * n_axes, vmem_limit_bytes=VMEM_LIMIT)


def _axial_rope_tables(n_lat):
    rows = n_lat // GRID_W
    row = jnp.repeat(jnp.arange(rows), GRID_W).astype(F32)
    col = jnp.tile(jnp.arange(GRID_W), rows).astype(F32)
    n_freq = MLA_ROPE // 4
    inv = ROPE_BASE ** (-jnp.arange(n_freq, dtype=F32) / n_freq)
    ang = jnp.concatenate([row[:, None] * inv, col[:, None] * inv], axis=-1)
    return jnp.cos(ang), jnp.sin(ang)


def _hyena_filters(L, w1, b1, freq, w2, b2, w3):
    w1, b1, freq, w2, b2, w3 = (a.astype(F32) for a in (w1, b1, freq, w2, b2, w3))
    t_all = jnp.linspace(0.0, 1.0, L, dtype=F32)
    f = jnp.linspace(1e-4, HY_BANDS - 1, HY_BANDS, dtype=F32)[None, :]
    deltas = jnp.abs(jnp.linspace(math.log(HY_TARGET) / HY_SLOW_PCT,
                                  math.log(HY_TARGET) / HY_FAST_PCT, D_GROUP, dtype=F32))

    def taps(pos, direction):
        t = t_all[pos][:, None]
        w = 2.0 * math.pi * pos.astype(F32)[:, None] / L
        z = jnp.concatenate([t, jnp.cos(f * w), -jnp.sin(f * w)], axis=-1)
        h = jnp.sin(freq * (z @ w1 + b1))
        h = jnp.sin(freq * (h @ w2 + b2))
        h = (h @ w3).reshape(L, HY_ORDER, 2, D_GROUP)[:, :, direction]
        return h * jnp.exp(-t * deltas)[:, None, :]

    m = jnp.arange(L, dtype=jnp.int32)
    kf = taps(m, 0)
    kb = jnp.where((m > 0)[:, None, None], taps((L - m) % L, 1), 0.0)
    norm = jnp.sum(jnp.abs(kf), axis=0, keepdims=True) + jnp.sum(jnp.abs(kb), axis=0, keepdims=True)
    w_cols = HY_ORDER * D_GROUP
    return jnp.concatenate([(kf / norm).reshape(L, w_cols), (kb / norm).reshape(L, w_cols)], axis=1)


def _pack_w_in(w_in):
    D = w_in.shape[0]
    z = lambda n: jnp.zeros((D, n), w_in.dtype)
    return jnp.concatenate([
        w_in[:, :MQ_OFF],
        w_in[:, MQ_OFF:MKV_OFF], z(PQ_PAD - MQ_COLS),
        w_in[:, MKV_OFF:MKV_OFF + MLA_KV_RANK],
        w_in[:, MKV_OFF + MLA_KV_RANK:], z(LANES - MLA_ROPE),
    ], axis=1).astype(BF16)


def _pack_w_uq(w_uq):
    w = w_uq.reshape(MLA_Q_RANK, MLA_HEADS, MLA_NOPE + MLA_ROPE)
    w = jnp.pad(w, ((0, PQ_PAD - MLA_Q_RANK), (0, 0), (0, HEAD_SLOT - MLA_NOPE - MLA_ROPE)))
    return w.reshape(PQ_PAD, QK_COLS).astype(BF16)


def _pack_w_ukv(w_ukv):
    w = w_ukv.reshape(MLA_KV_RANK, MLA_HEADS, MLA_NOPE + MLA_V)
    k_part = jnp.pad(w[:, :, :MLA_NOPE], ((0, 0), (0, 0), (0, HEAD_SLOT - MLA_NOPE))).reshape(MLA_KV_RANK, QK_COLS)
    v_part = w[:, :, MLA_NOPE:].reshape(MLA_KV_RANK, V_COLS)
    top = jnp.concatenate([k_part, v_part], axis=1)
    eye = jnp.eye(LANES, dtype=w_ukv.dtype)[:, :MLA_ROPE]
    place = jnp.pad(eye, ((0, 0), (MLA_NOPE, HEAD_SLOT - MLA_NOPE - MLA_ROPE)))
    bot = jnp.concatenate([jnp.tile(place, (1, MLA_HEADS)), jnp.zeros((LANES, V_COLS), w_ukv.dtype)], axis=1)
    return jnp.concatenate([top, bot], axis=0).astype(BF16)


def _rope_slot_tables(n_lat):
    cos, sin = _axial_rope_tables(n_lat)
    half = MLA_ROPE // 2
    tail_w = HEAD_SLOT - MLA_NOPE - MLA_ROPE
    ones = lambda n: jnp.ones((n_lat, n), F32)
    zeros = lambda n: jnp.zeros((n_lat, n), F32)
    cf = jnp.concatenate([ones(MLA_NOPE), cos, cos, ones(tail_w)], axis=1)
    s_up = jnp.concatenate([zeros(MLA_NOPE + half), sin, zeros(tail_w)], axis=1)
    s_dn = jnp.concatenate([zeros(MLA_NOPE), -sin, zeros(half + tail_w)], axis=1)
    return cf, s_up, s_dn


def _identity_slot_tables(rows):
    return (jnp.ones((rows, HEAD_SLOT), F32), jnp.zeros((rows, HEAD_SLOT), F32), jnp.zeros((rows, HEAD_SLOT), F32))


def _rotate_slots(t, cf, s_up, s_dn):
    half = MLA_ROPE // 2
    outs = []
    for h in range(MLA_HEADS):
        tb = t[:, h * HEAD_SLOT:(h + 1) * HEAD_SLOT]
        outs.append(tb * cf + pltpu.roll(tb, half, 1) * s_up + pltpu.roll(tb, HEAD_SLOT - half, 1) * s_dn)
    return jnp.concatenate(outs, axis=1)


def _row_gather_copy(src_hbm, row, buf_ref, slot, j, sem_ref):
    return pltpu.make_async_copy(src_hbm.at[pl.ds(row, 1)], buf_ref.at[slot, pl.ds(j, 1)], sem_ref.at[slot])


def _start_row_gather(src_hbm, row_of, n_rows, buf_ref, slot, sem_ref):
    for j in range(n_rows):
        _row_gather_copy(src_hbm, row_of(j), buf_ref, slot, j, sem_ref).start()


def _wait_row_gather(src_hbm, n_rows, buf_ref, slot, sem_ref):
    pltpu.make_async_copy(src_hbm.at[pl.ds(0, n_rows)], buf_ref.at[slot], sem_ref.at[slot]).wait()


def _gathered_rows(idx_ref, ys_hbm, buf_ref, sem_ref, tm):
    i = pl.program_id(0)
    n = pl.num_programs(0)
    slot = i % 2

    @pl.when(i == 0)
    def _():
        _start_row_gather(ys_hbm, lambda j: idx_ref[j], tm, buf_ref, 0, sem_ref)

    nxt = jnp.minimum(i + 1, n - 1)
    _start_row_gather(ys_hbm, lambda j: idx_ref[nxt * tm + j], tm, buf_ref, 1 - slot, sem_ref)
    _wait_row_gather(ys_hbm, tm, buf_ref, slot, sem_ref)
    return buf_ref[slot]


def _drain_gathered_rows(ys_hbm, buf_ref, sem_ref, tm):
    i = pl.program_id(0)

    @pl.when(i == pl.num_programs(0) - 1)
    def _():
        _wait_row_gather(ys_hbm, tm, buf_ref, 1 - i % 2, sem_ref)


def _proj_gather_kernel(idx_ref, xm_ref, g2_ref, ys_hbm, a_ref, sh_ref, win_ref, qg_ref, wuq_ref, kvg_ref, wkv_ref,
                        cf_ref, su_ref, sd_ref, hy_ref, gm_ref, cv_ref, q_ref, k_ref, v_ref, xn_ref, buf_ref, sem_ref):
    y = _gathered_rows(idx_ref, ys_hbm, buf_ref, sem_ref, xm_ref.shape[0])
    xn_ref[...] = xm_ref[...] + g2_ref[...] * y
    _proj_kernel(xn_ref, a_ref, sh_ref, win_ref, qg_ref, wuq_ref, kvg_ref, wkv_ref, cf_ref, su_ref, sd_ref,
                 hy_ref, gm_ref, cv_ref, q_ref, k_ref, v_ref)
    _drain_gathered_rows(ys_hbm, buf_ref, sem_ref, xm_ref.shape[0])


def _proj_kernel(x_ref, a_ref, sh_ref, win_ref, qg_ref, wuq_ref, kvg_ref, wkv_ref, cf_ref, su_ref, sd_ref,
                 hy_ref, gm_ref, cv_ref, q_ref, k_ref, v_ref):
    x = x_ref[...]
    ms = jnp.mean(x * x, axis=-1, keepdims=True)
    h = x * lax.rsqrt(ms + EPS) * a_ref[...] + sh_ref[...]
    p = jnp.dot(h.astype(BF16), win_ref[...], preferred_element_type=F32)
    hy_ref[...] = p[:, HY_OFF:GM_OFF]
    gm_ref[...] = p[:, GM_OFF:CV_OFF]
    cv_ref[...] = p[:, CV_OFF:MQ_OFF]

    cf, su, sd = cf_ref[...], su_ref[...], sd_ref[...]
    cq = p[:, PQ_OFF:PQ_OFF + PQ_PAD]
    qn = cq * lax.rsqrt(jnp.sum(cq * cq, axis=-1, keepdims=True) * (1.0 / MLA_Q_RANK) + EPS) * qg_ref[...]
    q = jnp.dot(qn.astype(BF16), wuq_ref[...], preferred_element_type=F32)
    q = _rotate_slots(q, cf, su, sd) * (math.log2(math.e) / math.sqrt(MLA_NOPE + MLA_ROPE))
    q_ref[...] = q.astype(BF16)

    ckv = p[:, PKV_OFF:PKV_OFF + MLA_KV_RANK]
    kvn = ckv * lax.rsqrt(jnp.mean(ckv * ckv, axis=-1, keepdims=True) + EPS) * kvg_ref[...]
    kin = jnp.concatenate([kvn, p[:, PKR_OFF:PKR_OFF + LANES]], axis=1).astype(BF16)
    kv = jnp.dot(kin, wkv_ref[...], preferred_element_type=F32)
    k_ref[...] = _rotate_slots(kv[:, :QK_COLS], cf, su, sd).astype(BF16)
    v_ref[...] = kv[:, QK_COLS:].astype(BF16)


def _proj_call(xt, mod_a, mod_sh, mod_map, win, qg, wuq, kvg, wkv, rope_tabs, rope_map, tm, moe_in=None):
    T, D = xt.shape
    const = lambda i, *_: (0, 0)
    tok = lambda i, *_: (i, 0)
    mod = lambda i, *_: mod_map(i)
    rope = lambda i, *_: rope_map(i)
    out_cols = (HY_COLS, GM_COLS, CV_COLS, QK_COLS, QK_COLS, V_COLS)
    out_dtypes = (F32, F32, F32, BF16, BF16, BF16)
    in_specs = [
        pl.BlockSpec((None, 1, D), mod),
        pl.BlockSpec((None, 1, D), mod),
        pl.BlockSpec(win.shape, const),
        pl.BlockSpec(qg.shape, const),
        pl.BlockSpec(wuq.shape, const),
        pl.BlockSpec(kvg.shape, const),
        pl.BlockSpec(wkv.shape, const),
        pl.BlockSpec((tm, HEAD_SLOT), rope),
        pl.BlockSpec((tm, HEAD_SLOT), rope),
        pl.BlockSpec((tm, HEAD_SLOT), rope),
    ]
    out_specs = [pl.BlockSpec((tm, n), tok) for n in out_cols]
    out_shape = [jax.ShapeDtypeStruct((T, n), dt) for n, dt in zip(out_cols, out_dtypes)]
    shared = (mod_a, mod_sh, win, qg, wuq, kvg, wkv, *rope_tabs)
    if moe_in is None:
        return pl.pallas_call(
            _proj_kernel,
            grid=(T // tm,),
            in_specs=[pl.BlockSpec((tm, D), tok)] + in_specs,
            out_specs=out_specs,
            out_shape=out_shape,
            compiler_params=_cparams(1),
            name="proj_qkv",
        )(xt, *shared)
    tok_row, g2, ys = moe_in
    grid_spec = pltpu.PrefetchScalarGridSpec(
        num_scalar_prefetch=1,
        grid=(T // tm,),
        in_specs=[pl.BlockSpec((tm, D), tok), pl.BlockSpec((None, 1, D), mod),
                  pl.BlockSpec(memory_space=pl.ANY)] + in_specs,
        out_specs=out_specs + [pl.BlockSpec((tm, D), tok)],
        scratch_shapes=[pltpu.VMEM((2, tm, ys.shape[1]), ys.dtype), pltpu.SemaphoreType.DMA((2,))],
    )
    return pl.pallas_call(
        _proj_gather_kernel,
        grid_spec=grid_spec,
        out_shape=out_shape + [jax.ShapeDtypeStruct((T, D), F32)],
        compiler_params=_cparams(1),
        name="moe_residual_proj_qkv",
    )(tok_row, xt, g2, ys, *shared)


def _attend_heads(q_ref, key_refs, val_refs, o_ref):
    nt = (((1,), (1,)), ((), ()))
    lane = lax.broadcasted_iota(jnp.int32, (1, V_COLS), 1)
    vals = [v_ref[...] for v_ref in val_refs]
    sub = min(ATT_SUB, q_ref.shape[0])
    for r in range(q_ref.shape[0] // sub):
        rows = slice(r * sub, (r + 1) * sub)
        acc = jnp.zeros((sub, V_COLS), F32)
        for h in range(MLA_HEADS):
            sl_h = slice(h * HEAD_SLOT, (h + 1) * HEAD_SLOT)
            q = q_ref[rows, sl_h]
            scores = [lax.dot_general(q, k_ref[:, sl_h], nt, preferred_element_type=F32) for k_ref in key_refs]
            m = functools.reduce(jnp.maximum, [jnp.max(s, axis=-1, keepdims=True) for s in scores])
            probs = [jnp.exp2(s - m) for s in scores]
            denom = functools.reduce(jnp.add, [jnp.sum(p, axis=-1, keepdims=True) for p in probs])
            o = functools.reduce(jnp.add, [jnp.dot(p.astype(BF16), v, preferred_element_type=F32)
                                           for p, v in zip(probs, vals)])
            in_head = (lane >= h * MLA_V) & (lane < (h + 1) * MLA_V)
            acc = acc + jnp.where(in_head, o / denom, 0.0)
        o_ref[rows, :] = acc


def _attn_lat_kernel(q_ref, kl_ref, kc_ref, vl_ref, vc_ref, o_ref):
    _attend_heads(q_ref, (kl_ref, kc_ref), (vl_ref, vc_ref), o_ref)


def _attn_ctx_kernel(q_ref, k_ref, v_ref, o_ref):
    _attend_heads(q_ref, (k_ref,), (v_ref,), o_ref)


def _attn_lat_call(q, k, v, k_ctx, v_ctx, n_batch, n_lat, n_ctx):
    tq = ATT_TILE
    qt = n_lat // tq
    return pl.pallas_call(
        _attn_lat_kernel,
        grid=(n_batch, qt),
        in_specs=[
            pl.BlockSpec((tq, QK_COLS), lambda b, j: (b * qt + j, 0)),
            pl.BlockSpec((n_lat, QK_COLS), lambda b, j: (b, 0)),
            pl.BlockSpec((n_ctx, QK_COLS), lambda b, j: (b, 0)),
            pl.BlockSpec((n_lat, V_COLS), lambda b, j: (b, 0)),
            pl.BlockSpec((n_ctx, V_COLS), lambda b, j: (b, 0)),
        ],
        out_specs=pl.BlockSpec((tq, V_COLS), lambda b, j: (b * qt + j, 0)),
        out_shape=jax.ShapeDtypeStruct((n_batch * n_lat, V_COLS), F32),
        compiler_params=_cparams(2),
        name="attn_latent",
    )(q, k, k_ctx, v, v_ctx)


def _attn_ctx_call(q, k, v, n_batch, n_ctx):
    blk = lambda b: (b, 0)
    return pl.pallas_call(
        _attn_ctx_kernel,
        grid=(n_batch,),
        in_specs=[pl.BlockSpec((n_ctx, QK_COLS), blk), pl.BlockSpec((n_ctx, QK_COLS), blk),
                  pl.BlockSpec((n_ctx, V_COLS), blk)],
        out_specs=pl.BlockSpec((n_ctx, V_COLS), blk),
        out_shape=jax.ShapeDtypeStruct((n_batch * n_ctx, V_COLS), F32),
        compiler_params=_cparams(1),
        name="attn_context",
    )(q, k, v)


@functools.lru_cache(maxsize=None)
def _dft_factor_tables_np(L):
    f = np.arange(L, dtype=np.int64)[:, None]
    def trig(t):
        ang = ((f * t[None, :]) % (2 * L)).astype(np.float64) * (np.pi / L)
        return np.cos(ang).astype(np.float32), np.sin(ang).astype(np.float32)
    return trig(np.arange(L // LANES, dtype=np.int64) * LANES) + trig(np.arange(LANES, dtype=np.int64))


def _dft_matrices(L):
    assert L % LANES == 0
    c1, s1, c0, s0 = (jnp.asarray(a) for a in _dft_factor_tables_np(L))
    c1, s1, c0, s0 = c1[:, :, None], s1[:, :, None], c0[:, None, :], s0[:, None, :]
    c = (c1 * c0 - s1 * s0).reshape(L, L)
    s = (s1 * c0 + c1 * s0).reshape(L, L)
    alt = jnp.where(jnp.arange(L) % 2 == 0, 1.0, -1.0).astype(F32)
    row0 = (jnp.arange(L) == 0)
    s_fwd = jnp.where(row0[:, None], alt[None, :], s)
    s_inv = jnp.where(row0[None, :], alt[:, None], s)
    return c.astype(BF16), s_fwd.astype(BF16), s_inv.astype(BF16)


def _half_dft(L):
    assert L % 2 == 0
    return _dft_matrices(L // 2) + tuple(jnp.asarray(t) for t in _twiddle_np(L // 2))


def _half_butterfly(ae, ao, be, bo, c, s):
    return (ae + c * ao - s * bo, be + c * bo + s * ao,
            ae - c * ao + s * bo, -be + c * bo + s * ao)


def _twiddle_np(M):
    ang = np.arange(M, dtype=np.float64)[:, None] * (np.pi / (2 * M))
    return np.cos(ang).astype(np.float32), np.sin(ang).astype(np.float32)


def _spectrum_kernel(c_ref, s_ref, k_ref, tc_ref, ts_ref, krl_ref, kil_ref, krh_ref, kih_ref, sp_ref):
    tr, M = c_ref.shape
    w = k_ref.shape[1] // 4
    inv_l = 1.0 / (2 * M)
    k = k_ref[...]
    a = jnp.dot(c_ref[...], k, preferred_element_type=F32)
    b = jnp.dot(s_ref[...], k, preferred_element_type=F32)
    g = pl.program_id(0) * tr + lax.broadcasted_iota(jnp.int32, (tr, 1), 0)
    sign = jnp.where(g % 2 == 0, 1.0, -1.0)
    tc, ts = tc_ref[...], ts_ref[...]
    cols = lambda x, j: x[:, j * w:(j + 1) * w]
    f_lo_a, f_lo_b, f_hi_a, f_hi_b = _half_butterfly(cols(a, 0), cols(a, 2), cols(b, 0), cols(b, 2), tc, ts)
    b_lo_a, b_lo_b, b_hi_a, b_hi_b = _half_butterfly(cols(a, 1), cols(a, 3), cols(b, 1), cols(b, 3), tc, ts)
    kr_lo = (f_lo_a + sign * b_lo_a) * inv_l
    kr_hi = (f_hi_a + sign * b_hi_a) * inv_l
    krl_ref[...] = kr_lo
    kil_ref[...] = -(f_lo_b + sign * b_lo_b) * inv_l
    krh_ref[...] = kr_hi
    kih_ref[...] = -(f_hi_b + sign * b_hi_b) * inv_l

    @pl.when(pl.program_id(0) == 0)
    def _():
        mid_sign = 1.0 if M % 2 == 0 else -1.0
        sp_ref[...] = jnp.zeros_like(sp_ref)
        sp_ref[0:1, :] = 0.5 * kr_lo[0:1, :]
        sp_ref[1:2, :] = 0.5 * kr_hi[0:1, :]
        sp_ref[2:3, :] = (cols(b, 0)[0:1, :] + mid_sign * cols(b, 1)[0:1, :]) * inv_l
        sp_ref[3:4, :] = -(cols(b, 2)[0:1, :] + mid_sign * cols(b, 3)[0:1, :]) * inv_l


def _spectrum_call(dft, filters):
    cmat, smat, _, tc, ts = dft
    M = cmat.shape[0]
    w2 = filters.shape[1]
    w = w2 // 2
    samples = filters.reshape(M, 2 * w2).astype(BF16)
    tr = min(DFT_ROWS, M)
    row = lambda i: (i, 0)
    return pl.pallas_call(
        _spectrum_kernel,
        grid=(M // tr,),
        in_specs=[pl.BlockSpec((tr, M), row), pl.BlockSpec((tr, M), row),
                  pl.BlockSpec((M, 2 * w2), lambda i: (0, 0), pipeline_mode=pl.Buffered(1)),
                  pl.BlockSpec((tr, 1), row), pl.BlockSpec((tr, 1), row)],
        out_specs=[pl.BlockSpec((tr, w), row)] * 4 + [pl.BlockSpec((SUBLANES, w), lambda i: (0, 0))],
        out_shape=[jax.ShapeDtypeStruct((M, w), F32)] * 4 + [jax.ShapeDtypeStruct((SUBLANES, w), F32)],
        compiler_params=_cparams(1),
        name="hyena_filter_spectrum",
    )(cmat, smat, samples, tc, ts)


def _hy_prep_kernel(p_ref, w_ref, b_ref, z_ref, v16_ref):
    j = pl.program_id(1) // (D_GROUP // LANES)
    M = z_ref.shape[1]
    pe = p_ref[pl.ds(0, M, stride=2), :]
    po = p_ref[pl.ds(1, M, stride=2), :]
    row = lax.broadcasted_iota(jnp.int32, (M, 1), 0)
    po_prev = jnp.where(row == 0, 0.0, pltpu.roll(po, 1, 0))
    pe_next = jnp.where(row == M - 1, 0.0, pltpu.roll(pe, M - 1, 0))
    w0, w1, w2, bias = w_ref[0:1, :], w_ref[1:2, :], w_ref[2:3, :], b_ref[...]
    ze = po_prev * w0 + pe * w1 + po * w2 + bias
    zo = pe * w0 + po * w1 + pe_next * w2 + bias
    z_ref[0] = ze
    z_ref[1] = zo

    @pl.when(j == HY_ORDER)
    def _():
        v16_ref[0] = ze.astype(BF16)
        v16_ref[1] = zo.astype(BF16)


def _hy_prep_call(p_hy, short_w, short_b, n_batch, L):
    N = n_batch * D_GROUP
    M = L // 2
    hb = D_GROUP // LANES
    return pl.pallas_call(
        _hy_prep_kernel,
        grid=(n_batch, (HY_ORDER + 1) * hb),
        in_specs=[
            pl.BlockSpec((L, LANES), lambda b, j: (b, j)),
            pl.BlockSpec((HY_SHORT, LANES), lambda b, j: (0, j)),
            pl.BlockSpec((1, LANES), lambda b, j: (0, j)),
        ],
        out_specs=[pl.BlockSpec((None, 2, M, LANES), lambda b, j: (j // hb, 0, 0, b * hb + j % hb)),
                   pl.BlockSpec((2, M, LANES), lambda b, j: (0, 0, b * hb + jnp.where(j // hb == HY_ORDER, j % hb, 0)))],
        out_shape=[jax.ShapeDtypeStruct((HY_ORDER + 1, 2, M, N), F32), jax.ShapeDtypeStruct((2, M, N), BF16)],
        compiler_params=_cparams(2),
        name="hyena_short_conv",
    )(p_hy, short_w, short_b[None, :])


def _dft_fwd_kernel(c_ref, s_ref, ue_ref, uo_ref, tc_ref, ts_ref, krl_ref, kil_ref, krh_ref, kih_ref, sp_ref,
                    e_ref):
    tr = c_ref.shape[0]
    c, s = c_ref[...], s_ref[...]
    ue, uo = ue_ref[...], uo_ref[...]
    ae = jnp.dot(c, ue, preferred_element_type=F32)
    ao = jnp.dot(c, uo, preferred_element_type=F32)
    be = jnp.dot(s, ue, preferred_element_type=F32)
    bo = jnp.dot(s, uo, preferred_element_type=F32)
    tc, ts = tc_ref[...], ts_ref[...]
    first = (pl.program_id(1) * tr + lax.broadcasted_iota(jnp.int32, (tr, 1), 0)) == 0
    krl, kil, krh, kih = krl_ref[...], kil_ref[...], krh_ref[...], kih_ref[...]
    k0h, kLh, krm, kim = sp_ref[0:1, :], sp_ref[1:2, :], sp_ref[2:3, :], sp_ref[3:4, :]
    for g in range(ue.shape[1] // D_GROUP):
        sl = slice(g * D_GROUP, (g + 1) * D_GROUP)
        a1, b1, a2, b2 = _half_butterfly(ae[:, sl], ao[:, sl], be[:, sl], bo[:, sl], tc, ts)
        p1, q1 = krl * a1 + kil * b1, krl * b1 - kil * a1
        p2, q2 = krh * a2 + kih * b2, krh * b2 - kih * a2
        pm, qp = p1 - p2, q1 + q2
        dc, ny = k0h * a1, kLh * a2
        alt_e, alt_o = be[:, sl], bo[:, sl]
        e_ref[0, :, sl] = jnp.where(first, dc + ny, p1 + p2).astype(BF16)
        e_ref[1, :, sl] = jnp.where(first, krm * alt_e + kim * alt_o, q1 - q2).astype(BF16)
        e_ref[2, :, sl] = jnp.where(first, dc - ny, tc * pm + ts * qp).astype(BF16)
        e_ref[3, :, sl] = jnp.where(first, krm * alt_o - kim * alt_e, tc * qp - ts * pm).astype(BF16)


def _dft_fwd_call(dft, u16, tabs, order):
    cmat, smat, _, tc, ts = dft
    _, M, N = u16.shape
    tr = min(DFT_ROWS, M)
    tn = min(DFT_COLS, N)
    row = lambda h, i: (i, 0)
    tab = lambda h, i: (i, order)
    return pl.pallas_call(
        _dft_fwd_kernel,
        grid=(N // tn, M // tr),
        in_specs=[
            pl.BlockSpec((tr, M), row),
            pl.BlockSpec((tr, M), row),
            pl.BlockSpec((None, M, tn), lambda h, i: (0, 0, h), pipeline_mode=pl.Buffered(1)),
            pl.BlockSpec((None, M, tn), lambda h, i: (1, 0, h), pipeline_mode=pl.Buffered(1)),
            pl.BlockSpec((tr, 1), row), pl.BlockSpec((tr, 1), row),
        ] + [pl.BlockSpec((tr, D_GROUP), tab)] * 4 + [pl.BlockSpec((SUBLANES, D_GROUP), lambda h, i: (0, order))],
        out_specs=pl.BlockSpec((4, tr, tn), lambda h, i: (0, i, h)),
        out_shape=jax.ShapeDtypeStruct((4, M, N), BF16),
        compiler_params=_cparams(2),
        name="hyena_dft_fwd",
    )(cmat, smat, u16, u16, tc, ts, *tabs)


def _dft_inv_kernel(c_ref, st_ref, ec_ref, es_ref, oc_ref, os_ref, u_ref, g_ref, bias_ref, y_ref, y16_ref):
    c, st = c_ref[...], st_ref[...]
    bias = bias_ref[...]
    conv_e = (jnp.dot(c, ec_ref[...], preferred_element_type=F32)
              + jnp.dot(st, es_ref[...], preferred_element_type=F32))
    conv_o = (jnp.dot(c, oc_ref[...], preferred_element_type=F32)
              + jnp.dot(st, os_ref[...], preferred_element_type=F32))
    for par, conv in enumerate((conv_e, conv_o)):
        y = g_ref[par] * (conv + u_ref[par] * bias)
        y_ref[par] = y
        y16_ref[par] = y.astype(BF16)


def _dft_inv_call(dft, e16, u_arr, u_sel, g_arr, g_sel, bias_row):
    cmat, _, stmat, _, _ = dft
    _, M, N = e16.shape
    tr = min(DFT_ROWS, M)
    tn = min(DFT_COLS, N)
    row = lambda h, i: (i, 0)
    plane = lambda k: pl.BlockSpec((None, M, tn), lambda h, i: (k, 0, h), pipeline_mode=pl.Buffered(1))
    return pl.pallas_call(
        _dft_inv_kernel,
        grid=(N // tn, M // tr),
        in_specs=[
            pl.BlockSpec((tr, M), row),
            pl.BlockSpec((tr, M), row),
            plane(0), plane(1), plane(2), plane(3),
            pl.BlockSpec((None, 2, tr, tn), lambda h, i: (u_sel, 0, i, h)),
            pl.BlockSpec((None, 2, tr, tn), lambda h, i: (g_sel, 0, i, h)),
            pl.BlockSpec((1, tn), lambda h, i: (0, h)),
        ],
        out_specs=[pl.BlockSpec((2, tr, tn), lambda h, i: (0, i, h))] * 2,
        out_shape=[jax.ShapeDtypeStruct((2, M, N), F32), jax.ShapeDtypeStruct((2, M, N), BF16)],
        compiler_params=_cparams(2),
        name="hyena_dft_inv",
    )(cmat, stmat, e16, e16, e16, e16, u_arr, g_arr, bias_row)


def _hyena_call(p_hy, short_w, short_b, filters, hy_bias, n_batch, L, dft):
    tabs = _spectrum_call(dft, filters)
    z, u16 = _hy_prep_call(p_hy, short_w, short_b, n_batch, L)
    y_stack = z
    u_sel = HY_ORDER
    for n in range(HY_ORDER):
        e16 = _dft_fwd_call(dft, u16, tabs, n)
        bias_row = jnp.tile(hy_bias[n][None, :].astype(F32), (1, n_batch))
        y, u16 = _dft_inv_call(dft, e16, y_stack, u_sel, z, n, bias_row)
        y_stack, u_sel = y[None], 0
    return y


def _mixers_kernel(gm_ref, cv_ref, cvp_ref, cvn_ref, lng_ref, lnb_ref, ws_ref, bsf_ref, dww_ref, dwb_ref,
                   cg_ref, cb_ref, avg_ref, ygm_ref, ycv_ref, glu_ref, *, tiles_per_seq):
    i = pl.program_id(0)
    tm = gm_ref.shape[0]
    lane = lax.broadcasted_iota(jnp.int32, (1, D_GROUP), 1)

    z = jax.nn.gelu(gm_ref[...], approximate=True)
    u, v = z[:, :D_GROUP], z[:, D_GROUP:]
    mu = jnp.mean(v, axis=-1, keepdims=True)
    vc = v - mu
    var = jnp.mean(vc * vc, axis=-1, keepdims=True)
    vn = (vc * lax.rsqrt(var + EPS) * lng_ref[...] + lnb_ref[...]).astype(BF16)
    hd = D_GROUP // GM_HEADS
    for c in range(tm // GM_CHUNK):
        rows = slice(c * GM_CHUNK, (c + 1) * GM_CHUNK)
        s = bsf_ref[...]
        for g in range(GM_HEADS):
            sg = jnp.dot(ws_ref[g], vn[rows, :], preferred_element_type=F32)
            s = s + jnp.where((lane >= g * hd) & (lane < (g + 1) * hd), sg, 0.0)
        ygm_ref[rows, :] = u[rows, :] * s

    def glu(t):
        return t[:, :D_GROUP] * jax.nn.sigmoid(t[:, D_GROUP:])

    first = (i % tiles_per_seq) == 0
    last = (i % tiles_per_seq) == tiles_per_seq - 1
    span = tm + 2 * CONV_HALO
    glu_ref[0, 0:CONV_HALO, :] = jnp.where(first, 0.0, glu(cvp_ref[...]))
    glu_ref[0, CONV_HALO:CONV_HALO + tm, :] = glu(cv_ref[...])
    glu_ref[0, CONV_HALO + tm:span, :] = jnp.where(last, 0.0, glu(cvn_ref[...]))
    glu_ref[0, span:, :] = jnp.zeros((SUBLANES, D_GROUP), F32)
    for b in range(1, SUBLANES):
        glu_ref[b, 0:span, :] = glu_ref[0, b:b + span, :]
    pad = (CV_WIDTH - 1) // 2
    rc = 128

    def group_mean(t):
        hi = t.astype(BF16)
        lo = (t - hi.astype(F32)).astype(BF16)
        return (jnp.dot(hi, avg_ref[...], preferred_element_type=F32)
                + jnp.dot(lo, avg_ref[...], preferred_element_type=F32))

    for c in range(tm // rc):
        acc = jnp.zeros((rc, D_GROUP), F32) + dwb_ref[...]
        for k in range(CV_WIDTH):
            start = c * rc + CONV_HALO - pad + k
            b = start % SUBLANES
            acc = acc + glu_ref[b, start - b:start - b + rc, :] * dww_ref[k:k + 1, :]
        d = acc - group_mean(acc)
        gvar = group_mean(d * d)
        n = d * lax.rsqrt(gvar + EPS) * cg_ref[...] + cb_ref[...]
        ycv_ref[c * rc:(c + 1) * rc, :] = n * jax.nn.sigmoid(n)


def _mixers_call(p_gm, p_cv, gm_ln_g, gm_ln_b, gm_ws, gm_bs, cv_dw_w, cv_dw_b, cv_ln_g, cv_ln_b, L, tm):
    T = p_gm.shape[0]
    tps = L // tm
    hb = tm // CONV_HALO
    n_hblk = T // CONV_HALO
    const2 = lambda i: (0, 0)
    tok = lambda i: (i, 0)
    bs_full = jnp.repeat(gm_bs.T.astype(F32), D_GROUP // GM_HEADS, axis=1)
    gid = np.arange(D_GROUP) // (D_GROUP // CV_GROUPS)
    avg = jnp.asarray((gid[:, None] == gid[None, :]).astype(np.float32) / (D_GROUP // CV_GROUPS)).astype(BF16)
    row = lambda a: a[None, :].astype(F32)
    return pl.pallas_call(
        functools.partial(_mixers_kernel, tiles_per_seq=tps),
        grid=(T // tm,),
        in_specs=[
            pl.BlockSpec((tm, GM_COLS), tok),
            pl.BlockSpec((tm, CV_COLS), tok),
            pl.BlockSpec((CONV_HALO, CV_COLS), lambda i: (jnp.maximum(i * hb - 1, 0), 0)),
            pl.BlockSpec((CONV_HALO, CV_COLS), lambda i: (jnp.minimum((i + 1) * hb, n_hblk - 1), 0)),
            pl.BlockSpec((1, D_GROUP), const2),
            pl.BlockSpec((1, D_GROUP), const2),
            pl.BlockSpec((GM_HEADS, GM_CHUNK, GM_CHUNK), lambda i: (0, 0, 0)),
            pl.BlockSpec((GM_CHUNK, D_GROUP), const2),
            pl.BlockSpec((CV_WIDTH, D_GROUP), const2),
            pl.BlockSpec((1, D_GROUP), const2),
            pl.BlockSpec((1, D_GROUP), const2),
            pl.BlockSpec((1, D_GROUP), const2),
            pl.BlockSpec((D_GROUP, D_GROUP), const2),
        ],
        out_specs=[pl.BlockSpec((tm, D_GROUP), tok)] * 2,
        out_shape=[jax.ShapeDtypeStruct((T, D_GROUP), F32)] * 2,
        scratch_shapes=[pltpu.VMEM((SUBLANES, tm + 2 * CONV_HALO + SUBLANES, D_GROUP), F32)],
        compiler_params=_cparams(1),
        name="gmlp_conv_mixers",
    )(p_gm, p_cv, p_cv, p_cv, row(gm_ln_g), row(gm_ln_b), gm_ws.astype(BF16), bs_full, cv_dw_w.astype(F32),
      row(cv_dw_b), row(cv_ln_g), row(cv_ln_b), avg)


def _out_kernel(hye_ref, hyo_ref, gm_ref, cv_ref, at_ref, x_ref, mg_ref, wo_ref, g1_ref, a2_ref, sh2_ref, wrh_ref,
                wrl_ref, rb_ref, tri_ref, cin_ref, xm_ref, hfx_ref, cls_ref, rank_ref, cnt_ref, carry_ref, hy_ref):
    i = pl.program_id(0)
    tm = x_ref.shape[0]
    for k in range(D_GROUP // LANES):
        hy_ref[k, pl.ds(0, tm // 2, stride=2), :] = hye_ref[:, k * LANES:(k + 1) * LANES]
        hy_ref[k, pl.ds(1, tm // 2, stride=2), :] = hyo_ref[:, k * LANES:(k + 1) * LANES]
    y_hy = jnp.concatenate([hy_ref[k] for k in range(D_GROUP // LANES)], axis=1)
    o = None
    for g, y_src in enumerate((y_hy, gm_ref, cv_ref, at_ref)):
        y = y_src[...]
        n = y * lax.rsqrt(jnp.mean(y * y, axis=-1, keepdims=True) + EPS) * mg_ref[:, g * D_GROUP:(g + 1) * D_GROUP]
        part = jnp.dot(n.astype(BF16), wo_ref[g * D_GROUP:(g + 1) * D_GROUP, :], preferred_element_type=F32)
        o = part if o is None else o + part
    xm = x_ref[...] + g1_ref[...] * o
    xm_ref[...] = xm
    hf = xm * lax.rsqrt(jnp.mean(xm * xm, axis=-1, keepdims=True) + EPS) * a2_ref[...] + sh2_ref[...]

    hf_hi = hf.astype(BF16)
    hf_lo = (hf - hf_hi.astype(F32)).astype(BF16)
    logits = (jnp.dot(hf_hi, wrh_ref[...], preferred_element_type=F32)
              + jnp.dot(hf_lo, wrh_ref[...], preferred_element_type=F32)
              + jnp.dot(hf_hi, wrl_ref[...], preferred_element_type=F32))
    cls, g_lo, g_hi = _route_top2(jnp.transpose(logits), rb_ref[...])
    cls_ref[...] = cls

    d_model = hf.shape[1]
    hfx_ref[:, :d_model] = hf
    gate_rows = jnp.concatenate([g_lo, g_hi, jnp.zeros((LANES - TOP_K, tm), F32)], axis=0)
    hfx_ref[:, d_model:] = jnp.transpose(gate_rows)

    @pl.when(i == 0)
    def _():
        carry_ref[...] = cin_ref[...]

    sub = lax.broadcasted_iota(jnp.int32, (CLASS_ROWS, tm), 0)
    onehot = sub == cls
    prefix = jnp.dot(onehot.astype(BF16), tri_ref[...], preferred_element_type=F32)
    carry = carry_ref[...]
    rank = jnp.sum(jnp.where(onehot, prefix + carry[:, 0:1], 0.0), axis=0, keepdims=True)
    rank_ref[...] = rank.astype(jnp.int32)
    carry = carry + jnp.sum(onehot.astype(F32), axis=1, keepdims=True)
    carry_ref[...] = carry
    cnt_ref[...] = carry


def _out_alias_kernel(*refs):
    _out_kernel(*refs[1:])


def _out_call(y_hy_t, y_gm, y_cv, y_at, xt, mixg, wo, g1, a2, sh2, mod_map, wr_hi, wr_lo, rbias, counts_in, L, tm,
              hfx_prev, t_total, row0):
    T, D = xt.shape
    tps = L // tm
    blk0 = row0 // tm
    const = lambda i: (0, 0)
    tok = lambda i: (i, 0)
    lane_tok = lambda i: (0, i)
    tri = jnp.asarray(np.triu(np.ones((tm, tm), np.float32), k=1)).astype(BF16)
    W = D + LANES
    parity = lambda par: pl.BlockSpec((None, tm // 2, D_GROUP), lambda i: (par, i % tps, i // tps))
    in_specs = [parity(0), parity(1)] + [pl.BlockSpec((tm, D_GROUP), tok)] * 3 + [
        pl.BlockSpec((tm, D), tok),
        pl.BlockSpec(mixg.shape, const),
        pl.BlockSpec(wo.shape, const),
        pl.BlockSpec((None, 1, D), mod_map),
        pl.BlockSpec((None, 1, D), mod_map),
        pl.BlockSpec((None, 1, D), mod_map),
        pl.BlockSpec(wr_hi.shape, const),
        pl.BlockSpec(wr_lo.shape, const),
        pl.BlockSpec(rbias.shape, const),
        pl.BlockSpec((tm, tm), const),
        pl.BlockSpec((CLASS_ROWS, LANES), const),
    ]
    args = (y_hy_t, y_hy_t, y_gm, y_cv, y_at, xt, mixg, wo, g1, a2, sh2, wr_hi, wr_lo, rbias, tri, counts_in)
    aliased = hfx_prev is not None
    return pl.pallas_call(
        _out_alias_kernel if aliased else _out_kernel,
        grid=(T // tm,),
        in_specs=([pl.BlockSpec(memory_space=pl.ANY)] if aliased else []) + in_specs,
        out_specs=[pl.BlockSpec((tm, D), tok), pl.BlockSpec((tm, W), lambda i: (blk0 + i, 0)),
                   pl.BlockSpec((1, tm), lane_tok), pl.BlockSpec((1, tm), lane_tok),
                   pl.BlockSpec((CLASS_ROWS, LANES), const)],
        out_shape=[jax.ShapeDtypeStruct((T, D), F32), jax.ShapeDtypeStruct((t_total, W), F32),
                   jax.ShapeDtypeStruct((1, T), jnp.int32), jax.ShapeDtypeStruct((1, T), jnp.int32),
                   jax.ShapeDtypeStruct((CLASS_ROWS, LANES), F32)],
        scratch_shapes=[pltpu.VMEM((CLASS_ROWS, LANES), F32), pltpu.VMEM((D_GROUP // LANES, tm, LANES), F32)],
        input_output_aliases={0: 1} if aliased else {},
        compiler_params=_cparams(1),
        name="mix_out_norm2_route",
    )(*(((hfx_prev,) if aliased else ()) + args))


def _first_max_flags(vals):
    m = functools.reduce(jnp.maximum, vals)
    flags, taken = [], None
    for v in vals:
        f = v >= m
        if taken is not None:
            f = f & jnp.logical_not(taken)
        flags.append(f)
        taken = f if taken is None else taken | f
    return flags, m


def _pick(flags, vals):
    out = vals[-1]
    for f, v in zip(flags[-2::-1], vals[-2::-1]):
        out = jnp.where(f, v, out)
    return out


def _route_top2(lt, bias_col):
    s_all = jax.nn.sigmoid(lt[:N_EXPERTS, :])
    sel_all = s_all + bias_col
    s = [s_all[e:e + 1, :] for e in range(N_EXPERTS)]
    sel = [sel_all[e:e + 1, :] for e in range(N_EXPERTS)]
    neg = -jnp.inf
    E = EXPERTS_PER_GROUP

    def top2(vals):
        f1, m1 = _first_max_flags(vals)
        rest = [jnp.where(f, neg, v) for f, v in zip(f1, vals)]
        f2, m2 = _first_max_flags(rest)
        return f1, m1, f2, m2

    scores = []
    for g in range(N_EXPERT_GROUPS):
        _, m1, _, m2 = top2(sel[g * E:(g + 1) * E])
        scores.append(m1 + m2)
    gflags, _ = _first_max_flags(scores)
    bsel = [_pick(gflags, [sel[g * E + j] for g in range(N_EXPERT_GROUPS)]) for j in range(E)]
    bs = [_pick(gflags, [s[g * E + j] for g in range(N_EXPERT_GROUPS)]) for j in range(E)]
    f1, _, f2, _ = top2(bsel)
    zero = jnp.zeros_like(bs[0])
    w1 = functools.reduce(jnp.add, [jnp.where(f, v, zero) for f, v in zip(f1, bs)])
    w2 = functools.reduce(jnp.add, [jnp.where(f, v, zero) for f, v in zip(f2, bs)])
    izero = jnp.zeros(w1.shape, jnp.int32)
    j1 = functools.reduce(jnp.add, [jnp.where(f, j, izero) for j, f in enumerate(f1)])
    j2 = functools.reduce(jnp.add, [jnp.where(f, j, izero) for j, f in enumerate(f2)])
    gi = functools.reduce(jnp.add, [jnp.where(f, g, izero) for g, f in enumerate(gflags)])
    lo, hi = jnp.minimum(j1, j2), jnp.maximum(j1, j2)
    pair = jnp.where(lo == 0, 0, jnp.where(lo == 1, 3, 5)) + hi - lo - 1
    tot = w1 + w2
    first_is_lo = j1 < j2
    return gi * N_PAIRS + pair, jnp.where(first_is_lo, w1, w2) / tot, jnp.where(first_is_lo, w2, w1) / tot


def _table_lookup(table, idx):
    n = table.shape[0]
    hit = idx[..., None] == jnp.arange(n, dtype=jnp.int32)
    return jnp.sum(jnp.where(hit, table, 0), axis=-1)


def _class_plan(cls, rank, counts, tile):
    T = cls.shape[0]
    n_tiles = (T + N_CLASSES * tile) // tile
    order = jnp.argsort(cls, stable=True).astype(jnp.int32)
    padded = ((counts + tile - 1) // tile) * tile
    seg_end = jnp.cumsum(padded)
    seg_start = seg_end - padded
    src_start = jnp.cumsum(counts) - counts
    tok_row = _table_lookup(seg_start, cls) + rank

    tile_first = jnp.arange(n_tiles, dtype=jnp.int32) * tile
    tile_class = jnp.minimum(jnp.sum((seg_end[None, :] <= tile_first[:, None]).astype(jnp.int32), axis=1),
                             N_CLASSES - 1)
    off = tile_first - _table_lookup(seg_start, tile_class)
    tile_cnt = jnp.clip(_table_lookup(counts, tile_class) - off, 0, tile)
    tile_src = jnp.clip(_table_lookup(src_start, tile_class) + off, 0, T - 1)
    grp = tile_class // N_PAIRS
    pair = tile_class % N_PAIRS
    tile_lo = grp * EXPERTS_PER_GROUP + _table_lookup(jnp.asarray(PAIR_LO, jnp.int32), pair)
    tile_hi = grp * EXPERTS_PER_GROUP + _table_lookup(jnp.asarray(PAIR_HI, jnp.int32), pair)
    return order, tile_lo, tile_hi, tile_src, tile_cnt, tok_row


def _moe_pair_kernel(order_ref, lo_ref, hi_ref, src_ref, cnt_ref, hfx_hbm, wga_ref, wua_ref, wda_ref, wgb_ref,
                     wub_ref, wdb_ref, o_ref, buf_ref, sem_ref):
    i = pl.program_id(0)
    n = pl.num_programs(0)
    tile = o_ref.shape[0]
    n_tok = order_ref.shape[0]
    slot = i % 2

    def start(t, s):
        base = src_ref[t]
        _start_row_gather(hfx_hbm, lambda j: order_ref[jnp.minimum(base + j, n_tok - 1)], tile, buf_ref, s, sem_ref)

    @pl.when((i == 0) & (cnt_ref[0] > 0))
    def _():
        start(0, 0)

    @pl.when((cnt_ref[i] <= 0) & (i > 0) & (cnt_ref[jnp.maximum(i - 1, 0)] > 0))
    def _():
        _wait_row_gather(hfx_hbm, tile, buf_ref, slot, sem_ref)

    @pl.when(cnt_ref[i] > 0)
    def _():
        start(jnp.minimum(i + 1, n - 1), 1 - slot)
        _wait_row_gather(hfx_hbm, tile, buf_ref, slot, sem_ref)
        d_model = buf_ref.shape[2] - LANES
        x = buf_ref[slot, :, :d_model].astype(BF16)
        live = lax.broadcasted_iota(jnp.int32, (tile, 1), 0) < cnt_ref[i]
        gates = buf_ref[slot, :, d_model:]
        g_lo = jnp.where(live, gates[:, 0:1], 0.0)
        g_hi = jnp.where(live, gates[:, 1:2], 0.0)

        def hidden(wg_ref, wu_ref, gate):
            hg = jnp.dot(x, wg_ref[...], preferred_element_type=F32)
            hu = jnp.dot(x, wu_ref[...], preferred_element_type=F32)
            return (hg * jax.nn.sigmoid(hg) * hu * gate).astype(BF16)

        o_ref[...] = (jnp.dot(hidden(wga_ref, wua_ref, g_lo), wda_ref[...], preferred_element_type=F32)
                      + jnp.dot(hidden(wgb_ref, wub_ref, g_hi), wdb_ref[...], preferred_element_type=F32))

    @pl.when(cnt_ref[i] <= 0)
    def _():
        o_ref[...] = jnp.zeros_like(o_ref)


def _moe_pair_call(hfx, order, tile_lo, tile_hi, tile_src, tile_cnt, wg, wu, wd, tile):
    T, W = hfx.shape
    D = W - LANES
    F = wg.shape[-1]
    n_tiles = tile_lo.shape[0]
    lo = lambda i, o, tl, th, ts, tc: (tl[i], 0, 0)
    hi = lambda i, o, tl, th, ts, tc: (th[i], 0, 0)
    grid_spec = pltpu.PrefetchScalarGridSpec(
        num_scalar_prefetch=5,
        grid=(n_tiles,),
        in_specs=[
            pl.BlockSpec(memory_space=pl.ANY),
            pl.BlockSpec((None, D, F), lo), pl.BlockSpec((None, D, F), lo), pl.BlockSpec((None, F, D), lo),
            pl.BlockSpec((None, D, F), hi), pl.BlockSpec((None, D, F), hi), pl.BlockSpec((None, F, D), hi),
        ],
        out_specs=pl.BlockSpec((tile, D), lambda i, o, tl, th, ts, tc: (i, 0)),
        scratch_shapes=[pltpu.VMEM((2, tile, W), F32), pltpu.SemaphoreType.DMA((2,))],
    )
    return pl.pallas_call(
        _moe_pair_kernel,
        grid_spec=grid_spec,
        out_shape=jax.ShapeDtypeStruct((n_tiles * tile, D), F32),
        compiler_params=_cparams(1),
        name="moe_pair_grouped",
    )(order, tile_lo, tile_hi, tile_src, tile_cnt, hfx, wg, wu, wd, wg, wu, wd)


def _final_kernel(idx_ref, xm_ref, g2_ref, ys_hbm, g_ref, o_ref, buf_ref, sem_ref):
    x = xm_ref[...] + g2_ref[...] * _gathered_rows(idx_ref, ys_hbm, buf_ref, sem_ref, xm_ref.shape[0])
    o_ref[...] = x * lax.rsqrt(jnp.mean(x * x, axis=-1, keepdims=True) + EPS) * g_ref[...]
    _drain_gathered_rows(ys_hbm, buf_ref, sem_ref, xm_ref.shape[0])


def _final_call(x_mid, tok_row, g2, mod_map, ys, g, tm):
    T, D = x_mid.shape
    tok = lambda i, *_: (i, 0)
    grid_spec = pltpu.PrefetchScalarGridSpec(
        num_scalar_prefetch=1,
        grid=(T // tm,),
        in_specs=[pl.BlockSpec((tm, D), tok), pl.BlockSpec((None, 1, D), lambda i, *_: mod_map(i)),
                  pl.BlockSpec(memory_space=pl.ANY), pl.BlockSpec((1, D), lambda i, *_: (0, 0))],
        out_specs=pl.BlockSpec((tm, D), tok),
        scratch_shapes=[pltpu.VMEM((2, tm, ys.shape[1]), ys.dtype), pltpu.SemaphoreType.DMA((2,))],
    )
    return pl.pallas_call(
        _final_kernel,
        grid_spec=grid_spec,
        out_shape=jax.ShapeDtypeStruct((T, D), F32),
        compiler_params=_cparams(1),
        name="moe_residual_final_norm",
    )(tok_row, x_mid, g2, ys, g[None, :].astype(F32))


def kernel(x, c, ctx, c_ctx, ada_w, ada_b, norm1_g, norm2_g, w_in, hy_short_w, hy_short_b, hy_f_w1, hy_f_b1, hy_f_freq, hy_f_w2, hy_f_b2, hy_f_w3, hy_bias, gm_ln_g, gm_ln_b, gm_ws, gm_bs, cv_dw_w, cv_dw_b, cv_ln_g, cv_ln_b, mla_qa_norm, w_uq, mla_kva_norm, w_ukv, mix_norm_g, w_out, w_router, router_bias, exp_w_gate, exp_w_up, exp_w_down, final_norm_g):
    B, n_lat, D = x.shape
    n_ctx = ctx.shape[1]
    T_lat, T_ctx = B * n_lat, B * n_ctx
    tm_lat = min(512, n_lat)
    tm_ctx = min(256, n_ctx)
    assert n_lat % tm_lat == 0 and n_ctx % tm_ctx == 0 and n_lat % ATT_TILE == 0
    assert tm_lat % GM_CHUNK == 0 and tm_ctx % GM_CHUNK == 0 and T_lat % MOE_TILE == 0 and T_ctx % MOE_TILE == 0

    tps_lat, tps_ctx = n_lat // tm_lat, n_ctx // tm_ctx
    dft = {L: _half_dft(L) for L in {n_lat, n_ctx}}
    rope_lat = _rope_slot_tables(n_lat)
    rope_ctx = _identity_slot_tables(tm_ctx)
    wr_pad = jnp.pad(w_router.astype(F32), ((0, 0), (0, LANES - N_EXPERTS)))
    wr_hi = wr_pad.astype(BF16)
    wr_lo = (wr_pad - wr_hi.astype(F32)).astype(BF16)
    rbias = router_bias.astype(F32)[:, None]
    cond = jnp.concatenate([c, c_ctx[None, :]], axis=0)
    streams = {
        'lat': dict(x=x.reshape(T_lat, D), L=n_lat, tm=tm_lat, mod_map=lambda i: (i // tps_lat, 0, 0),
                    rope=rope_lat, rope_map=lambda i: (i % tps_lat, 0), row0=0, moe_in=None),
        'ctx': dict(x=ctx.reshape(T_ctx, D), L=n_ctx, tm=tm_ctx, mod_map=lambda i: (B, 0, 0),
                    rope=rope_ctx, rope_map=lambda i: (0, 0), row0=T_lat, moe_in=None),
    }

    for l in range(DEPTH):
        last = l == DEPTH - 1
        m = jax.nn.silu(cond) @ ada_w[l] + ada_b[l]
        sh1, sc1, g1, sh2, sc2, g2 = [t[:, None, :] for t in jnp.split(m, 6, axis=-1)]
        a1 = norm1_g[l][None, None, :] * (1.0 + sc1)
        a2 = norm2_g[l][None, None, :] * (1.0 + sc2)
        qg = jnp.pad(mla_qa_norm[l], (0, PQ_PAD - MLA_Q_RANK))[None, :].astype(F32)
        kvg = mla_kva_norm[l][None, :].astype(F32)
        win, wuq, wkv = _pack_w_in(w_in[l]), _pack_w_uq(w_uq[l]), _pack_w_ukv(w_ukv[l])
        filt = (hy_f_w1[l], hy_f_b1[l], hy_f_freq[l], hy_f_w2[l], hy_f_b2[l], hy_f_w3[l])
        wg, wu, wd = exp_w_gate[l].astype(BF16), exp_w_up[l].astype(BF16), exp_w_down[l].astype(BF16)

        proj = {name: _proj_call(s['x'], a1, sh1, s['mod_map'], win, qg, wuq, kvg, wkv, s['rope'], s['rope_map'],
                                 s['tm'], s['moe_in']) for name, s in streams.items()}
        for name, s in streams.items():
            if s['moe_in'] is not None:
                s['x'] = proj[name][6]
        k_ctx, v_ctx = proj['ctx'][4], proj['ctx'][5]

        active = ('lat',) if last else ('lat', 'ctx')
        x_mid, cls, rank = {}, {}, {}
        t_moe = T_lat if last else T_lat + T_ctx
        hfx = None if last else jnp.zeros((t_moe, D + LANES), F32)
        counts = jnp.zeros((CLASS_ROWS, LANES), F32)
        for name in active:
            s = streams[name]
            p_hy, p_gm, p_cv, q, k, v = proj[name][:6]
            if name == 'lat':
                y_at = _attn_lat_call(q, k, v, k_ctx, v_ctx, B, n_lat, n_ctx)
            else:
                y_at = _attn_ctx_call(q, k, v, B, n_ctx)
            filters = _hyena_filters(s['L'], *filt)
            y_hy_t = _hyena_call(p_hy, hy_short_w[l], hy_short_b[l], filters, hy_bias[l], B, s['L'], dft[s['L']])
            y_gm, y_cv = _mixers_call(p_gm, p_cv, gm_ln_g[l], gm_ln_b[l], gm_ws[l], gm_bs[l], cv_dw_w[l],
                                      cv_dw_b[l], cv_ln_g[l], cv_ln_b[l], s['L'], s['tm'])
            x_mid[name], hfx, cls[name], rank[name], counts = _out_call(
                y_hy_t, y_gm, y_cv, y_at, s['x'], mix_norm_g[l][None, :].astype(F32), w_out[l].astype(BF16),
                g1, a2, sh2, s['mod_map'], wr_hi, wr_lo, rbias, counts, s['L'], s['tm'],
                hfx, t_moe, s['row0'])

        cat = lambda d: jnp.concatenate([d[name][0] for name in active], axis=0)
        order, tile_lo, tile_hi, tile_src, tile_cnt, tok_row = _class_plan(
            cat(cls), cat(rank), counts[:N_CLASSES, 0].astype(jnp.int32), MOE_TILE)
        ys = _moe_pair_call(hfx, order, tile_lo, tile_hi, tile_src, tile_cnt, wg, wu, wd, MOE_TILE)
        for name in active:
            s = streams[name]
            s['x'] = x_mid[name]
            s['moe_in'] = (lax.slice_in_dim(tok_row, s['row0'], s['row0'] + x_mid[name].shape[0]), g2, ys)

    s = streams['lat']
    tok_row, g2, ys = s['moe_in']
    return _final_call(s['x'], tok_row, g2, s['mod_map'], ys, final_norm_g, s['tm']).reshape(B, n_lat, D)
```

```python
import functools
import math

import jax
import jax.numpy as jnp
import numpy as np
from jax import lax
from jax.experimental import pallas as pl
from jax.experimental.pallas import tpu as pltpu

F32 = jnp.float32
BF16 = jnp.bfloat16

D_MODEL = 1024
DEPTH = 2
GRID_W = 64
EPS = 1e-6

D_GROUP = 256
N_MIXERS = 4
HY_ORDER = 2
HY_SHORT = 3
HY_EMB = 33
HY_BANDS = (HY_EMB - 1) // 2
HY_TARGET = 1e-2
HY_FAST_PCT = 0.3
HY_SLOW_PCT = 1.5
GM_CHUNK = 128
GM_HEADS = 4
CV_WIDTH = 31
CV_GROUPS = 4
MLA_HEADS = 4
MLA_NOPE = 64
MLA_ROPE = 32
MLA_V = 64
MLA_Q_RANK = 192
MLA_KV_RANK = 128
ROPE_BASE = 10000.0
N_EXPERTS = 16
N_EXPERT_GROUPS = 4
EXPERTS_PER_GROUP = N_EXPERTS // N_EXPERT_GROUPS
TOP_K = 2
D_EXPERT = 512

HY_COLS = (HY_ORDER + 1) * D_GROUP
GM_COLS = 2 * D_GROUP
CV_COLS = 2 * D_GROUP
MQ_COLS = MLA_Q_RANK
MKV_COLS = MLA_KV_RANK + MLA_ROPE
HY_OFF = 0
GM_OFF = HY_OFF + HY_COLS
CV_OFF = GM_OFF + GM_COLS
MQ_OFF = CV_OFF + CV_COLS
MKV_OFF = MQ_OFF + MQ_COLS
IN_COLS = MKV_OFF + MKV_COLS

LANES = 128
SUBLANES = 8
HEAD_SLOT = LANES
QK_COLS = MLA_HEADS * HEAD_SLOT
V_COLS = MLA_HEADS * MLA_V
PQ_OFF = MQ_OFF
PQ_PAD = 256
PKV_OFF = PQ_OFF + PQ_PAD
PKR_OFF = PKV_OFF + MLA_KV_RANK
PROJ_COLS = PKR_OFF + LANES

PAIR_LO = (0, 0, 0, 1, 1, 2)
PAIR_HI = (1, 2, 3, 2, 3, 3)
N_PAIRS = len(PAIR_LO)
N_CLASSES = N_EXPERT_GROUPS * N_PAIRS
CLASS_ROWS = -(-N_CLASSES // SUBLANES) * SUBLANES

ATT_TILE = 512
ATT_SUB = 256
MOE_TILE = 256
DFT_ROWS = 256
DFT_COLS = 1024
CONV_HALO = 16
VMEM_LIMIT = 56 * 1024 * 1024


def _cparams(n_axes):
    return pltpu.CompilerParams(dimension_semantics=("arbitrary",) * n_axes, vmem_limit_bytes=VMEM_LIMIT)


def _axial_rope_tables(n_lat):
    rows = n_lat // GRID_W
    row = jnp.repeat(jnp.arange(rows), GRID_W).astype(F32)
    col = jnp.tile(jnp.arange(GRID_W), rows).astype(F32)
    n_freq = MLA_ROPE // 4
    inv = ROPE_BASE ** (-jnp.arange(n_freq, dtype=F32) / n_freq)
    ang = jnp.concatenate([row[:, None] * inv, col[:, None] * inv], axis=-1)
    return jnp.cos(ang), jnp.sin(ang)


def _hyena_filters(L, w1, b1, freq, w2, b2, w3):
    w1, b1, freq, w2, b2, w3 = (a.astype(F32) for a in (w1, b1, freq, w2, b2, w3))
    t_all = jnp.linspace(0.0, 1.0, L, dtype=F32)
    f = jnp.linspace(1e-4, HY_BANDS - 1, HY_BANDS, dtype=F32)[None, :]
    deltas = jnp.abs(jnp.linspace(math.log(HY_TARGET) / HY_SLOW_PCT,
                                  math.log(HY_TARGET) / HY_FAST_PCT, D_GROUP, dtype=F32))

    def taps(pos, direction):
        t = t_all[pos][:, None]
        w = 2.0 * math.pi * pos.astype(F32)[:, None] / L
        z = jnp.concatenate([t, jnp.cos(f * w), -jnp.sin(f * w)], axis=-1)
        h = jnp.sin(freq * (z @ w1 + b1))
        h = jnp.sin(freq * (h @ w2 + b2))
        h = (h @ w3).reshape(L, HY_ORDER, 2, D_GROUP)[:, :, direction]
        return h * jnp.exp(-t * deltas)[:, None, :]

    m = jnp.arange(L, dtype=jnp.int32)
    kf = taps(m, 0)
    kb = jnp.where((m > 0)[:, None, None], taps((L - m) % L, 1), 0.0)
    norm = jnp.sum(jnp.abs(kf), axis=0, keepdims=True) + jnp.sum(jnp.abs(kb), axis=0, keepdims=True)
    w_cols = HY_ORDER * D_GROUP
    return jnp.concatenate([(kf / norm).reshape(L, w_cols), (kb / norm).reshape(L, w_cols)], axis=1)


def _pack_w_in(w_in):
    D = w_in.shape[0]
    z = lambda n: jnp.zeros((D, n), w_in.dtype)
    return jnp.concatenate([
        w_in[:, :MQ_OFF],
        w_in[:, MQ_OFF:MKV_OFF], z(PQ_PAD - MQ_COLS),
        w_in[:, MKV_OFF:MKV_OFF + MLA_KV_RANK],
        w_in[:, MKV_OFF + MLA_KV_RANK:], z(LANES - MLA_ROPE),
    ], axis=1).astype(BF16)


def _pack_w_uq(w_uq):
    w = w_uq.reshape(MLA_Q_RANK, MLA_HEADS, MLA_NOPE + MLA_ROPE)
    w = jnp.pad(w, ((0, PQ_PAD - MLA_Q_RANK), (0, 0), (0, HEAD_SLOT - MLA_NOPE - MLA_ROPE)))
    return w.reshape(PQ_PAD, QK_COLS).astype(BF16)


def _pack_w_ukv(w_ukv):
    w = w_ukv.reshape(MLA_KV_RANK, MLA_HEADS, MLA_NOPE + MLA_V)
    k_part = jnp.pad(w[:, :, :MLA_NOPE], ((0, 0), (0, 0), (0, HEAD_SLOT - MLA_NOPE))).reshape(MLA_KV_RANK, QK_COLS)
    v_part = w[:, :, MLA_NOPE:].reshape(MLA_KV_RANK, V_COLS)
    top = jnp.concatenate([k_part, v_part], axis=1)
    eye = jnp.eye(LANES, dtype=w_ukv.dtype)[:, :MLA_ROPE]
    place = jnp.pad(eye, ((0, 0), (MLA_NOPE, HEAD_SLOT - MLA_NOPE - MLA_ROPE)))
    bot = jnp.concatenate([jnp.tile(place, (1, MLA_HEADS)), jnp.zeros((LANES, V_COLS), w_ukv.dtype)], axis=1)
    return jnp.concatenate([top, bot], axis=0).astype(BF16)


def _rope_slot_tables(n_lat):
    cos, sin = _axial_rope_tables(n_lat)
    half = MLA_ROPE // 2
    tail_w = HEAD_SLOT - MLA_NOPE - MLA_ROPE
    ones = lambda n: jnp.ones((n_lat, n), F32)
    zeros = lambda n: jnp.zeros((n_lat, n), F32)
    cf = jnp.concatenate([ones(MLA_NOPE), cos, cos, ones(tail_w)], axis=1)
    s_up = jnp.concatenate([zeros(MLA_NOPE + half), sin, zeros(tail_w)], axis=1)
    s_dn = jnp.concatenate([zeros(MLA_NOPE), -sin, zeros(half + tail_w)], axis=1)
    return cf, s_up, s_dn


def _identity_slot_tables(rows):
    return (jnp.ones((rows, HEAD_SLOT), F32), jnp.zeros((rows, HEAD_SLOT), F32), jnp.zeros((rows, HEAD_SLOT), F32))


def _rotate_slots(t, cf, s_up, s_dn):
    half = MLA_ROPE // 2
    outs = []
    for h in range(MLA_HEADS):
        tb = t[:, h * HEAD_SLOT:(h + 1) * HEAD_SLOT]
        outs.append(tb * cf + pltpu.roll(tb, half, 1) * s_up + pltpu.roll(tb, HEAD_SLOT - half, 1) * s_dn)
    return jnp.concatenate(outs, axis=1)


def _row_gather_copy(src_hbm, row, buf_ref, slot, j, sem_ref):
    return pltpu.make_async_copy(src_hbm.at[pl.ds(row, 1)], buf_ref.at[slot, pl.ds(j, 1)], sem_ref.at[slot])


def _start_row_gather(src_hbm, row_of, n_rows, buf_ref, slot, sem_ref):
    for j in range(n_rows):
        _row_gather_copy(src_hbm, row_of(j), buf_ref, slot, j, sem_ref).start(priority=j % 2)


def _wait_row_gather(src_hbm, n_rows, buf_ref, slot, sem_ref):
    pltpu.make_async_copy(src_hbm.at[pl.ds(0, n_rows)], buf_ref.at[slot], sem_ref.at[slot]).wait()


def _gathered_rows(idx_ref, ys_hbm, buf_ref, sem_ref, tm):
    i = pl.program_id(0)
    n = pl.num_programs(0)
    slot = i % 2

    @pl.when(i == 0)
    def _():
        _start_row_gather(ys_hbm, lambda j: idx_ref[j], tm, buf_ref, 0, sem_ref)

    nxt = jnp.minimum(i + 1, n - 1)
    _start_row_gather(ys_hbm, lambda j: idx_ref[nxt * tm + j], tm, buf_ref, 1 - slot, sem_ref)
    _wait_row_gather(ys_hbm, tm, buf_ref, slot, sem_ref)
    return buf_ref[slot]


def _drain_gathered_rows(ys_hbm, buf_ref, sem_ref, tm):
    i = pl.program_id(0)

    @pl.when(i == pl.num_programs(0) - 1)
    def _():
        _wait_row_gather(ys_hbm, tm, buf_ref, 1 - i % 2, sem_ref)


def _proj_gather_kernel(idx_ref, xm_ref, g2_ref, ys_hbm, a_ref, sh_ref, win_ref, qg_ref, wuq_ref, kvg_ref, wkv_ref,
                        cf_ref, su_ref, sd_ref, hy_ref, gm_ref, cv_ref, q_ref, k_ref, v_ref, xn_ref, buf_ref, sem_ref):
    y = _gathered_rows(idx_ref, ys_hbm, buf_ref, sem_ref, xm_ref.shape[0])
    xn_ref[...] = xm_ref[...] + g2_ref[...] * y
    _proj_kernel(xn_ref, a_ref, sh_ref, win_ref, qg_ref, wuq_ref, kvg_ref, wkv_ref, cf_ref, su_ref, sd_ref,
                 hy_ref, gm_ref, cv_ref, q_ref, k_ref, v_ref)
    _drain_gathered_rows(ys_hbm, buf_ref, sem_ref, xm_ref.shape[0])


def _proj_kernel(x_ref, a_ref, sh_ref, win_ref, qg_ref, wuq_ref, kvg_ref, wkv_ref, cf_ref, su_ref, sd_ref,
                 hy_ref, gm_ref, cv_ref, q_ref, k_ref, v_ref):
    x = x_ref[...]
    ms = jnp.mean(x * x, axis=-1, keepdims=True)
    h = x * lax.rsqrt(ms + EPS) * a_ref[...] + sh_ref[...]
    p = jnp.dot(h.astype(BF16), win_ref[...], preferred_element_type=F32)
    hy_ref[...] = p[:, HY_OFF:GM_OFF]
    gm_ref[...] = p[:, GM_OFF:CV_OFF]
    cv_ref[...] = p[:, CV_OFF:MQ_OFF]

    cf, su, sd = cf_ref[...], su_ref[...], sd_ref[...]
    cq = p[:, PQ_OFF:PQ_OFF + PQ_PAD]
    qn = cq * lax.rsqrt(jnp.sum(cq * cq, axis=-1, keepdims=True) * (1.0 / MLA_Q_RANK) + EPS) * qg_ref[...]
    q = jnp.dot(qn.astype(BF16), wuq_ref[...], preferred_element_type=F32)
    q = _rotate_slots(q, cf, su, sd) * (math.log2(math.e) / math.sqrt(MLA_NOPE + MLA_ROPE))
    q_ref[...] = q.astype(BF16)

    ckv = p[:, PKV_OFF:PKV_OFF + MLA_KV_RANK]
    kvn = ckv * lax.rsqrt(jnp.mean(ckv * ckv, axis=-1, keepdims=True) + EPS) * kvg_ref[...]
    kin = jnp.concatenate([kvn, p[:, PKR_OFF:PKR_OFF + LANES]], axis=1).astype(BF16)
    kv = jnp.dot(kin, wkv_ref[...], preferred_element_type=F32)
    k_ref[...] = jnp.transpose(_rotate_slots(kv[:, :QK_COLS], cf, su, sd)).astype(BF16)
    v_ref[...] = kv[:, QK_COLS:].astype(BF16)


def _proj_call(xt, mod_a, mod_sh, mod_map, win, qg, wuq, kvg, wkv, rope_tabs, rope_map, tm, moe_in=None):
    T, D = xt.shape
    const = lambda i, *_: (0, 0)
    tok = lambda i, *_: (i, 0)
    mod = lambda i, *_: mod_map(i)
    rope = lambda i, *_: rope_map(i)
    out_cols = (HY_COLS, GM_COLS, CV_COLS, QK_COLS, QK_COLS, V_COLS)
    out_dtypes = (F32, F32, F32, BF16, BF16, BF16)
    in_specs = [
        pl.BlockSpec((None, 1, D), mod),
        pl.BlockSpec((None, 1, D), mod),
        pl.BlockSpec(win.shape, const),
        pl.BlockSpec(qg.shape, const),
        pl.BlockSpec(wuq.shape, const),
        pl.BlockSpec(kvg.shape, const),
        pl.BlockSpec(wkv.shape, const),
        pl.BlockSpec((tm, HEAD_SLOT), rope),
        pl.BlockSpec((tm, HEAD_SLOT), rope),
        pl.BlockSpec((tm, HEAD_SLOT), rope),
    ]
    out_specs = [pl.BlockSpec((tm, n), tok) for n in out_cols]
    out_shape = [jax.ShapeDtypeStruct((T, n), dt) for n, dt in zip(out_cols, out_dtypes)]
    out_specs[4] = pl.BlockSpec((QK_COLS, tm), lambda i, *_: (0, i))
    out_shape[4] = jax.ShapeDtypeStruct((QK_COLS, T), BF16)
    shared = (mod_a, mod_sh, win, qg, wuq, kvg, wkv, *rope_tabs)
    if moe_in is None:
        return pl.pallas_call(
            _proj_kernel,
            grid=(T // tm,),
            in_specs=[pl.BlockSpec((tm, D), tok)] + in_specs,
            out_specs=out_specs,
            out_shape=out_shape,
            compiler_params=_cparams(1),
            name="proj_qkv",
        )(xt, *shared)
    tok_row, g2, ys = moe_in
    grid_spec = pltpu.PrefetchScalarGridSpec(
        num_scalar_prefetch=1,
        grid=(T // tm,),
        in_specs=[pl.BlockSpec((tm, D), tok), pl.BlockSpec((None, 1, D), mod),
                  pl.BlockSpec(memory_space=pl.ANY)] + in_specs,
        out_specs=out_specs + [pl.BlockSpec((tm, D), tok)],
        scratch_shapes=[pltpu.VMEM((2, tm, ys.shape[1]), ys.dtype), pltpu.SemaphoreType.DMA((2,))],
    )
    return pl.pallas_call(
        _proj_gather_kernel,
        grid_spec=grid_spec,
        out_shape=out_shape + [jax.ShapeDtypeStruct((T, D), F32)],
        compiler_params=_cparams(1),
        name="moe_residual_proj_qkv",
    )(tok_row, xt, g2, ys, *shared)


def _attend_heads(q_ref, key_refs, val_refs, o_ref):
    lane = lax.broadcasted_iota(jnp.int32, (1, V_COLS), 1)
    vals = [v_ref[...] for v_ref in val_refs]
    sub = min(ATT_SUB, q_ref.shape[0])
    for r in range(q_ref.shape[0] // sub):
        rows = slice(r * sub, (r + 1) * sub)
        acc = jnp.zeros((sub, V_COLS), F32)
        for h in range(MLA_HEADS):
            sl_h = slice(h * HEAD_SLOT, (h + 1) * HEAD_SLOT)
            q = q_ref[rows, sl_h]
            scores = [jnp.dot(q, kt_ref[sl_h, :], preferred_element_type=F32) for kt_ref in key_refs]
            m = functools.reduce(jnp.maximum, [jnp.max(s, axis=-1, keepdims=True) for s in scores])
            probs = [jnp.exp2(s - m) for s in scores]
            denom = functools.reduce(jnp.add, [jnp.sum(p, axis=-1, keepdims=True) for p in probs])
            o = functools.reduce(jnp.add, [jnp.dot(p.astype(BF16), v, preferred_element_type=F32)
                                           for p, v in zip(probs, vals)])
            in_head = (lane >= h * MLA_V) & (lane < (h + 1) * MLA_V)
            acc = acc + jnp.where(in_head, o / denom, 0.0)
        o_ref[rows, :] = acc


def _attn_lat_kernel(q_ref, kl_ref, kc_ref, vl_ref, vc_ref, o_ref):
    _attend_heads(q_ref, (kl_ref, kc_ref), (vl_ref, vc_ref), o_ref)


def _attn_ctx_kernel(q_ref, k_ref, v_ref, o_ref):
    _attend_heads(q_ref, (k_ref,), (v_ref,), o_ref)


def _attn_lat_call(q, k, v, k_ctx, v_ctx, n_batch, n_lat, n_ctx):
    tq = ATT_TILE
    qt = n_lat // tq
    return pl.pallas_call(
        _attn_lat_kernel,
        grid=(n_batch, qt),
        in_specs=[
            pl.BlockSpec((tq, QK_COLS), lambda b, j: (b * qt + j, 0)),
            pl.BlockSpec((QK_COLS, n_lat), lambda b, j: (0, b)),
            pl.BlockSpec((QK_COLS, n_ctx), lambda b, j: (0, b)),
            pl.BlockSpec((n_lat, V_COLS), lambda b, j: (b, 0)),
            pl.BlockSpec((n_ctx, V_COLS), lambda b, j: (b, 0)),
        ],
        out_specs=pl.BlockSpec((tq, V_COLS), lambda b, j: (b * qt + j, 0)),
        out_shape=jax.ShapeDtypeStruct((n_batch * n_lat, V_COLS), F32),
        compiler_params=_cparams(2),
        name="attn_latent",
    )(q, k, k_ctx, v, v_ctx)


def _attn_ctx_call(q, k, v, n_batch, n_ctx):
    blk = lambda b: (b, 0)
    return pl.pallas_call(
        _attn_ctx_kernel,
        grid=(n_batch,),
        in_specs=[pl.BlockSpec((n_ctx, QK_COLS), blk), pl.BlockSpec((QK_COLS, n_ctx), lambda b: (0, b)),
                  pl.BlockSpec((n_ctx, V_COLS), blk)],
        out_specs=pl.BlockSpec((n_ctx, V_COLS), blk),
        out_shape=jax.ShapeDtypeStruct((n_batch * n_ctx, V_COLS), F32),
        compiler_params=_cparams(1),
        name="attn_context",
    )(q, k, v)


@functools.lru_cache(maxsize=None)
def _dft_factor_tables_np(L):
    f = np.arange(L, dtype=np.int64)[:, None]
    def trig(t):
        ang = ((f * t[None, :]) % (2 * L)).astype(np.float64) * (np.pi / L)
        return np.cos(ang).astype(np.float32), np.sin(ang).astype(np.float32)
    return trig(np.arange(L // LANES, dtype=np.int64) * LANES) + trig(np.arange(LANES, dtype=np.int64))


def _dft_matrices(L):
    assert L % LANES == 0
    c1, s1, c0, s0 = (jnp.asarray(a) for a in _dft_factor_tables_np(L))
    c1, s1, c0, s0 = c1[:, :, None], s1[:, :, None], c0[:, None, :], s0[:, None, :]
    c = (c1 * c0 - s1 * s0).reshape(L, L)
    s = (s1 * c0 + c1 * s0).reshape(L, L)
    alt = jnp.where(jnp.arange(L) % 2 == 0, 1.0, -1.0).astype(F32)
    row0 = (jnp.arange(L) == 0)
    s_fwd = jnp.where(row0[:, None], alt[None, :], s)
    s_inv = jnp.where(row0[None, :], alt[:, None], s)
    return c.astype(BF16), s_fwd.astype(BF16), s_inv.astype(BF16)


def _half_dft(L):
    assert L % 2 == 0
    return _dft_matrices(L // 2) + tuple(jnp.asarray(t) for t in _twiddle_np(L // 2))


def _half_butterfly(ae, ao, be, bo, c, s):
    return (ae + c * ao - s * bo, be + c * bo + s * ao,
            ae - c * ao + s * bo, -be + c * bo + s * ao)


def _twiddle_np(M):
    ang = np.arange(M, dtype=np.float64)[:, None] * (np.pi / (2 * M))
    return np.cos(ang).astype(np.float32), np.sin(ang).astype(np.float32)


def _spectrum_kernel(c_ref, s_ref, k_ref, tc_ref, ts_ref, krl_ref, kil_ref, krh_ref, kih_ref, sp_ref):
    tr, M = c_ref.shape
    w = k_ref.shape[1] // 4
    inv_l = 1.0 / (2 * M)
    k = k_ref[...]
    a = jnp.dot(c_ref[...], k, preferred_element_type=F32)
    b = jnp.dot(s_ref[...], k, preferred_element_type=F32)
    g = pl.program_id(0) * tr + lax.broadcasted_iota(jnp.int32, (tr, 1), 0)
    sign = jnp.where(g % 2 == 0, 1.0, -1.0)
    tc, ts = tc_ref[...], ts_ref[...]
    cols = lambda x, j: x[:, j * w:(j + 1) * w]
    f_lo_a, f_lo_b, f_hi_a, f_hi_b = _half_butterfly(cols(a, 0), cols(a, 2), cols(b, 0), cols(b, 2), tc, ts)
    b_lo_a, b_lo_b, b_hi_a, b_hi_b = _half_butterfly(cols(a, 1), cols(a, 3), cols(b, 1), cols(b, 3), tc, ts)
    kr_lo = (f_lo_a + sign * b_lo_a) * inv_l
    kr_hi = (f_hi_a + sign * b_hi_a) * inv_l
    krl_ref[...] = kr_lo
    kil_ref[...] = -(f_lo_b + sign * b_lo_b) * inv_l
    krh_ref[...] = kr_hi
    kih_ref[...] = -(f_hi_b + sign * b_hi_b) * inv_l

    @pl.when(pl.program_id(0) == 0)
    def _():
        mid_sign = 1.0 if M % 2 == 0 else -1.0
        sp_ref[...] = jnp.zeros_like(sp_ref)
        sp_ref[0:1, :] = 0.5 * kr_lo[0:1, :]
        sp_ref[1:2, :] = 0.5 * kr_hi[0:1, :]
        sp_ref[2:3, :] = (cols(b, 0)[0:1, :] + mid_sign * cols(b, 1)[0:1, :]) * inv_l
        sp_ref[3:4, :] = -(cols(b, 2)[0:1, :] + mid_sign * cols(b, 3)[0:1, :]) * inv_l


def _spectrum_call(dft, filters):
    cmat, smat, _, tc, ts = dft
    M = cmat.shape[0]
    w2 = filters.shape[1]
    w = w2 // 2
    samples = filters.reshape(M, 2 * w2).astype(BF16)
    tr = min(DFT_ROWS, M)
    row = lambda i: (i, 0)
    return pl.pallas_call(
        _spectrum_kernel,
        grid=(M // tr,),
        in_specs=[pl.BlockSpec((tr, M), row), pl.BlockSpec((tr, M), row),
                  pl.BlockSpec((M, 2 * w2), lambda i: (0, 0), pipeline_mode=pl.Buffered(1)),
                  pl.BlockSpec((tr, 1), row), pl.BlockSpec((tr, 1), row)],
        out_specs=[pl.BlockSpec((tr, w), row)] * 4 + [pl.BlockSpec((SUBLANES, w), lambda i: (0, 0))],
        out_shape=[jax.ShapeDtypeStruct((M, w), F32)] * 4 + [jax.ShapeDtypeStruct((SUBLANES, w), F32)],
        compiler_params=_cparams(1),
        name="hyena_filter_spectrum",
    )(cmat, smat, samples, tc, ts)


def _hy_prep_kernel(p_ref, w_ref, b_ref, z_ref, v16_ref):
    j = pl.program_id(1) // (D_GROUP // LANES)
    M = z_ref.shape[1]
    pe = p_ref[pl.ds(0, M, stride=2), :]
    po = p_ref[pl.ds(1, M, stride=2), :]
    row = lax.broadcasted_iota(jnp.int32, (M, 1), 0)
    po_prev = jnp.where(row == 0, 0.0, pltpu.roll(po, 1, 0))
    pe_next = jnp.where(row == M - 1, 0.0, pltpu.roll(pe, M - 1, 0))
    w0, w1, w2, bias = w_ref[0:1, :], w_ref[1:2, :], w_ref[2:3, :], b_ref[...]
    ze = po_prev * w0 + pe * w1 + po * w2 + bias
    zo = pe * w0 + po * w1 + pe_next * w2 + bias
    z_ref[0] = ze
    z_ref[1] = zo

    @pl.when(j == HY_ORDER)
    def _():
        v16_ref[0] = ze.astype(BF16)
        v16_ref[1] = zo.astype(BF16)


def _hy_prep_call(p_hy, short_w, short_b, n_batch, L):
    N = n_batch * D_GROUP
    M = L // 2
    hb = D_GROUP // LANES
    return pl.pallas_call(
        _hy_prep_kernel,
        grid=(n_batch, (HY_ORDER + 1) * hb),
        in_specs=[
            pl.BlockSpec((L, LANES), lambda b, j: (b, j)),
            pl.BlockSpec((HY_SHORT, LANES), lambda b, j: (0, j)),
            pl.BlockSpec((1, LANES), lambda b, j: (0, j)),
        ],
        out_specs=[pl.BlockSpec((None, 2, M, LANES), lambda b, j: (j // hb, 0, 0, b * hb + j % hb)),
                   pl.BlockSpec((2, M, LANES), lambda b, j: (0, 0, b * hb + jnp.where(j // hb == HY_ORDER, j % hb, 0)))],
        out_shape=[jax.ShapeDtypeStruct((HY_ORDER + 1, 2, M, N), F32), jax.ShapeDtypeStruct((2, M, N), BF16)],
        compiler_params=_cparams(2),
        name="hyena_short_conv",
    )(p_hy, short_w, short_b[None, :])


def _dft_fwd_kernel(c_ref, s_ref, ue_ref, uo_ref, tc_ref, ts_ref, krl_ref, kil_ref, krh_ref, kih_ref, sp_ref,
                    e_ref):
    tr = c_ref.shape[0]
    c, s = c_ref[...], s_ref[...]
    ue, uo = ue_ref[...], uo_ref[...]
    ae = jnp.dot(c, ue, preferred_element_type=F32)
    ao = jnp.dot(c, uo, preferred_element_type=F32)
    be = jnp.dot(s, ue, preferred_element_type=F32)
    bo = jnp.dot(s, uo, preferred_element_type=F32)
    tc, ts = tc_ref[...], ts_ref[...]
    first = (pl.program_id(1) * tr + lax.broadcasted_iota(jnp.int32, (tr, 1), 0)) == 0
    krl, kil, krh, kih = krl_ref[...], kil_ref[...], krh_ref[...], kih_ref[...]
    k0h, kLh, krm, kim = sp_ref[0:1, :], sp_ref[1:2, :], sp_ref[2:3, :], sp_ref[3:4, :]
    for g in range(ue.shape[1] // D_GROUP):
        sl = slice(g * D_GROUP, (g + 1) * D_GROUP)
        a1, b1, a2, b2 = _half_butterfly(ae[:, sl], ao[:, sl], be[:, sl], bo[:, sl], tc, ts)
        p1, q1 = krl * a1 + kil * b1, krl * b1 - kil * a1
        p2, q2 = krh * a2 + kih * b2, krh * b2 - kih * a2
        pm, qp = p1 - p2, q1 + q2
        dc, ny = k0h * a1, kLh * a2
        alt_e, alt_o = be[:, sl], bo[:, sl]
        e_ref[0, :, sl] = jnp.where(first, dc + ny, p1 + p2).astype(BF16)
        e_ref[1, :, sl] = jnp.where(first, krm * alt_e + kim * alt_o, q1 - q2).astype(BF16)
        e_ref[2, :, sl] = jnp.where(first, dc - ny, tc * pm + ts * qp).astype(BF16)
        e_ref[3, :, sl] = jnp.where(first, krm * alt_o - kim * alt_e, tc * qp - ts * pm).astype(BF16)


def _dft_fwd_call(dft, u16, tabs, order):
    cmat, smat, _, tc, ts = dft
    _, M, N = u16.shape
    tr = min(DFT_ROWS, M)
    tn = min(DFT_COLS, N)
    row = lambda h, i: (i, 0)
    tab = lambda h, i: (i, order)
    return pl.pallas_call(
        _dft_fwd_kernel,
        grid=(N // tn, M // tr),
        in_specs=[
            pl.BlockSpec((tr, M), row),
            pl.BlockSpec((tr, M), row),
            pl.BlockSpec((None, M, tn), lambda h, i: (0, 0, h), pipeline_mode=pl.Buffered(1)),
            pl.BlockSpec((None, M, tn), lambda h, i: (1, 0, h), pipeline_mode=pl.Buffered(1)),
            pl.BlockSpec((tr, 1), row), pl.BlockSpec((tr, 1), row),
        ] + [pl.BlockSpec((tr, D_GROUP), tab)] * 4 + [pl.BlockSpec((SUBLANES, D_GROUP), lambda h, i: (0, order))],
        out_specs=pl.BlockSpec((4, tr, tn), lambda h, i: (0, i, h)),
        out_shape=jax.ShapeDtypeStruct((4, M, N), BF16),
        compiler_params=_cparams(2),
        name="hyena_dft_fwd",
    )(cmat, smat, u16, u16, tc, ts, *tabs)


def _dft_inv_kernel(c_ref, st_ref, ec_ref, es_ref, oc_ref, os_ref, u_ref, g_ref, bias_ref, y_ref, y16_ref):
    c, st = c_ref[...], st_ref[...]
    bias = bias_ref[...]
    conv_e = (jnp.dot(c, ec_ref[...], preferred_element_type=F32)
              + jnp.dot(st, es_ref[...], preferred_element_type=F32))
    conv_o = (jnp.dot(c, oc_ref[...], preferred_element_type=F32)
              + jnp.dot(st, os_ref[...], preferred_element_type=F32))
    for par, conv in enumerate((conv_e, conv_o)):
        y = g_ref[par] * (conv + u_ref[par] * bias)
        y_ref[par] = y
        y16_ref[par] = y.astype(BF16)


def _dft_inv_call(dft, e16, u_arr, u_sel, g_arr, g_sel, bias_row):
    cmat, _, stmat, _, _ = dft
    _, M, N = e16.shape
    tr = min(DFT_ROWS, M)
    tn = min(DFT_COLS, N)
    row = lambda h, i: (i, 0)
    plane = lambda k: pl.BlockSpec((None, M, tn), lambda h, i: (k, 0, h), pipeline_mode=pl.Buffered(1))
    return pl.pallas_call(
        _dft_inv_kernel,
        grid=(N // tn, M // tr),
        in_specs=[
            pl.BlockSpec((tr, M), row),
            pl.BlockSpec((tr, M), row),
            plane(0), plane(1), plane(2), plane(3),
            pl.BlockSpec((None, 2, tr, tn), lambda h, i: (u_sel, 0, i, h)),
            pl.BlockSpec((None, 2, tr, tn), lambda h, i: (g_sel, 0, i, h)),
            pl.BlockSpec((1, tn), lambda h, i: (0, h)),
        ],
        out_specs=[pl.BlockSpec((2, tr, tn), lambda h, i: (0, i, h))] * 2,
        out_shape=[jax.ShapeDtypeStruct((2, M, N), F32), jax.ShapeDtypeStruct((2, M, N), BF16)],
        compiler_params=_cparams(2),
        name="hyena_dft_inv",
    )(cmat, stmat, e16, e16, e16, e16, u_arr, g_arr, bias_row)


def _hyena_call(p_hy, short_w, short_b, filters, hy_bias, n_batch, L, dft):
    tabs = _spectrum_call(dft, filters)
    z, u16 = _hy_prep_call(p_hy, short_w, short_b, n_batch, L)
    y_stack = z
    u_sel = HY_ORDER
    for n in range(HY_ORDER):
        e16 = _dft_fwd_call(dft, u16, tabs, n)
        bias_row = jnp.tile(hy_bias[n][None, :].astype(F32), (1, n_batch))
        y, u16 = _dft_inv_call(dft, e16, y_stack, u_sel, z, n, bias_row)
        y_stack, u_sel = y[None], 0
    return y


def _mixers_kernel(gm_ref, cv_ref, cvp_ref, cvn_ref, lng_ref, lnb_ref, ws_ref, bsf_ref, dww_ref, dwb_ref,
                   cg_ref, cb_ref, avg_ref, ygm_ref, ycv_ref, glu_ref, *, tiles_per_seq):
    i = pl.program_id(0)
    tm = gm_ref.shape[0]
    lane = lax.broadcasted_iota(jnp.int32, (1, D_GROUP), 1)

    z = jax.nn.gelu(gm_ref[...], approximate=True)
    u, v = z[:, :D_GROUP], z[:, D_GROUP:]
    mu = jnp.mean(v, axis=-1, keepdims=True)
    vc = v - mu
    var = jnp.mean(vc * vc, axis=-1, keepdims=True)
    vn = (vc * lax.rsqrt(var + EPS) * lng_ref[...] + lnb_ref[...]).astype(BF16)
    hd = D_GROUP // GM_HEADS
    for c in range(tm // GM_CHUNK):
        rows = slice(c * GM_CHUNK, (c + 1) * GM_CHUNK)
        s = bsf_ref[...]
        for g in range(GM_HEADS):
            sg = jnp.dot(ws_ref[g], vn[rows, :], preferred_element_type=F32)
            s = s + jnp.where((lane >= g * hd) & (lane < (g + 1) * hd), sg, 0.0)
        ygm_ref[rows, :] = u[rows, :] * s

    def glu(t):
        return t[:, :D_GROUP] * jax.nn.sigmoid(t[:, D_GROUP:])

    first = (i % tiles_per_seq) == 0
    last = (i % tiles_per_seq) == tiles_per_seq - 1
    span = tm + 2 * CONV_HALO
    glu_ref[0, 0:CONV_HALO, :] = jnp.where(first, 0.0, glu(cvp_ref[...]))
    glu_ref[0, CONV_HALO:CONV_HALO + tm, :] = glu(cv_ref[...])
    glu_ref[0, CONV_HALO + tm:span, :] = jnp.where(last, 0.0, glu(cvn_ref[...]))
    glu_ref[0, span:, :] = jnp.zeros((SUBLANES, D_GROUP), F32)
    for b in range(1, SUBLANES):
        glu_ref[b, 0:span, :] = glu_ref[0, b:b + span, :]
    pad = (CV_WIDTH - 1) // 2
    rc = 128

    def group_mean(t):
        hi = t.astype(BF16)
        lo = (t - hi.astype(F32)).astype(BF16)
        return (jnp.dot(hi, avg_ref[...], preferred_element_type=F32)
                + jnp.dot(lo, avg_ref[...], preferred_element_type=F32))

    for c in range(tm // rc):
        acc = jnp.zeros((rc, D_GROUP), F32) + dwb_ref[...]
        for k in range(CV_WIDTH):
            start = c * rc + CONV_HALO - pad + k
            b = start % SUBLANES
            acc = acc + glu_ref[b, start - b:start - b + rc, :] * dww_ref[k:k + 1, :]
        d = acc - group_mean(acc)
        gvar = group_mean(d * d)
        n = d * lax.rsqrt(gvar + EPS) * cg_ref[...] + cb_ref[...]
        ycv_ref[c * rc:(c + 1) * rc, :] = n * jax.nn.sigmoid(n)


def _mixers_call(p_gm, p_cv, gm_ln_g, gm_ln_b, gm_ws, gm_bs, cv_dw_w, cv_dw_b, cv_ln_g, cv_ln_b, L, tm):
    T = p_gm.shape[0]
    tps = L // tm
    hb = tm // CONV_HALO
    n_hblk = T // CONV_HALO
    const2 = lambda i: (0, 0)
    tok = lambda i: (i, 0)
    bs_full = jnp.repeat(gm_bs.T.astype(F32), D_GROUP // GM_HEADS, axis=1)
    gid = np.arange(D_GROUP) // (D_GROUP // CV_GROUPS)
    avg = jnp.asarray((gid[:, None] == gid[None, :]).astype(np.float32) / (D_GROUP // CV_GROUPS)).astype(BF16)
    row = lambda a: a[None, :].astype(F32)
    return pl.pallas_call(
        functools.partial(_mixers_kernel, tiles_per_seq=tps),
        grid=(T // tm,),
        in_specs=[
            pl.BlockSpec((tm, GM_COLS), tok),
            pl.BlockSpec((tm, CV_COLS), tok),
            pl.BlockSpec((CONV_HALO, CV_COLS), lambda i: (jnp.maximum(i * hb - 1, 0), 0)),
            pl.BlockSpec((CONV_HALO, CV_COLS), lambda i: (jnp.minimum((i + 1) * hb, n_hblk - 1), 0)),
            pl.BlockSpec((1, D_GROUP), const2),
            pl.BlockSpec((1, D_GROUP), const2),
            pl.BlockSpec((GM_HEADS, GM_CHUNK, GM_CHUNK), lambda i: (0, 0, 0)),
            pl.BlockSpec((GM_CHUNK, D_GROUP), const2),
            pl.BlockSpec((CV_WIDTH, D_GROUP), const2),
            pl.BlockSpec((1, D_GROUP), const2),
            pl.BlockSpec((1, D_GROUP), const2),
            pl.BlockSpec((1, D_GROUP), const2),
            pl.BlockSpec((D_GROUP, D_GROUP), const2),
        ],
        out_specs=[pl.BlockSpec((tm, D_GROUP), tok)] * 2,
        out_shape=[jax.ShapeDtypeStruct((T, D_GROUP), F32)] * 2,
        scratch_shapes=[pltpu.VMEM((SUBLANES, tm + 2 * CONV_HALO + SUBLANES, D_GROUP), F32)],
        compiler_params=_cparams(1),
        name="gmlp_conv_mixers",
    )(p_gm, p_cv, p_cv, p_cv, row(gm_ln_g), row(gm_ln_b), gm_ws.astype(BF16), bs_full, cv_dw_w.astype(F32),
      row(cv_dw_b), row(cv_ln_g), row(cv_ln_b), avg)


def _out_kernel(hye_ref, hyo_ref, gm_ref, cv_ref, at_ref, x_ref, mg_ref, wo_ref, g1_ref, a2_ref, sh2_ref, wrh_ref,
                wrl_ref, rb_ref, tri_ref, cin_ref, xm_ref, hfx_ref, cls_ref, rank_ref, cnt_ref, carry_ref, hy_ref):
    i = pl.program_id(0)
    tm = x_ref.shape[0]
    for k in range(D_GROUP // LANES):
        hy_ref[k, pl.ds(0, tm // 2, stride=2), :] = hye_ref[:, k * LANES:(k + 1) * LANES]
        hy_ref[k, pl.ds(1, tm // 2, stride=2), :] = hyo_ref[:, k * LANES:(k + 1) * LANES]
    y_hy = jnp.concatenate([hy_ref[k] for k in range(D_GROUP // LANES)], axis=1)
    o = None
    for g, y_src in enumerate((y_hy, gm_ref, cv_ref, at_ref)):
        y = y_src[...]
        n = y * lax.rsqrt(jnp.mean(y * y, axis=-1, keepdims=True) + EPS) * mg_ref[:, g * D_GROUP:(g + 1) * D_GROUP]
        part = jnp.dot(n.astype(BF16), wo_ref[g * D_GROUP:(g + 1) * D_GROUP, :], preferred_element_type=F32)
        o = part if o is None else o + part
    xm = x_ref[...] + g1_ref[...] * o
    xm_ref[...] = xm
    hf = xm * lax.rsqrt(jnp.mean(xm * xm, axis=-1, keepdims=True) + EPS) * a2_ref[...] + sh2_ref[...]

    hf_hi = hf.astype(BF16)
    hf_lo = (hf - hf_hi.astype(F32)).astype(BF16)
    logits = (jnp.dot(hf_hi, wrh_ref[...], preferred_element_type=F32)
              + jnp.dot(hf_lo, wrh_ref[...], preferred_element_type=F32)
              + jnp.dot(hf_hi, wrl_ref[...], preferred_element_type=F32))
    cls, g_lo, g_hi = _route_top2(jnp.transpose(logits), rb_ref[...])
    cls_ref[...] = cls

    d_model = hf.shape[1]
    hfx_ref[:, :d_model] = hf
    gate_rows = jnp.concatenate([g_lo, g_hi, jnp.zeros((LANES - TOP_K, tm), F32)], axis=0)
    hfx_ref[:, d_model:] = jnp.transpose(gate_rows)

    @pl.when(i == 0)
    def _():
        carry_ref[...] = cin_ref[...]

    sub = lax.broadcasted_iota(jnp.int32, (CLASS_ROWS, tm), 0)
    onehot = sub == cls
    prefix = jnp.dot(onehot.astype(BF16), tri_ref[...], preferred_element_type=F32)
    carry = carry_ref[...]
    rank = jnp.sum(jnp.where(onehot, prefix + carry[:, 0:1], 0.0), axis=0, keepdims=True)
    rank_ref[...] = rank.astype(jnp.int32)
    carry = carry + jnp.sum(onehot.astype(F32), axis=1, keepdims=True)
    carry_ref[...] = carry
    cnt_ref[...] = carry


def _out_alias_kernel(*refs):
    _out_kernel(*refs[1:])


def _out_call(y_hy_t, y_gm, y_cv, y_at, xt, mixg, wo, g1, a2, sh2, mod_map, wr_hi, wr_lo, rbias, counts_in, L, tm,
              hfx_prev, t_total, row0):
    T, D = xt.shape
    tps = L // tm
    blk0 = row0 // tm
    const = lambda i: (0, 0)
    tok = lambda i: (i, 0)
    lane_tok = lambda i: (0, i)
    tri = jnp.asarray(np.triu(np.ones((tm, tm), np.float32), k=1)).astype(BF16)
    W = D + LANES
    parity = lambda par: pl.BlockSpec((None, tm // 2, D_GROUP), lambda i: (par, i % tps, i // tps))
    in_specs = [parity(0), parity(1)] + [pl.BlockSpec((tm, D_GROUP), tok)] * 3 + [
        pl.BlockSpec((tm, D), tok),
        pl.BlockSpec(mixg.shape, const),
        pl.BlockSpec(wo.shape, const),
        pl.BlockSpec((None, 1, D), mod_map),
        pl.BlockSpec((None, 1, D), mod_map),
        pl.BlockSpec((None, 1, D), mod_map),
        pl.BlockSpec(wr_hi.shape, const),
        pl.BlockSpec(wr_lo.shape, const),
        pl.BlockSpec(rbias.shape, const),
        pl.BlockSpec((tm, tm), const),
        pl.BlockSpec((CLASS_ROWS, LANES), const),
    ]
    args = (y_hy_t, y_hy_t, y_gm, y_cv, y_at, xt, mixg, wo, g1, a2, sh2, wr_hi, wr_lo, rbias, tri, counts_in)
    aliased = hfx_prev is not None
    return pl.pallas_call(
        _out_alias_kernel if aliased else _out_kernel,
        grid=(T // tm,),
        in_specs=([pl.BlockSpec(memory_space=pl.ANY)] if aliased else []) + in_specs,
        out_specs=[pl.BlockSpec((tm, D), tok), pl.BlockSpec((tm, W), lambda i: (blk0 + i, 0)),
                   pl.BlockSpec((1, tm), lane_tok), pl.BlockSpec((1, tm), lane_tok),
                   pl.BlockSpec((CLASS_ROWS, LANES), const)],
        out_shape=[jax.ShapeDtypeStruct((T, D), F32), jax.ShapeDtypeStruct((t_total, W), F32),
                   jax.ShapeDtypeStruct((1, T), jnp.int32), jax.ShapeDtypeStruct((1, T), jnp.int32),
                   jax.ShapeDtypeStruct((CLASS_ROWS, LANES), F32)],
        scratch_shapes=[pltpu.VMEM((CLASS_ROWS, LANES), F32), pltpu.VMEM((D_GROUP // LANES, tm, LANES), F32)],
        input_output_aliases={0: 1} if aliased else {},
        compiler_params=_cparams(1),
        name="mix_out_norm2_route",
    )(*(((hfx_prev,) if aliased else ()) + args))


def _first_max_flags(vals):
    m = functools.reduce(jnp.maximum, vals)
    flags, taken = [], None
    for v in vals:
        f = v >= m
        if taken is not None:
            f = f & jnp.logical_not(taken)
        flags.append(f)
        taken = f if taken is None else taken | f
    return flags, m


def _pick(flags, vals):
    out = vals[-1]
    for f, v in zip(flags[-2::-1], vals[-2::-1]):
        out = jnp.where(f, v, out)
    return out


def _route_top2(lt, bias_col):
    s_all = jax.nn.sigmoid(lt[:N_EXPERTS, :])
    sel_all = s_all + bias_col
    s = [s_all[e:e + 1, :] for e in range(N_EXPERTS)]
    sel = [sel_all[e:e + 1, :] for e in range(N_EXPERTS)]
    neg = -jnp.inf
    E = EXPERTS_PER_GROUP

    def top2(vals):
        f1, m1 = _first_max_flags(vals)
        rest = [jnp.where(f, neg, v) for f, v in zip(f1, vals)]
        f2, m2 = _first_max_flags(rest)
        return f1, m1, f2, m2

    scores = []
    for g in range(N_EXPERT_GROUPS):
        _, m1, _, m2 = top2(sel[g * E:(g + 1) * E])
        scores.append(m1 + m2)
    gflags, _ = _first_max_flags(scores)
    bsel = [_pick(gflags, [sel[g * E + j] for g in range(N_EXPERT_GROUPS)]) for j in range(E)]
    bs = [_pick(gflags, [s[g * E + j] for g in range(N_EXPERT_GROUPS)]) for j in range(E)]
    f1, _, f2, _ = top2(bsel)
    zero = jnp.zeros_like(bs[0])
    w1 = functools.reduce(jnp.add, [jnp.where(f, v, zero) for f, v in zip(f1, bs)])
    w2 = functools.reduce(jnp.add, [jnp.where(f, v, zero) for f, v in zip(f2, bs)])
    izero = jnp.zeros(w1.shape, jnp.int32)
    j1 = functools.reduce(jnp.add, [jnp.where(f, j, izero) for j, f in enumerate(f1)])
    j2 = functools.reduce(jnp.add, [jnp.where(f, j, izero) for j, f in enumerate(f2)])
    gi = functools.reduce(jnp.add, [jnp.where(f, g, izero) for g, f in enumerate(gflags)])
    lo, hi = jnp.minimum(j1, j2), jnp.maximum(j1, j2)
    pair = jnp.where(lo == 0, 0, jnp.where(lo == 1, 3, 5)) + hi - lo - 1
    tot = w1 + w2
    first_is_lo = j1 < j2
    return gi * N_PAIRS + pair, jnp.where(first_is_lo, w1, w2) / tot, jnp.where(first_is_lo, w2, w1) / tot


def _table_lookup(table, idx):
    n = table.shape[0]
    hit = idx[..., None] == jnp.arange(n, dtype=jnp.int32)
    return jnp.sum(jnp.where(hit, table, 0), axis=-1)


def _class_plan(cls, rank, counts, tile):
    T = cls.shape[0]
    n_tiles = (T + N_CLASSES * tile) // tile
    order = jnp.argsort(cls, stable=True).astype(jnp.int32)
    padded = ((counts + tile - 1) // tile) * tile
    seg_end = jnp.cumsum(padded)
    seg_start = seg_end - padded
    src_start = jnp.cumsum(counts) - counts
    tok_row = _table_lookup(seg_start, cls) + rank

    tile_first = jnp.arange(n_tiles, dtype=jnp.int32) * tile
    tile_class = jnp.minimum(jnp.sum((seg_end[None, :] <= tile_first[:, None]).astype(jnp.int32), axis=1),
                             N_CLASSES - 1)
    off = tile_first - _table_lookup(seg_start, tile_class)
    tile_cnt = jnp.clip(_table_lookup(counts, tile_class) - off, 0, tile)
    tile_src = jnp.clip(_table_lookup(src_start, tile_class) + off, 0, T - 1)
    grp = tile_class // N_PAIRS
    pair = tile_class % N_PAIRS
    tile_lo = grp * EXPERTS_PER_GROUP + _table_lookup(jnp.asarray(PAIR_LO, jnp.int32), pair)
    tile_hi = grp * EXPERTS_PER_GROUP + _table_lookup(jnp.asarray(PAIR_HI, jnp.int32), pair)
    return order, tile_lo, tile_hi, tile_src, tile_cnt, tok_row


def _moe_pair_kernel(order_ref, lo_ref, hi_ref, src_ref, cnt_ref, hfx_hbm, wga_ref, wua_ref, wda_ref, wgb_ref,
                     wub_ref, wdb_ref, o_ref, buf_ref, sem_ref):
    i = pl.program_id(0)
    n = pl.num_programs(0)
    tile = o_ref.shape[0]
    n_tok = order_ref.shape[0]
    slot = i % 2

    def start(t, s):
        base = src_ref[t]
        _start_row_gather(hfx_hbm, lambda j: order_ref[jnp.minimum(base + j, n_tok - 1)], tile, buf_ref, s, sem_ref)

    @pl.when((i == 0) & (cnt_ref[0] > 0))
    def _():
        start(0, 0)

    @pl.when((cnt_ref[i] <= 0) & (i > 0) & (cnt_ref[jnp.maximum(i - 1, 0)] > 0))
    def _():
        _wait_row_gather(hfx_hbm, tile, buf_ref, slot, sem_ref)

    @pl.when(cnt_ref[i] > 0)
    def _():
        start(jnp.minimum(i + 1, n - 1), 1 - slot)
        _wait_row_gather(hfx_hbm, tile, buf_ref, slot, sem_ref)
        d_model = buf_ref.shape[2] - LANES
        x = buf_ref[slot, :, :d_model].astype(BF16)
        live = lax.broadcasted_iota(jnp.int32, (tile, 1), 0) < cnt_ref[i]
        gates = buf_ref[slot, :, d_model:]
        g_lo = jnp.where(live, gates[:, 0:1], 0.0)
        g_hi = jnp.where(live, gates[:, 1:2], 0.0)

        def hidden(wg_ref, wu_ref, gate):
            hg = jnp.dot(x, wg_ref[...], preferred_element_type=F32)
            hu = jnp.dot(x, wu_ref[...], preferred_element_type=F32)
            return (hg * jax.nn.sigmoid(hg) * hu * gate).astype(BF16)

        o_ref[...] = (jnp.dot(hidden(wga_ref, wua_ref, g_lo), wda_ref[...], preferred_element_type=F32)
                      + jnp.dot(hidden(wgb_ref, wub_ref, g_hi), wdb_ref[...], preferred_element_type=F32))

    @pl.when(cnt_ref[i] <= 0)
    def _():
        o_ref[...] = jnp.zeros_like(o_ref)


def _moe_pair_call(hfx, order, tile_lo, tile_hi, tile_src, tile_cnt, wg, wu, wd, tile):
    T, W = hfx.shape
    D = W - LANES
    F = wg.shape[-1]
    n_tiles = tile_lo.shape[0]
    lo = lambda i, o, tl, th, ts, tc: (tl[i], 0, 0)
    hi = lambda i, o, tl, th, ts, tc: (th[i], 0, 0)
    grid_spec = pltpu.PrefetchScalarGridSpec(
        num_scalar_prefetch=5,
        grid=(n_tiles,),
        in_specs=[
            pl.BlockSpec(memory_space=pl.ANY),
            pl.BlockSpec((None, D, F), lo), pl.BlockSpec((None, D, F), lo), pl.BlockSpec((None, F, D), lo),
            pl.BlockSpec((None, D, F), hi), pl.BlockSpec((None, D, F), hi), pl.BlockSpec((None, F, D), hi),
        ],
        out_specs=pl.BlockSpec((tile, D), lambda i, o, tl, th, ts, tc: (i, 0)),
        scratch_shapes=[pltpu.VMEM((2, tile, W), F32), pltpu.SemaphoreType.DMA((2,))],
    )
    return pl.pallas_call(
        _moe_pair_kernel,
        grid_spec=grid_spec,
        out_shape=jax.ShapeDtypeStruct((n_tiles * tile, D), F32),
        compiler_params=_cparams(1),
        name="moe_pair_grouped",
    )(order, tile_lo, tile_hi, tile_src, tile_cnt, hfx, wg, wu, wd, wg, wu, wd)


def _final_kernel(idx_ref, xm_ref, g2_ref, ys_hbm, g_ref, o_ref, buf_ref, sem_ref):
    x = xm_ref[...] + g2_ref[...] * _gathered_rows(idx_ref, ys_hbm, buf_ref, sem_ref, xm_ref.shape[0])
    o_ref[...] = x * lax.rsqrt(jnp.mean(x * x, axis=-1, keepdims=True) + EPS) * g_ref[...]
    _drain_gathered_rows(ys_hbm, buf_ref, sem_ref, xm_ref.shape[0])


def _final_call(x_mid, tok_row, g2, mod_map, ys, g, tm):
    T, D = x_mid.shape
    tok = lambda i, *_: (i, 0)
    grid_spec = pltpu.PrefetchScalarGridSpec(
        num_scalar_prefetch=1,
        grid=(T // tm,),
        in_specs=[pl.BlockSpec((tm, D), tok), pl.BlockSpec((None, 1, D), lambda i, *_: mod_map(i)),
                  pl.BlockSpec(memory_space=pl.ANY), pl.BlockSpec((1, D), lambda i, *_: (0, 0))],
        out_specs=pl.BlockSpec((tm, D), tok),
        scratch_shapes=[pltpu.VMEM((2, tm, ys.shape[1]), ys.dtype), pltpu.SemaphoreType.DMA((2,))],
    )
    return pl.pallas_call(
        _final_kernel,
        grid_spec=grid_spec,
        out_shape=jax.ShapeDtypeStruct((T, D), F32),
        compiler_params=_cparams(1),
        name="moe_residual_final_norm",
    )(tok_row, x_mid, g2, ys, g[None, :].astype(F32))


def kernel(x, c, ctx, c_ctx, ada_w, ada_b, norm1_g, norm2_g, w_in, hy_short_w, hy_short_b, hy_f_w1, hy_f_b1, hy_f_freq, hy_f_w2, hy_f_b2, hy_f_w3, hy_bias, gm_ln_g, gm_ln_b, gm_ws, gm_bs, cv_dw_w, cv_dw_b, cv_ln_g, cv_ln_b, mla_qa_norm, w_uq, mla_kva_norm, w_ukv, mix_norm_g, w_out, w_router, router_bias, exp_w_gate, exp_w_up, exp_w_down, final_norm_g):
    B, n_lat, D = x.shape
    n_ctx = ctx.shape[1]
    T_lat, T_ctx = B * n_lat, B * n_ctx
    tm_lat = min(512, n_lat)
    tm_ctx = min(256, n_ctx)
    assert n_lat % tm_lat == 0 and n_ctx % tm_ctx == 0 and n_lat % ATT_TILE == 0
    assert tm_lat % GM_CHUNK == 0 and tm_ctx % GM_CHUNK == 0 and T_lat % MOE_TILE == 0 and T_ctx % MOE_TILE == 0

    tps_lat, tps_ctx = n_lat // tm_lat, n_ctx // tm_ctx
    dft = {L: _half_dft(L) for L in {n_lat, n_ctx}}
    rope_lat = _rope_slot_tables(n_lat)
    rope_ctx = _identity_slot_tables(tm_ctx)
    wr_pad = jnp.pad(w_router.astype(F32), ((0, 0), (0, LANES - N_EXPERTS)))
    wr_hi = wr_pad.astype(BF16)
    wr_lo = (wr_pad - wr_hi.astype(F32)).astype(BF16)
    rbias = router_bias.astype(F32)[:, None]
    cond = jnp.concatenate([c, c_ctx[None, :]], axis=0)
    streams = {
        'lat': dict(x=x.reshape(T_lat, D), L=n_lat, tm=tm_lat, mod_map=lambda i: (i // tps_lat, 0, 0),
                    rope=rope_lat, rope_map=lambda i: (i % tps_lat, 0), row0=0, moe_in=None),
        'ctx': dict(x=ctx.reshape(T_ctx, D), L=n_ctx, tm=tm_ctx, mod_map=lambda i: (B, 0, 0),
                    rope=rope_ctx, rope_map=lambda i: (0, 0), row0=T_lat, moe_in=None),
    }

    for l in range(DEPTH):
        last = l == DEPTH - 1
        m = jax.nn.silu(cond) @ ada_w[l] + ada_b[l]
        sh1, sc1, g1, sh2, sc2, g2 = [t[:, None, :] for t in jnp.split(m, 6, axis=-1)]
        a1 = norm1_g[l][None, None, :] * (1.0 + sc1)
        a2 = norm2_g[l][None, None, :] * (1.0 + sc2)
        qg = jnp.pad(mla_qa_norm[l], (0, PQ_PAD - MLA_Q_RANK))[None, :].astype(F32)
        kvg = mla_kva_norm[l][None, :].astype(F32)
        win, wuq, wkv = _pack_w_in(w_in[l]), _pack_w_uq(w_uq[l]), _pack_w_ukv(w_ukv[l])
        filt = (hy_f_w1[l], hy_f_b1[l], hy_f_freq[l], hy_f_w2[l], hy_f_b2[l], hy_f_w3[l])
        wg, wu, wd = exp_w_gate[l].astype(BF16), exp_w_up[l].astype(BF16), exp_w_down[l].astype(BF16)

        proj = {name: _proj_call(s['x'], a1, sh1, s['mod_map'], win, qg, wuq, kvg, wkv, s['rope'], s['rope_map'],
                                 s['tm'], s['moe_in']) for name, s in streams.items()}
        for name, s in streams.items():
            if s['moe_in'] is not None:
                s['x'] = proj[name][6]
        k_ctx, v_ctx = proj['ctx'][4], proj['ctx'][5]

        active = ('lat',) if last else ('lat', 'ctx')
        x_mid, cls, rank = {}, {}, {}
        t_moe = T_lat if last else T_lat + T_ctx
        hfx = None if last else jnp.zeros((t_moe, D + LANES), F32)
        counts = jnp.zeros((CLASS_ROWS, LANES), F32)
        for name in active:
            s = streams[name]
            p_hy, p_gm, p_cv, q, k, v = proj[name][:6]
            if name == 'lat':
                y_at = _attn_lat_call(q, k, v, k_ctx, v_ctx, B, n_lat, n_ctx)
            else:
                y_at = _attn_ctx_call(q, k, v, B, n_ctx)
            filters = _hyena_filters(s['L'], *filt)
            y_hy_t = _hyena_call(p_hy, hy_short_w[l], hy_short_b[l], filters, hy_bias[l], B, s['L'], dft[s['L']])
            y_gm, y_cv = _mixers_call(p_gm, p_cv, gm_ln_g[l], gm_ln_b[l], gm_ws[l], gm_bs[l], cv_dw_w[l],
                                      cv_dw_b[l], cv_ln_g[l], cv_ln_b[l], s['L'], s['tm'])
            x_mid[name], hfx, cls[name], rank[name], counts = _out_call(
                y_hy_t, y_gm, y_cv, y_at, s['x'], mix_norm_g[l][None, :].astype(F32), w_out[l].astype(BF16),
                g1, a2, sh2, s['mod_map'], wr_hi, wr_lo, rbias, counts, s['L'], s['tm'],
                hfx, t_moe, s['row0'])

        cat = lambda d: jnp.concatenate([d[name][0] for name in active], axis=0)
        order, tile_lo, tile_hi, tile_src, tile_cnt, tok_row = _class_plan(
            cat(cls), cat(rank), counts[:N_CLASSES, 0].astype(jnp.int32), MOE_TILE)
        ys = _moe_pair_call(hfx, order, tile_lo, tile_hi, tile_src, tile_cnt, wg, wu, wd, MOE_TILE)
        for name in active:
            s = streams[name]
            s['x'] = x_mid[name]
            s['moe_in'] = (lax.slice_in_dim(tok_row, s['row0'], s['row0'] + x_mid[name].shape[0]), g2, ys)

    s = streams['lat']
    tok_row, g2, ys = s['moe_in']
    return _final_call(s['x'], tok_row, g2, s['mod_map'], ys, final_norm_g, s['tm']).reshape(B, n_lat, D)
```

```python
import functools
import math

import jax
import jax.numpy as jnp
import numpy as np
from jax import lax
from jax.experimental import pallas as pl
from jax.experimental.pallas import tpu as pltpu

F32 = jnp.float32
BF16 = jnp.bfloat16

D_MODEL = 1024
DEPTH = 2
GRID_W = 64
EPS = 1e-6

D_GROUP = 256
N_MIXERS = 4
HY_ORDER = 2
HY_SHORT = 3
HY_EMB = 33
HY_BANDS = (HY_EMB - 1) // 2
HY_TARGET = 1e-2
HY_FAST_PCT = 0.3
HY_SLOW_PCT = 1.5
GM_CHUNK = 128
GM_HEADS = 4
CV_WIDTH = 31
CV_GROUPS = 4
MLA_HEADS = 4
MLA_NOPE = 64
MLA_ROPE = 32
MLA_V = 64
MLA_Q_RANK = 192
MLA_KV_RANK = 128
ROPE_BASE = 10000.0
N_EXPERTS = 16
N_EXPERT_GROUPS = 4
EXPERTS_PER_GROUP = N_EXPERTS // N_EXPERT_GROUPS
TOP_K = 2
D_EXPERT = 512

HY_COLS = (HY_ORDER + 1) * D_GROUP
GM_COLS = 2 * D_GROUP
CV_COLS = 2 * D_GROUP
MQ_COLS = MLA_Q_RANK
MKV_COLS = MLA_KV_RANK + MLA_ROPE
HY_OFF = 0
GM_OFF = HY_OFF + HY_COLS
CV_OFF = GM_OFF + GM_COLS
MQ_OFF = CV_OFF + CV_COLS
MKV_OFF = MQ_OFF + MQ_COLS
IN_COLS = MKV_OFF + MKV_COLS

LANES = 128
SUBLANES = 8
HEAD_SLOT = LANES
QK_COLS = MLA_HEADS * HEAD_SLOT
V_COLS = MLA_HEADS * MLA_V
PQ_OFF = MQ_OFF
PQ_PAD = 256
PKV_OFF = PQ_OFF + PQ_PAD
PKR_OFF = PKV_OFF + MLA_KV_RANK
PROJ_COLS = PKR_OFF + LANES

PAIR_LO = (0, 0, 0, 1, 1, 2)
PAIR_HI = (1, 2, 3, 2, 3, 3)
N_PAIRS = len(PAIR_LO)
N_CLASSES = N_EXPERT_GROUPS * N_PAIRS
CLASS_ROWS = -(-N_CLASSES // SUBLANES) * SUBLANES

ATT_TILE = 512
ATT_SUB = 256
MOE_TILE = 256
DFT_ROWS = 256
DFT_COLS = 1024
CONV_HALO = 16
VMEM_LIMIT = 56 * 1024 * 1024


def _cparams(n_axes):
    return pltpu.CompilerParams(dimension_semantics=("arbitrary",) * n_axes, vmem_limit_bytes=VMEM_LIMIT)


def _axial_rope_tables(n_lat):
    rows = n_lat // GRID_W
    row = jnp.repeat(jnp.arange(rows), GRID_W).astype(F32)
    col = jnp.tile(jnp.arange(GRID_W), rows).astype(F32)
    n_freq = MLA_ROPE // 4
    inv = ROPE_BASE ** (-jnp.arange(n_freq, dtype=F32) / n_freq)
    ang = jnp.concatenate([row[:, None] * inv, col[:, None] * inv], axis=-1)
    return jnp.cos(ang), jnp.sin(ang)


def _hyena_filters(L, w1, b1, freq, w2, b2, w3):
    w1, b1, freq, w2, b2, w3 = (a.astype(F32) for a in (w1, b1, freq, w2, b2, w3))
    t_all = jnp.linspace(0.0, 1.0, L, dtype=F32)
    f = jnp.linspace(1e-4, HY_BANDS - 1, HY_BANDS, dtype=F32)[None, :]
    deltas = jnp.abs(jnp.linspace(math.log(HY_TARGET) / HY_SLOW_PCT,
                                  math.log(HY_TARGET) / HY_FAST_PCT, D_GROUP, dtype=F32))

    def taps(pos, direction):
        t = t_all[pos][:, None]
        w = 2.0 * math.pi * pos.astype(F32)[:, None] / L
        z = jnp.concatenate([t, jnp.cos(f * w), -jnp.sin(f * w)], axis=-1)
        h = jnp.sin(freq * (z @ w1 + b1))
        h = jnp.sin(freq * (h @ w2 + b2))
        h = (h @ w3).reshape(L, HY_ORDER, 2, D_GROUP)[:, :, direction]
        return h * jnp.exp(-t * deltas)[:, None, :]

    m = jnp.arange(L, dtype=jnp.int32)
    kf = taps(m, 0)
    kb = jnp.where((m > 0)[:, None, None], taps((L - m) % L, 1), 0.0)
    norm = jnp.sum(jnp.abs(kf), axis=0, keepdims=True) + jnp.sum(jnp.abs(kb), axis=0, keepdims=True)
    w_cols = HY_ORDER * D_GROUP
    return jnp.concatenate([(kf / norm).reshape(L, w_cols), (kb / norm).reshape(L, w_cols)], axis=1)


def _pack_w_in(w_in):
    D = w_in.shape[0]
    z = lambda n: jnp.zeros((D, n), w_in.dtype)
    return jnp.concatenate([
        w_in[:, :MQ_OFF],
        w_in[:, MQ_OFF:MKV_OFF], z(PQ_PAD - MQ_COLS),
        w_in[:, MKV_OFF:MKV_OFF + MLA_KV_RANK],
        w_in[:, MKV_OFF + MLA_KV_RANK:], z(LANES - MLA_ROPE),
    ], axis=1).astype(BF16)


def _pack_w_uq(w_uq):
    w = w_uq.reshape(MLA_Q_RANK, MLA_HEADS, MLA_NOPE + MLA_ROPE)
    w = jnp.pad(w, ((0, PQ_PAD - MLA_Q_RANK), (0, 0), (0, HEAD_SLOT - MLA_NOPE - MLA_ROPE)))
    return w.reshape(PQ_PAD, QK_COLS).astype(BF16)


def _pack_w_ukv(w_ukv):
    w = w_ukv.reshape(MLA_KV_RANK, MLA_HEADS, MLA_NOPE + MLA_V)
    k_part = jnp.pad(w[:, :, :MLA_NOPE], ((0, 0), (0, 0), (0, HEAD_SLOT - MLA_NOPE))).reshape(MLA_KV_RANK, QK_COLS)
    v_part = w[:, :, MLA_NOPE:].reshape(MLA_KV_RANK, V_COLS)
    top = jnp.concatenate([k_part, v_part], axis=1)
    eye = jnp.eye(LANES, dtype=w_ukv.dtype)[:, :MLA_ROPE]
    place = jnp.pad(eye, ((0, 0), (MLA_NOPE, HEAD_SLOT - MLA_NOPE - MLA_ROPE)))
    bot = jnp.concatenate([jnp.tile(place, (1, MLA_HEADS)), jnp.zeros((LANES, V_COLS), w_ukv.dtype)], axis=1)
    return jnp.concatenate([top, bot], axis=0).astype(BF16)


def _rope_slot_tables(n_lat):
    cos, sin = _axial_rope_tables(n_lat)
    half = MLA_ROPE // 2
    tail_w = HEAD_SLOT - MLA_NOPE - MLA_ROPE
    ones = lambda n: jnp.ones((n_lat, n), F32)
    zeros = lambda n: jnp.zeros((n_lat, n), F32)
    cf = jnp.concatenate([ones(MLA_NOPE), cos, cos, ones(tail_w)], axis=1)
    s_up = jnp.concatenate([zeros(MLA_NOPE + half), sin, zeros(tail_w)], axis=1)
    s_dn = jnp.concatenate([zeros(MLA_NOPE), -sin, zeros(half + tail_w)], axis=1)
    return cf, s_up, s_dn


def _identity_slot_tables(rows):
    return (jnp.ones((rows, HEAD_SLOT), F32), jnp.zeros((rows, HEAD_SLOT), F32), jnp.zeros((rows, HEAD_SLOT), F32))


def _rotate_slots(t, cf, s_up, s_dn):
    half = MLA_ROPE // 2
    outs = []
    for h in range(MLA_HEADS):
        tb = t[:, h * HEAD_SLOT:(h + 1) * HEAD_SLOT]
        outs.append(tb * cf + pltpu.roll(tb, half, 1) * s_up + pltpu.roll(tb, HEAD_SLOT - half, 1) * s_dn)
    return jnp.concatenate(outs, axis=1)


def _row_gather_copy(src_hbm, row, buf_ref, slot, j, sem_ref):
    return pltpu.make_async_copy(src_hbm.at[pl.ds(row, 1)], buf_ref.at[slot, pl.ds(j, 1)], sem_ref.at[slot])


def _start_row_gather(src_hbm, row_of, n_rows, buf_ref, slot, sem_ref):
    for j in range(n_rows):
        _row_gather_copy(src_hbm, row_of(j), buf_ref, slot, j, sem_ref).start()


def _wait_row_gather(src_hbm, n_rows, buf_ref, slot, sem_ref):
    pltpu.make_async_copy(src_hbm.at[pl.ds(0, n_rows)], buf_ref.at[slot], sem_ref.at[slot]).wait()


def _gathered_rows(idx_ref, ys_hbm, buf_ref, sem_ref, tm):
    i = pl.program_id(0)
    n = pl.num_programs(0)
    slot = i % 2

    @pl.when(i == 0)
    def _():
        _start_row_gather(ys_hbm, lambda j: idx_ref[j], tm, buf_ref, 0, sem_ref)

    nxt = jnp.minimum(i + 1, n - 1)
    _start_row_gather(ys_hbm, lambda j: idx_ref[nxt * tm + j], tm, buf_ref, 1 - slot, sem_ref)
    _wait_row_gather(ys_hbm, tm, buf_ref, slot, sem_ref)
    return buf_ref[slot]


def _drain_gathered_rows(ys_hbm, buf_ref, sem_ref, tm):
    i = pl.program_id(0)

    @pl.when(i == pl.num_programs(0) - 1)
    def _():
        _wait_row_gather(ys_hbm, tm, buf_ref, 1 - i % 2, sem_ref)


def _proj_gather_kernel(idx_ref, xm_ref, g2_ref, ys_hbm, a_ref, sh_ref, win_ref, qg_ref, wuq_ref, kvg_ref, wkv_ref,
                        cf_ref, su_ref, sd_ref, hy_ref, gm_ref, cv_ref, q_ref, k_ref, v_ref, xn_ref, buf_ref, sem_ref):
    y = _gathered_rows(idx_ref, ys_hbm, buf_ref, sem_ref, xm_ref.shape[0])
    xn_ref[...] = xm_ref[...] + g2_ref[...] * y
    _proj_kernel(xn_ref, a_ref, sh_ref, win_ref, qg_ref, wuq_ref, kvg_ref, wkv_ref, cf_ref, su_ref, sd_ref,
                 hy_ref, gm_ref, cv_ref, q_ref, k_ref, v_ref)
    _drain_gathered_rows(ys_hbm, buf_ref, sem_ref, xm_ref.shape[0])


def _proj_kernel(x_ref, a_ref, sh_ref, win_ref, qg_ref, wuq_ref, kvg_ref, wkv_ref, cf_ref, su_ref, sd_ref,
                 hy_ref, gm_ref, cv_ref, q_ref, k_ref, v_ref):
    x = x_ref[...]
    ms = jnp.mean(x * x, axis=-1, keepdims=True)
    h = x * lax.rsqrt(ms + EPS) * a_ref[...] + sh_ref[...]
    p = jnp.dot(h.astype(BF16), win_ref[...], preferred_element_type=F32)
    hy_ref[...] = p[:, HY_OFF:GM_OFF]
    gm_ref[...] = p[:, GM_OFF:CV_OFF]
    cv_ref[...] = p[:, CV_OFF:MQ_OFF]

    cf, su, sd = cf_ref[...], su_ref[...], sd_ref[...]
    cq = p[:, PQ_OFF:PQ_OFF + PQ_PAD]
    qn = cq * lax.rsqrt(jnp.sum(cq * cq, axis=-1, keepdims=True) * (1.0 / MLA_Q_RANK) + EPS) * qg_ref[...]
    q = jnp.dot(qn.astype(BF16), wuq_ref[...], preferred_element_type=F32)
    q = _rotate_slots(q, cf, su, sd) * (math.log2(math.e) / math.sqrt(MLA_NOPE + MLA_ROPE))
    q_ref[...] = q.astype(BF16)

    ckv = p[:, PKV_OFF:PKV_OFF + MLA_KV_RANK]
    kvn = ckv * lax.rsqrt(jnp.mean(ckv * ckv, axis=-1, keepdims=True) + EPS) * kvg_ref[...]
    kin = jnp.concatenate([kvn, p[:, PKR_OFF:PKR_OFF + LANES]], axis=1).astype(BF16)
    kv = jnp.dot(kin, wkv_ref[...], preferred_element_type=F32)
    k_ref[...] = _rotate_slots(kv[:, :QK_COLS], cf, su, sd).astype(BF16)
    v_ref[...] = kv[:, QK_COLS:].astype(BF16)


def _proj_call(xt, mod_a, mod_sh, mod_map, win, qg, wuq, kvg, wkv, rope_tabs, rope_map, tm, moe_in=None):
    T, D = xt.shape
    const = lambda i, *_: (0, 0)
    tok = lambda i, *_: (i, 0)
    mod = lambda i, *_: mod_map(i)
    rope = lambda i, *_: rope_map(i)
    out_cols = (HY_COLS, GM_COLS, CV_COLS, QK_COLS, QK_COLS, V_COLS)
    out_dtypes = (F32, F32, F32, BF16, BF16, BF16)
    in_specs = [
        pl.BlockSpec((None, 1, D), mod),
        pl.BlockSpec((None, 1, D), mod),
        pl.BlockSpec(win.shape, const),
        pl.BlockSpec(qg.shape, const),
        pl.BlockSpec(wuq.shape, const),
        pl.BlockSpec(kvg.shape, const),
        pl.BlockSpec(wkv.shape, const),
        pl.BlockSpec((tm, HEAD_SLOT), rope),
        pl.BlockSpec((tm, HEAD_SLOT), rope),
        pl.BlockSpec((tm, HEAD_SLOT), rope),
    ]
    out_specs = [pl.BlockSpec((tm, n), tok) for n in out_cols]
    out_shape = [jax.ShapeDtypeStruct((T, n), dt) for n, dt in zip(out_cols, out_dtypes)]
    shared = (mod_a, mod_sh, win, qg, wuq, kvg, wkv, *rope_tabs)
    if moe_in is None:
        return pl.pallas_call(
            _proj_kernel,
            grid=(T // tm,),
            in_specs=[pl.BlockSpec((tm, D), tok)] + in_specs,
            out_specs=out_specs,
            out_shape=out_shape,
            compiler_params=_cparams(1),
            name="proj_qkv",
        )(xt, *shared)
    tok_row, g2, ys = moe_in
    grid_spec = pltpu.PrefetchScalarGridSpec(
        num_scalar_prefetch=1,
        grid=(T // tm,),
        in_specs=[pl.BlockSpec((tm, D), tok), pl.BlockSpec((None, 1, D), mod),
                  pl.BlockSpec(memory_space=pl.ANY)] + in_specs,
        out_specs=out_specs + [pl.BlockSpec((tm, D), tok)],
        scratch_shapes=[pltpu.VMEM((2, tm, ys.shape[1]), ys.dtype), pltpu.SemaphoreType.DMA((2,))],
    )
    return pl.pallas_call(
        _proj_gather_kernel,
        grid_spec=grid_spec,
        out_shape=out_shape + [jax.ShapeDtypeStruct((T, D), F32)],
        compiler_params=_cparams(1),
        name="moe_residual_proj_qkv",
    )(tok_row, xt, g2, ys, *shared)


def _attend_heads(q_ref, key_refs, val_refs, o_ref):
    nt = (((1,), (1,)), ((), ()))
    lane = lax.broadcasted_iota(jnp.int32, (1, V_COLS), 1)
    vals = [v_ref[...] for v_ref in val_refs]
    sub = min(ATT_SUB, q_ref.shape[0])
    for r in range(q_ref.shape[0] // sub):
        rows = slice(r * sub, (r + 1) * sub)
        acc = jnp.zeros((sub, V_COLS), F32)
        for h in range(MLA_HEADS):
            sl_h = slice(h * HEAD_SLOT, (h + 1) * HEAD_SLOT)
            q = q_ref[rows, sl_h]
            scores = [lax.dot_general(q, k_ref[:, sl_h], nt, preferred_element_type=F32) for k_ref in key_refs]
            m = functools.reduce(jnp.maximum, [jnp.max(s, axis=-1, keepdims=True) for s in scores])
            probs = [jnp.exp2(s - m) for s in scores]
            denom = functools.reduce(jnp.add, [jnp.sum(p, axis=-1, keepdims=True) for p in probs])
            o = functools.reduce(jnp.add, [jnp.dot(p.astype(BF16), v, preferred_element_type=F32)
                                           for p, v in zip(probs, vals)])
            in_head = (lane >= h * MLA_V) & (lane < (h + 1) * MLA_V)
            acc = acc + jnp.where(in_head, o / denom, 0.0)
        o_ref[rows, :] = acc


def _attn_lat_kernel(q_ref, kl_ref, kc_ref, vl_ref, vc_ref, o_ref):
    _attend_heads(q_ref, (kl_ref, kc_ref), (vl_ref, vc_ref), o_ref)


def _attn_ctx_kernel(q_ref, k_ref, v_ref, o_ref):
    _attend_heads(q_ref, (k_ref,), (v_ref,), o_ref)


def _attn_lat_call(q, k, v, k_ctx, v_ctx, n_batch, n_lat, n_ctx):
    tq = ATT_TILE
    qt = n_lat // tq
    return pl.pallas_call(
        _attn_lat_kernel,
        grid=(n_batch, qt),
        in_specs=[
            pl.BlockSpec((tq, QK_COLS), lambda b, j: (b * qt + j, 0)),
            pl.BlockSpec((n_lat, QK_COLS), lambda b, j: (b, 0)),
            pl.BlockSpec((n_ctx, QK_COLS), lambda b, j: (b, 0)),
            pl.BlockSpec((n_lat, V_COLS), lambda b, j: (b, 0)),
            pl.BlockSpec((n_ctx, V_COLS), lambda b, j: (b, 0)),
        ],
        out_specs=pl.BlockSpec((tq, V_COLS), lambda b, j: (b * qt + j, 0)),
        out_shape=jax.ShapeDtypeStruct((n_batch * n_lat, V_COLS), F32),
        compiler_params=_cparams(2),
        name="attn_latent",
    )(q, k, k_ctx, v, v_ctx)


def _attn_ctx_call(q, k, v, n_batch, n_ctx):
    blk = lambda b: (b, 0)
    return pl.pallas_call(
        _attn_ctx_kernel,
        grid=(n_batch,),
        in_specs=[pl.BlockSpec((n_ctx, QK_COLS), blk), pl.BlockSpec((n_ctx, QK_COLS), blk),
                  pl.BlockSpec((n_ctx, V_COLS), blk)],
        out_specs=pl.BlockSpec((n_ctx, V_COLS), blk),
        out_shape=jax.ShapeDtypeStruct((n_batch * n_ctx, V_COLS), F32),
        compiler_params=_cparams(1),
        name="attn_context",
    )(q, k, v)


@functools.lru_cache(maxsize=None)
def _dft_factor_tables_np(L):
    f = np.arange(L, dtype=np.int64)[:, None]
    def trig(t):
        ang = ((f * t[None, :]) % (2 * L)).astype(np.float64) * (np.pi / L)
        return np.cos(ang).astype(np.float32), np.sin(ang).astype(np.float32)
    return trig(np.arange(L // LANES, dtype=np.int64) * LANES) + trig(np.arange(LANES, dtype=np.int64))


def _dft_matrices(L):
    assert L % LANES == 0
    c1, s1, c0, s0 = (jnp.asarray(a) for a in _dft_factor_tables_np(L))
    c1, s1, c0, s0 = c1[:, :, None], s1[:, :, None], c0[:, None, :], s0[:, None, :]
    c = (c1 * c0 - s1 * s0).reshape(L, L)
    s = (s1 * c0 + c1 * s0).reshape(L, L)
    alt = jnp.where(jnp.arange(L) % 2 == 0, 1.0, -1.0).astype(F32)
    row0 = (jnp.arange(L) == 0)
    s_fwd = jnp.where(row0[:, None], alt[None, :], s)
    s_inv = jnp.where(row0[None, :], alt[:, None], s)
    return c.astype(BF16), s_fwd.astype(BF16), s_inv.astype(BF16)


def _half_dft(L):
    assert L % 2 == 0
    return _dft_matrices(L // 2) + tuple(jnp.asarray(t) for t in _twiddle_np(L // 2))


def _half_butterfly(ae, ao, be, bo, c, s):
    return (ae + c * ao - s * bo, be + c * bo + s * ao,
            ae - c * ao + s * bo, -be + c * bo + s * ao)


def _twiddle_np(M):
    ang = np.arange(M, dtype=np.float64)[:, None] * (np.pi / (2 * M))
    return np.cos(ang).astype(np.float32), np.sin(ang).astype(np.float32)


def _spectrum_kernel(c_ref, s_ref, k_ref, tc_ref, ts_ref, krl_ref, kil_ref, krh_ref, kih_ref, sp_ref):
    tr, M = c_ref.shape
    w = k_ref.shape[1] // 4
    inv_l = 1.0 / (2 * M)
    k = k_ref[...]
    a = jnp.dot(c_ref[...], k, preferred_element_type=F32)
    b = jnp.dot(s_ref[...], k, preferred_element_type=F32)
    g = pl.program_id(0) * tr + lax.broadcasted_iota(jnp.int32, (tr, 1), 0)
    sign = jnp.where(g % 2 == 0, 1.0, -1.0)
    tc, ts = tc_ref[...], ts_ref[...]
    cols = lambda x, j: x[:, j * w:(j + 1) * w]
    f_lo_a, f_lo_b, f_hi_a, f_hi_b = _half_butterfly(cols(a, 0), cols(a, 2), cols(b, 0), cols(b, 2), tc, ts)
    b_lo_a, b_lo_b, b_hi_a, b_hi_b = _half_butterfly(cols(a, 1), cols(a, 3), cols(b, 1), cols(b, 3), tc, ts)
    kr_lo = (f_lo_a + sign * b_lo_a) * inv_l
    kr_hi = (f_hi_a + sign * b_hi_a) * inv_l
    krl_ref[...] = kr_lo
    kil_ref[...] = -(f_lo_b + sign * b_lo_b) * inv_l
    krh_ref[...] = kr_hi
    kih_ref[...] = -(f_hi_b + sign * b_hi_b) * inv_l

    @pl.when(pl.program_id(0) == 0)
    def _():
        mid_sign = 1.0 if M % 2 == 0 else -1.0
        sp_ref[...] = jnp.zeros_like(sp_ref)
        sp_ref[0:1, :] = 0.5 * kr_lo[0:1, :]
        sp_ref[1:2, :] = 0.5 * kr_hi[0:1, :]
        sp_ref[2:3, :] = (cols(b, 0)[0:1, :] + mid_sign * cols(b, 1)[0:1, :]) * inv_l
        sp_ref[3:4, :] = -(cols(b, 2)[0:1, :] + mid_sign * cols(b, 3)[0:1, :]) * inv_l


def _spectrum_call(dft, filters):
    cmat, smat, _, tc, ts = dft
    M = cmat.shape[0]
    w2 = filters.shape[1]
    w = w2 // 2
    samples = filters.reshape(M, 2 * w2).astype(BF16)
    tr = min(DFT_ROWS, M)
    row = lambda i: (i, 0)
    return pl.pallas_call(
        _spectrum_kernel,
        grid=(M // tr,),
        in_specs=[pl.BlockSpec((tr, M), row), pl.BlockSpec((tr, M), row),
                  pl.BlockSpec((M, 2 * w2), lambda i: (0, 0), pipeline_mode=pl.Buffered(1)),
                  pl.BlockSpec((tr, 1), row), pl.BlockSpec((tr, 1), row)],
        out_specs=[pl.BlockSpec((tr, w), row)] * 4 + [pl.BlockSpec((SUBLANES, w), lambda i: (0, 0))],
        out_shape=[jax.ShapeDtypeStruct((M, w), F32)] * 4 + [jax.ShapeDtypeStruct((SUBLANES, w), F32)],
        compiler_params=_cparams(1),
        name="hyena_filter_spectrum",
    )(cmat, smat, samples, tc, ts)


def _hy_prep_kernel(p_ref, w_ref, b_ref, z_ref):
    M = z_ref.shape[1]
    pe = p_ref[pl.ds(0, M, stride=2), :]
    po = p_ref[pl.ds(1, M, stride=2), :]
    row = lax.broadcasted_iota(jnp.int32, (M, 1), 0)
    po_prev = jnp.where(row == 0, 0.0, pltpu.roll(po, 1, 0))
    pe_next = jnp.where(row == M - 1, 0.0, pltpu.roll(pe, M - 1, 0))
    w0, w1, w2, bias = w_ref[0:1, :], w_ref[1:2, :], w_ref[2:3, :], b_ref[...]
    ze = po_prev * w0 + pe * w1 + po * w2 + bias
    zo = pe * w0 + po * w1 + pe_next * w2 + bias
    z_ref[0] = ze.astype(z_ref.dtype)
    z_ref[1] = zo.astype(z_ref.dtype)


def _hy_prep_call(p_hy, short_w, short_b, n_batch, L):
    N = n_batch * D_GROUP
    M = L // 2
    hb = D_GROUP // LANES
    return pl.pallas_call(
        _hy_prep_kernel,
        grid=(n_batch, (HY_ORDER + 1) * hb),
        in_specs=[
            pl.BlockSpec((L, LANES), lambda b, j: (b, j)),
            pl.BlockSpec((HY_SHORT, LANES), lambda b, j: (0, j)),
            pl.BlockSpec((1, LANES), lambda b, j: (0, j)),
        ],
        out_specs=pl.BlockSpec((None, 2, M, LANES), lambda b, j: (j // hb, 0, 0, b * hb + j % hb)),
        out_shape=jax.ShapeDtypeStruct((HY_ORDER + 1, 2, M, N), BF16),
        compiler_params=_cparams(2),
        name="hyena_short_conv",
    )(p_hy, short_w, short_b[None, :])


def _dft_fwd_kernel(c_ref, s_ref, ue_ref, uo_ref, tc_ref, ts_ref, krl_ref, kil_ref, krh_ref, kih_ref, sp_ref,
                    e_ref):
    tr = c_ref.shape[0]
    c, s = c_ref[...], s_ref[...]
    ue, uo = ue_ref[...], uo_ref[...]
    ae = jnp.dot(c, ue, preferred_element_type=F32)
    ao = jnp.dot(c, uo, preferred_element_type=F32)
    be = jnp.dot(s, ue, preferred_element_type=F32)
    bo = jnp.dot(s, uo, preferred_element_type=F32)
    tc, ts = tc_ref[...], ts_ref[...]
    first = (pl.program_id(1) * tr + lax.broadcasted_iota(jnp.int32, (tr, 1), 0)) == 0
    krl, kil, krh, kih = krl_ref[...], kil_ref[...], krh_ref[...], kih_ref[...]
    k0h, kLh, krm, kim = sp_ref[0:1, :], sp_ref[1:2, :], sp_ref[2:3, :], sp_ref[3:4, :]
    for g in range(ue.shape[1] // D_GROUP):
        sl = slice(g * D_GROUP, (g + 1) * D_GROUP)
        a1, b1, a2, b2 = _half_butterfly(ae[:, sl], ao[:, sl], be[:, sl], bo[:, sl], tc, ts)
        p1, q1 = krl * a1 + kil * b1, krl * b1 - kil * a1
        p2, q2 = krh * a2 + kih * b2, krh * b2 - kih * a2
        pm, qp = p1 - p2, q1 + q2
        dc, ny = k0h * a1, kLh * a2
        alt_e, alt_o = be[:, sl], bo[:, sl]
        e_ref[0, :, sl] = jnp.where(first, dc + ny, p1 + p2).astype(BF16)
        e_ref[1, :, sl] = jnp.where(first, krm * alt_e + kim * alt_o, q1 - q2).astype(BF16)
        e_ref[2, :, sl] = jnp.where(first, dc - ny, tc * pm + ts * qp).astype(BF16)
        e_ref[3, :, sl] = jnp.where(first, krm * alt_o - kim * alt_e, tc * qp - ts * pm).astype(BF16)


def _dft_fwd_call(dft, u_arr, u_sel, tabs, order):
    cmat, smat, _, tc, ts = dft
    _, _, M, N = u_arr.shape
    tr = min(DFT_ROWS, M)
    tn = min(DFT_COLS, N)
    row = lambda h, i: (i, 0)
    tab = lambda h, i: (i, order)
    u16 = u_arr
    return pl.pallas_call(
        _dft_fwd_kernel,
        grid=(N // tn, M // tr),
        in_specs=[
            pl.BlockSpec((tr, M), row),
            pl.BlockSpec((tr, M), row),
            pl.BlockSpec((None, None, M, tn), lambda h, i: (u_sel, 0, 0, h), pipeline_mode=pl.Buffered(1)),
            pl.BlockSpec((None, None, M, tn), lambda h, i: (u_sel, 1, 0, h), pipeline_mode=pl.Buffered(1)),
            pl.BlockSpec((tr, 1), row), pl.BlockSpec((tr, 1), row),
        ] + [pl.BlockSpec((tr, D_GROUP), tab)] * 4 + [pl.BlockSpec((SUBLANES, D_GROUP), lambda h, i: (0, order))],
        out_specs=pl.BlockSpec((4, tr, tn), lambda h, i: (0, i, h)),
        out_shape=jax.ShapeDtypeStruct((4, M, N), BF16),
        compiler_params=_cparams(2),
        name="hyena_dft_fwd",
    )(cmat, smat, u16, u16, tc, ts, *tabs)


def _dft_inv_kernel(c_ref, st_ref, ec_ref, es_ref, oc_ref, os_ref, u_ref, g_ref, bias_ref, y_ref):
    c, st = c_ref[...], st_ref[...]
    bias = bias_ref[...]
    conv_e = (jnp.dot(c, ec_ref[...], preferred_element_type=F32)
              + jnp.dot(st, es_ref[...], preferred_element_type=F32))
    conv_o = (jnp.dot(c, oc_ref[...], preferred_element_type=F32)
              + jnp.dot(st, os_ref[...], preferred_element_type=F32))
    for par, conv in enumerate((conv_e, conv_o)):
        y = g_ref[par].astype(F32) * (conv + u_ref[par].astype(F32) * bias)
        y_ref[par] = y.astype(y_ref.dtype)


def _dft_inv_call(dft, e16, u_arr, u_sel, g_arr, g_sel, bias_row, out_dtype):
    cmat, _, stmat, _, _ = dft
    _, M, N = e16.shape
    tr = min(DFT_ROWS, M)
    tn = min(DFT_COLS, N)
    row = lambda h, i: (i, 0)
    plane = lambda k: pl.BlockSpec((None, M, tn), lambda h, i: (k, 0, h), pipeline_mode=pl.Buffered(1))
    return pl.pallas_call(
        _dft_inv_kernel,
        grid=(N // tn, M // tr),
        in_specs=[
            pl.BlockSpec((tr, M), row),
            pl.BlockSpec((tr, M), row),
            plane(0), plane(1), plane(2), plane(3),
            pl.BlockSpec((None, 2, tr, tn), lambda h, i: (u_sel, 0, i, h)),
            pl.BlockSpec((None, 2, tr, tn), lambda h, i: (g_sel, 0, i, h)),
            pl.BlockSpec((1, tn), lambda h, i: (0, h)),
        ],
        out_specs=pl.BlockSpec((2, tr, tn), lambda h, i: (0, i, h)),
        out_shape=jax.ShapeDtypeStruct((2, M, N), out_dtype),
        compiler_params=_cparams(2),
        name="hyena_dft_inv",
    )(cmat, stmat, e16, e16, e16, e16, u_arr, g_arr, bias_row)


def _hyena_call(p_hy, short_w, short_b, filters, hy_bias, n_batch, L, dft):
    tabs = _spectrum_call(dft, filters)
    z = _hy_prep_call(p_hy, short_w, short_b, n_batch, L)
    u_arr, u_sel = z, HY_ORDER
    for n in range(HY_ORDER):
        e16 = _dft_fwd_call(dft, u_arr, u_sel, tabs, n)
        bias_row = jnp.tile(hy_bias[n][None, :].astype(F32), (1, n_batch))
        y = _dft_inv_call(dft, e16, u_arr, u_sel, z, n, bias_row, BF16 if n + 1 < HY_ORDER else F32)
        u_arr, u_sel = y[None], 0
    return y


def _mixers_kernel(gm_ref, cv_ref, cvp_ref, cvn_ref, lng_ref, lnb_ref, ws_ref, bsf_ref, dww_ref, dwb_ref,
                   cg_ref, cb_ref, avg_ref, ygm_ref, ycv_ref, glu_ref, *, tiles_per_seq):
    i = pl.program_id(0)
    tm = gm_ref.shape[0]
    lane = lax.broadcasted_iota(jnp.int32, (1, D_GROUP), 1)

    z = jax.nn.gelu(gm_ref[...], approximate=True)
    u, v = z[:, :D_GROUP], z[:, D_GROUP:]
    mu = jnp.mean(v, axis=-1, keepdims=True)
    vc = v - mu
    var = jnp.mean(vc * vc, axis=-1, keepdims=True)
    vn = (vc * lax.rsqrt(var + EPS) * lng_ref[...] + lnb_ref[...]).astype(BF16)
    hd = D_GROUP // GM_HEADS
    for c in range(tm // GM_CHUNK):
        rows = slice(c * GM_CHUNK, (c + 1) * GM_CHUNK)
        s = bsf_ref[...]
        for g in range(GM_HEADS):
            sg = jnp.dot(ws_ref[g], vn[rows, :], preferred_element_type=F32)
            s = s + jnp.where((lane >= g * hd) & (lane < (g + 1) * hd), sg, 0.0)
        ygm_ref[rows, :] = u[rows, :] * s

    def glu(t):
        return t[:, :D_GROUP] * jax.nn.sigmoid(t[:, D_GROUP:])

    first = (i % tiles_per_seq) == 0
    last = (i % tiles_per_seq) == tiles_per_seq - 1
    span = tm + 2 * CONV_HALO
    glu_ref[0, 0:CONV_HALO, :] = jnp.where(first, 0.0, glu(cvp_ref[...]))
    glu_ref[0, CONV_HALO:CONV_HALO + tm, :] = glu(cv_ref[...])
    glu_ref[0, CONV_HALO + tm:span, :] = jnp.where(last, 0.0, glu(cvn_ref[...]))
    glu_ref[0, span:, :] = jnp.zeros((SUBLANES, D_GROUP), F32)
    for b in range(1, SUBLANES):
        glu_ref[b, 0:span, :] = glu_ref[0, b:b + span, :]
    pad = (CV_WIDTH - 1) // 2
    rc = 128

    def group_mean(t):
        hi = t.astype(BF16)
        lo = (t - hi.astype(F32)).astype(BF16)
        return (jnp.dot(hi, avg_ref[...], preferred_element_type=F32)
                + jnp.dot(lo, avg_ref[...], preferred_element_type=F32))

    for c in range(tm // rc):
        acc = jnp.zeros((rc, D_GROUP), F32) + dwb_ref[...]
        for k in range(CV_WIDTH):
            start = c * rc + CONV_HALO - pad + k
            b = start % SUBLANES
            acc = acc + glu_ref[b, start - b:start - b + rc, :] * dww_ref[k:k + 1, :]
        d = acc - group_mean(acc)
        gvar = group_mean(d * d)
        n = d * lax.rsqrt(gvar + EPS) * cg_ref[...] + cb_ref[...]
        ycv_ref[c * rc:(c + 1) * rc, :] = n * jax.nn.sigmoid(n)


def _mixers_call(p_gm, p_cv, gm_ln_g, gm_ln_b, gm_ws, gm_bs, cv_dw_w, cv_dw_b, cv_ln_g, cv_ln_b, L, tm):
    T = p_gm.shape[0]
    tps = L // tm
    hb = tm // CONV_HALO
    n_hblk = T // CONV_HALO
    const2 = lambda i: (0, 0)
    tok = lambda i: (i, 0)
    bs_full = jnp.repeat(gm_bs.T.astype(F32), D_GROUP // GM_HEADS, axis=1)
    gid = np.arange(D_GROUP) // (D_GROUP // CV_GROUPS)
    avg = jnp.asarray((gid[:, None] == gid[None, :]).astype(np.float32) / (D_GROUP // CV_GROUPS)).astype(BF16)
    row = lambda a: a[None, :].astype(F32)
    return pl.pallas_call(
        functools.partial(_mixers_kernel, tiles_per_seq=tps),
        grid=(T // tm,),
        in_specs=[
            pl.BlockSpec((tm, GM_COLS), tok),
            pl.BlockSpec((tm, CV_COLS), tok),
            pl.BlockSpec((CONV_HALO, CV_COLS), lambda i: (jnp.maximum(i * hb - 1, 0), 0)),
            pl.BlockSpec((CONV_HALO, CV_COLS), lambda i: (jnp.minimum((i + 1) * hb, n_hblk - 1), 0)),
            pl.BlockSpec((1, D_GROUP), const2),
            pl.BlockSpec((1, D_GROUP), const2),
            pl.BlockSpec((GM_HEADS, GM_CHUNK, GM_CHUNK), lambda i: (0, 0, 0)),
            pl.BlockSpec((GM_CHUNK, D_GROUP), const2),
            pl.BlockSpec((CV_WIDTH, D_GROUP), const2),
            pl.BlockSpec((1, D_GROUP), const2),
            pl.BlockSpec((1, D_GROUP), const2),
            pl.BlockSpec((1, D_GROUP), const2),
            pl.BlockSpec((D_GROUP, D_GROUP), const2),
        ],
        out_specs=[pl.BlockSpec((tm, D_GROUP), tok)] * 2,
        out_shape=[jax.ShapeDtypeStruct((T, D_GROUP), F32)] * 2,
        scratch_shapes=[pltpu.VMEM((SUBLANES, tm + 2 * CONV_HALO + SUBLANES, D_GROUP), F32)],
        compiler_params=_cparams(1),
        name="gmlp_conv_mixers",
    )(p_gm, p_cv, p_cv, p_cv, row(gm_ln_g), row(gm_ln_b), gm_ws.astype(BF16), bs_full, cv_dw_w.astype(F32),
      row(cv_dw_b), row(cv_ln_g), row(cv_ln_b), avg)


def _out_kernel(hye_ref, hyo_ref, gm_ref, cv_ref, at_ref, x_ref, mg_ref, wo_ref, g1_ref, a2_ref, sh2_ref, wrh_ref,
                wrl_ref, rb_ref, tri_ref, cin_ref, xm_ref, hfx_ref, cls_ref, rank_ref, cnt_ref, carry_ref, hy_ref):
    i = pl.program_id(0)
    tm = x_ref.shape[0]
    for k in range(D_GROUP // LANES):
        hy_ref[k, pl.ds(0, tm // 2, stride=2), :] = hye_ref[:, k * LANES:(k + 1) * LANES]
        hy_ref[k, pl.ds(1, tm // 2, stride=2), :] = hyo_ref[:, k * LANES:(k + 1) * LANES]
    y_hy = jnp.concatenate([hy_ref[k] for k in range(D_GROUP // LANES)], axis=1)
    o = None
    for g, y_src in enumerate((y_hy, gm_ref, cv_ref, at_ref)):
        y = y_src[...]
        n = y * lax.rsqrt(jnp.mean(y * y, axis=-1, keepdims=True) + EPS) * mg_ref[:, g * D_GROUP:(g + 1) * D_GROUP]
        part = jnp.dot(n.astype(BF16), wo_ref[g * D_GROUP:(g + 1) * D_GROUP, :], preferred_element_type=F32)
        o = part if o is None else o + part
    xm = x_ref[...] + g1_ref[...] * o
    xm_ref[...] = xm
    hf = xm * lax.rsqrt(jnp.mean(xm * xm, axis=-1, keepdims=True) + EPS) * a2_ref[...] + sh2_ref[...]

    hf_hi = hf.astype(BF16)
    hf_lo = (hf - hf_hi.astype(F32)).astype(BF16)
    logits = (jnp.dot(hf_hi, wrh_ref[...], preferred_element_type=F32)
              + jnp.dot(hf_lo, wrh_ref[...], preferred_element_type=F32)
              + jnp.dot(hf_hi, wrl_ref[...], preferred_element_type=F32))
    cls, g_lo, g_hi = _route_top2(jnp.transpose(logits), rb_ref[...])
    cls_ref[...] = cls

    d_model = hf.shape[1]
    hfx_ref[:, :d_model] = hf
    gate_rows = jnp.concatenate([g_lo, g_hi, jnp.zeros((LANES - TOP_K, tm), F32)], axis=0)
    hfx_ref[:, d_model:] = jnp.transpose(gate_rows)

    @pl.when(i == 0)
    def _():
        carry_ref[...] = cin_ref[...]

    sub = lax.broadcasted_iota(jnp.int32, (CLASS_ROWS, tm), 0)
    onehot = sub == cls
    prefix = jnp.dot(onehot.astype(BF16), tri_ref[...], preferred_element_type=F32)
    carry = carry_ref[...]
    rank = jnp.sum(jnp.where(onehot, prefix + carry[:, 0:1], 0.0), axis=0, keepdims=True)
    rank_ref[...] = rank.astype(jnp.int32)
    carry = carry + jnp.sum(onehot.astype(F32), axis=1, keepdims=True)
    carry_ref[...] = carry
    cnt_ref[...] = carry


def _out_alias_kernel(*refs):
    _out_kernel(*refs[1:])


def _out_call(y_hy_t, y_gm, y_cv, y_at, xt, mixg, wo, g1, a2, sh2, mod_map, wr_hi, wr_lo, rbias, counts_in, L, tm,
              hfx_prev, t_total, row0):
    T, D = xt.shape
    tps = L // tm
    blk0 = row0 // tm
    const = lambda i: (0, 0)
    tok = lambda i: (i, 0)
    lane_tok = lambda i: (0, i)
    tri = jnp.asarray(np.triu(np.ones((tm, tm), np.float32), k=1)).astype(BF16)
    W = D + LANES
    parity = lambda par: pl.BlockSpec((None, tm // 2, D_GROUP), lambda i: (par, i % tps, i // tps))
    in_specs = [parity(0), parity(1)] + [pl.BlockSpec((tm, D_GROUP), tok)] * 3 + [
        pl.BlockSpec((tm, D), tok),
        pl.BlockSpec(mixg.shape, const),
        pl.BlockSpec(wo.shape, const),
        pl.BlockSpec((None, 1, D), mod_map),
        pl.BlockSpec((None, 1, D), mod_map),
        pl.BlockSpec((None, 1, D), mod_map),
        pl.BlockSpec(wr_hi.shape, const),
        pl.BlockSpec(wr_lo.shape, const),
        pl.BlockSpec(rbias.shape, const),
        pl.BlockSpec((tm, tm), const),
        pl.BlockSpec((CLASS_ROWS, LANES), const),
    ]
    args = (y_hy_t, y_hy_t, y_gm, y_cv, y_at, xt, mixg, wo, g1, a2, sh2, wr_hi, wr_lo, rbias, tri, counts_in)
    aliased = hfx_prev is not None
    return pl.pallas_call(
        _out_alias_kernel if aliased else _out_kernel,
        grid=(T // tm,),
        in_specs=([pl.BlockSpec(memory_space=pl.ANY)] if aliased else []) + in_specs,
        out_specs=[pl.BlockSpec((tm, D), tok), pl.BlockSpec((tm, W), lambda i: (blk0 + i, 0)),
                   pl.BlockSpec((1, tm), lane_tok), pl.BlockSpec((1, tm), lane_tok),
                   pl.BlockSpec((CLASS_ROWS, LANES), const)],
        out_shape=[jax.ShapeDtypeStruct((T, D), F32), jax.ShapeDtypeStruct((t_total, W), F32),
                   jax.ShapeDtypeStruct((1, T), jnp.int32), jax.ShapeDtypeStruct((1, T), jnp.int32),
                   jax.ShapeDtypeStruct((CLASS_ROWS, LANES), F32)],
        scratch_shapes=[pltpu.VMEM((CLASS_ROWS, LANES), F32), pltpu.VMEM((D_GROUP // LANES, tm, LANES), F32)],
        input_output_aliases={0: 1} if aliased else {},
        compiler_params=_cparams(1),
        name="mix_out_norm2_route",
    )(*(((hfx_prev,) if aliased else ()) + args))


def _first_max_flags(vals):
    m = functools.reduce(jnp.maximum, vals)
    flags, taken = [], None
    for v in vals:
        f = v >= m
        if taken is not None:
            f = f & jnp.logical_not(taken)
        flags.append(f)
        taken = f if taken is None else taken | f
    return flags, m


def _pick(flags, vals):
    out = vals[-1]
    for f, v in zip(flags[-2::-1], vals[-2::-1]):
        out = jnp.where(f, v, out)
    return out


def _route_top2(lt, bias_col):
    s_all = jax.nn.sigmoid(lt[:N_EXPERTS, :])
    sel_all = s_all + bias_col
    s = [s_all[e:e + 1, :] for e in range(N_EXPERTS)]
    sel = [sel_all[e:e + 1, :] for e in range(N_EXPERTS)]
    neg = -jnp.inf
    E = EXPERTS_PER_GROUP

    def top2(vals):
        f1, m1 = _first_max_flags(vals)
        rest = [jnp.where(f, neg, v) for f, v in zip(f1, vals)]
        f2, m2 = _first_max_flags(rest)
        return f1, m1, f2, m2

    scores = []
    for g in range(N_EXPERT_GROUPS):
        _, m1, _, m2 = top2(sel[g * E:(g + 1) * E])
        scores.append(m1 + m2)
    gflags, _ = _first_max_flags(scores)
    bsel = [_pick(gflags, [sel[g * E + j] for g in range(N_EXPERT_GROUPS)]) for j in range(E)]
    bs = [_pick(gflags, [s[g * E + j] for g in range(N_EXPERT_GROUPS)]) for j in range(E)]
    f1, _, f2, _ = top2(bsel)
    zero = jnp.zeros_like(bs[0])
    w1 = functools.reduce(jnp.add, [jnp.where(f, v, zero) for f, v in zip(f1, bs)])
    w2 = functools.reduce(jnp.add, [jnp.where(f, v, zero) for f, v in zip(f2, bs)])
    izero = jnp.zeros(w1.shape, jnp.int32)
    j1 = functools.reduce(jnp.add, [jnp.where(f, j, izero) for j, f in enumerate(f1)])
    j2 = functools.reduce(jnp.add, [jnp.where(f, j, izero) for j, f in enumerate(f2)])
    gi = functools.reduce(jnp.add, [jnp.where(f, g, izero) for g, f in enumerate(gflags)])
    lo, hi = jnp.minimum(j1, j2), jnp.maximum(j1, j2)
    pair = jnp.where(lo == 0, 0, jnp.where(lo == 1, 3, 5)) + hi - lo - 1
    tot = w1 + w2
    first_is_lo = j1 < j2
    return gi * N_PAIRS + pair, jnp.where(first_is_lo, w1, w2) / tot, jnp.where(first_is_lo, w2, w1) / tot


def _table_lookup(table, idx):
    n = table.shape[0]
    hit = idx[..., None] == jnp.arange(n, dtype=jnp.int32)
    return jnp.sum(jnp.where(hit, table, 0), axis=-1)


def _class_plan(cls, rank, counts, tile):
    T = cls.shape[0]
    n_tiles = (T + N_CLASSES * tile) // tile
    order = jnp.argsort(cls, stable=True).astype(jnp.int32)
    padded = ((counts + tile - 1) // tile) * tile
    seg_end = jnp.cumsum(padded)
    seg_start = seg_end - padded
    src_start = jnp.cumsum(counts) - counts
    tok_row = _table_lookup(seg_start, cls) + rank

    tile_first = jnp.arange(n_tiles, dtype=jnp.int32) * tile
    tile_class = jnp.minimum(jnp.sum((seg_end[None, :] <= tile_first[:, None]).astype(jnp.int32), axis=1),
                             N_CLASSES - 1)
    off = tile_first - _table_lookup(seg_start, tile_class)
    tile_cnt = jnp.clip(_table_lookup(counts, tile_class) - off, 0, tile)
    tile_src = jnp.clip(_table_lookup(src_start, tile_class) + off, 0, T - 1)
    grp = tile_class // N_PAIRS
    pair = tile_class % N_PAIRS
    tile_lo = grp * EXPERTS_PER_GROUP + _table_lookup(jnp.asarray(PAIR_LO, jnp.int32), pair)
    tile_hi = grp * EXPERTS_PER_GROUP + _table_lookup(jnp.asarray(PAIR_HI, jnp.int32), pair)
    return order, tile_lo, tile_hi, tile_src, tile_cnt, tok_row


def _moe_pair_kernel(order_ref, lo_ref, hi_ref, src_ref, cnt_ref, hfx_hbm, wga_ref, wua_ref, wda_ref, wgb_ref,
                     wub_ref, wdb_ref, o_ref, buf_ref, sem_ref):
    i = pl.program_id(0)
    n = pl.num_programs(0)
    tile = o_ref.shape[0]
    n_tok = order_ref.shape[0]
    slot = i % 2

    def start(t, s):
        base = src_ref[t]
        _start_row_gather(hfx_hbm, lambda j: order_ref[jnp.minimum(base + j, n_tok - 1)], tile, buf_ref, s, sem_ref)

    @pl.when((i == 0) & (cnt_ref[0] > 0))
    def _():
        start(0, 0)

    @pl.when((cnt_ref[i] <= 0) & (i > 0) & (cnt_ref[jnp.maximum(i - 1, 0)] > 0))
    def _():
        _wait_row_gather(hfx_hbm, tile, buf_ref, slot, sem_ref)

    @pl.when(cnt_ref[i] > 0)
    def _():
        start(jnp.minimum(i + 1, n - 1), 1 - slot)
        _wait_row_gather(hfx_hbm, tile, buf_ref, slot, sem_ref)
        d_model = buf_ref.shape[2] - LANES
        x = buf_ref[slot, :, :d_model].astype(BF16)
        live = lax.broadcasted_iota(jnp.int32, (tile, 1), 0) < cnt_ref[i]
        gates = buf_ref[slot, :, d_model:]
        g_lo = jnp.where(live, gates[:, 0:1], 0.0)
        g_hi = jnp.where(live, gates[:, 1:2], 0.0)

        def hidden(wg_ref, wu_ref, gate):
            hg = jnp.dot(x, wg_ref[...], preferred_element_type=F32)
            hu = jnp.dot(x, wu_ref[...], preferred_element_type=F32)
            return (hg * jax.nn.sigmoid(hg) * hu * gate).astype(BF16)

        o_ref[...] = (jnp.dot(hidden(wga_ref, wua_ref, g_lo), wda_ref[...], preferred_element_type=F32)
                      + jnp.dot(hidden(wgb_ref, wub_ref, g_hi), wdb_ref[...], preferred_element_type=F32))

    @pl.when(cnt_ref[i] <= 0)
    def _():
        o_ref[...] = jnp.zeros_like(o_ref)


def _moe_pair_call(hfx, order, tile_lo, tile_hi, tile_src, tile_cnt, wg, wu, wd, tile):
    T, W = hfx.shape
    D = W - LANES
    F = wg.shape[-1]
    n_tiles = tile_lo.shape[0]
    lo = lambda i, o, tl, th, ts, tc: (tl[i], 0, 0)
    hi = lambda i, o, tl, th, ts, tc: (th[i], 0, 0)
    grid_spec = pltpu.PrefetchScalarGridSpec(
        num_scalar_prefetch=5,
        grid=(n_tiles,),
        in_specs=[
            pl.BlockSpec(memory_space=pl.ANY),
            pl.BlockSpec((None, D, F), lo), pl.BlockSpec((None, D, F), lo), pl.BlockSpec((None, F, D), lo),
            pl.BlockSpec((None, D, F), hi), pl.BlockSpec((None, D, F), hi), pl.BlockSpec((None, F, D), hi),
        ],
        out_specs=pl.BlockSpec((tile, D), lambda i, o, tl, th, ts, tc: (i, 0)),
        scratch_shapes=[pltpu.VMEM((2, tile, W), F32), pltpu.SemaphoreType.DMA((2,))],
    )
    return pl.pallas_call(
        _moe_pair_kernel,
        grid_spec=grid_spec,
        out_shape=jax.ShapeDtypeStruct((n_tiles * tile, D), F32),
        compiler_params=_cparams(1),
        name="moe_pair_grouped",
    )(order, tile_lo, tile_hi, tile_src, tile_cnt, hfx, wg, wu, wd, wg, wu, wd)


def _final_kernel(idx_ref, xm_ref, g2_ref, ys_hbm, g_ref, o_ref, buf_ref, sem_ref):
    x = xm_ref[...] + g2_ref[...] * _gathered_rows(idx_ref, ys_hbm, buf_ref, sem_ref, xm_ref.shape[0])
    o_ref[...] = x * lax.rsqrt(jnp.mean(x * x, axis=-1, keepdims=True) + EPS) * g_ref[...]
    _drain_gathered_rows(ys_hbm, buf_ref, sem_ref, xm_ref.shape[0])


def _final_call(x_mid, tok_row, g2, mod_map, ys, g, tm):
    T, D = x_mid.shape
    tok = lambda i, *_: (i, 0)
    grid_spec = pltpu.PrefetchScalarGridSpec(
        num_scalar_prefetch=1,
        grid=(T // tm,),
        in_specs=[pl.BlockSpec((tm, D), tok), pl.BlockSpec((None, 1, D), lambda i, *_: mod_map(i)),
                  pl.BlockSpec(memory_space=pl.ANY), pl.BlockSpec((1, D), lambda i, *_: (0, 0))],
        out_specs=pl.BlockSpec((tm, D), tok),
        scratch_shapes=[pltpu.VMEM((2, tm, ys.shape[1]), ys.dtype), pltpu.SemaphoreType.DMA((2,))],
    )
    return pl.pallas_call(
        _final_kernel,
        grid_spec=grid_spec,
        out_shape=jax.ShapeDtypeStruct((T, D), F32),
        compiler_params=_cparams(1),
        name="moe_residual_final_norm",
    )(tok_row, x_mid, g2, ys, g[None, :].astype(F32))


def kernel(x, c, ctx, c_ctx, ada_w, ada_b, norm1_g, norm2_g, w_in, hy_short_w, hy_short_b, hy_f_w1, hy_f_b1, hy_f_freq, hy_f_w2, hy_f_b2, hy_f_w3, hy_bias, gm_ln_g, gm_ln_b, gm_ws, gm_bs, cv_dw_w, cv_dw_b, cv_ln_g, cv_ln_b, mla_qa_norm, w_uq, mla_kva_norm, w_ukv, mix_norm_g, w_out, w_router, router_bias, exp_w_gate, exp_w_up, exp_w_down, final_norm_g):
    B, n_lat, D = x.shape
    n_ctx = ctx.shape[1]
    T_lat, T_ctx = B * n_lat, B * n_ctx
    tm_lat = min(512, n_lat)
    tm_ctx = min(256, n_ctx)
    assert n_lat % tm_lat == 0 and n_ctx % tm_ctx == 0 and n_lat % ATT_TILE == 0
    assert tm_lat % GM_CHUNK == 0 and tm_ctx % GM_CHUNK == 0 and T_lat % MOE_TILE == 0 and T_ctx % MOE_TILE == 0

    tps_lat, tps_ctx = n_lat // tm_lat, n_ctx // tm_ctx
    dft = {L: _half_dft(L) for L in {n_lat, n_ctx}}
    rope_lat = _rope_slot_tables(n_lat)
    rope_ctx = _identity_slot_tables(tm_ctx)
    wr_pad = jnp.pad(w_router.astype(F32), ((0, 0), (0, LANES - N_EXPERTS)))
    wr_hi = wr_pad.astype(BF16)
    wr_lo = (wr_pad - wr_hi.astype(F32)).astype(BF16)
    rbias = router_bias.astype(F32)[:, None]
    cond = jnp.concatenate([c, c_ctx[None, :]], axis=0)
    streams = {
        'lat': dict(x=x.reshape(T_lat, D), L=n_lat, tm=tm_lat, mod_map=lambda i: (i // tps_lat, 0, 0),
                    rope=rope_lat, rope_map=lambda i: (i % tps_lat, 0), row0=0, moe_in=None),
        'ctx': dict(x=ctx.reshape(T_ctx, D), L=n_ctx, tm=tm_ctx, mod_map=lambda i: (B, 0, 0),
                    rope=rope_ctx, rope_map=lambda i: (0, 0), row0=T_lat, moe_in=None),
    }

    for l in range(DEPTH):
        last = l == DEPTH - 1
        m = jax.nn.silu(cond) @ ada_w[l] + ada_b[l]
        sh1, sc1, g1, sh2, sc2, g2 = [t[:, None, :] for t in jnp.split(m, 6, axis=-1)]
        a1 = norm1_g[l][None, None, :] * (1.0 + sc1)
        a2 = norm2_g[l][None, None, :] * (1.0 + sc2)
        qg = jnp.pad(mla_qa_norm[l], (0, PQ_PAD - MLA_Q_RANK))[None, :].astype(F32)
        kvg = mla_kva_norm[l][None, :].astype(F32)
        win, wuq, wkv = _pack_w_in(w_in[l]), _pack_w_uq(w_uq[l]), _pack_w_ukv(w_ukv[l])
        filt = (hy_f_w1[l], hy_f_b1[l], hy_f_freq[l], hy_f_w2[l], hy_f_b2[l], hy_f_w3[l])
        wg, wu, wd = exp_w_gate[l].astype(BF16), exp_w_up[l].astype(BF16), exp_w_down[l].astype(BF16)

        proj = {name: _proj_call(s['x'], a1, sh1, s['mod_map'], win, qg, wuq, kvg, wkv, s['rope'], s['rope_map'],
                                 s['tm'], s['moe_in']) for name, s in streams.items()}
        for name, s in streams.items():
            if s['moe_in'] is not None:
                s['x'] = proj[name][6]
        k_ctx, v_ctx = proj['ctx'][4], proj['ctx'][5]

        active = ('lat',) if last else ('lat', 'ctx')
        x_mid, cls, rank = {}, {}, {}
        t_moe = T_lat if last else T_lat + T_ctx
        hfx = None if last else jnp.zeros((t_moe, D + LANES), F32)
        counts = jnp.zeros((CLASS_ROWS, LANES), F32)
        for name in active:
            s = streams[name]
            p_hy, p_gm, p_cv, q, k, v = proj[name][:6]
            if name == 'lat':
                y_at = _attn_lat_call(q, k, v, k_ctx, v_ctx, B, n_lat, n_ctx)
            else:
                y_at = _attn_ctx_call(q, k, v, B, n_ctx)
            filters = _hyena_filters(s['L'], *filt)
            y_hy_t = _hyena_call(p_hy, hy_short_w[l], hy_short_b[l], filters, hy_bias[l], B, s['L'], dft[s['L']])
            y_gm, y_cv = _mixers_call(p_gm, p_cv, gm_ln_g[l], gm_ln_b[l], gm_ws[l], gm_bs[l], cv_dw_w[l],
                                      cv_dw_b[l], cv_ln_g[l], cv_ln_b[l], s['L'], s['tm'])
            x_mid[name], hfx, cls[name], rank[name], counts = _out_call(
                y_hy_t, y_gm, y_cv, y_at, s['x'], mix_norm_g[l][None, :].astype(F32), w_out[l].astype(BF16),
                g1, a2, sh2, s['mod_map'], wr_hi, wr_lo, rbias, counts, s['L'], s['tm'],
                hfx, t_moe, s['row0'])

        cat = lambda d: jnp.concatenate([d[name][0] for name in active], axis=0)
        order, tile_lo, tile_hi, tile_src, tile_cnt, tok_row = _class_plan(
            cat(cls), cat(rank), counts[:N_CLASSES, 0].astype(jnp.int32), MOE_TILE)
        ys = _moe_pair_call(hfx, order, tile_lo, tile_hi, tile_src, tile_cnt, wg, wu, wd, MOE_TILE)
        for name in active:
            s = streams[name]
            s['x'] = x_mid[name]
            s['moe_in'] = (lax.slice_in_dim(tok_row, s['row0'], s['row0'] + x_mid[name].shape[0]), g2, ys)

    s = streams['lat']
    tok_row, g2, ys = s['moe_in']
    return _final_call(s['x'], tok_row, g2, s['mod_map'], ys, final_norm_g, s['tm']).reshape(B, n_lat, D)
```

```python
import functools
import math

import jax
import jax.numpy as jnp
import numpy as np
from jax import lax
from jax.experimental import pallas as pl
from jax.experimental.pallas import tpu as pltpu

F32 = jnp.float32
BF16 = jnp.bfloat16

D_MODEL = 1024
DEPTH = 2
GRID_W = 64
EPS = 1e-6

D_GROUP = 256
N_MIXERS = 4
HY_ORDER = 2
HY_SHORT = 3
HY_EMB = 33
HY_BANDS = (HY_EMB - 1) // 2
HY_TARGET = 1e-2
HY_FAST_PCT = 0.3
HY_SLOW_PCT = 1.5
GM_CHUNK = 128
GM_HEADS = 4
CV_WIDTH = 31
CV_GROUPS = 4
MLA_HEADS = 4
MLA_NOPE = 64
MLA_ROPE = 32
MLA_V = 64
MLA_Q_RANK = 192
MLA_KV_RANK = 128
ROPE_BASE = 10000.0
N_EXPERTS = 16
N_EXPERT_GROUPS = 4
EXPERTS_PER_GROUP = N_EXPERTS // N_EXPERT_GROUPS
TOP_K = 2
D_EXPERT = 512

HY_COLS = (HY_ORDER + 1) * D_GROUP
GM_COLS = 2 * D_GROUP
CV_COLS = 2 * D_GROUP
MQ_COLS = MLA_Q_RANK
MKV_COLS = MLA_KV_RANK + MLA_ROPE
HY_OFF = 0
GM_OFF = HY_OFF + HY_COLS
CV_OFF = GM_OFF + GM_COLS
MQ_OFF = CV_OFF + CV_COLS
MKV_OFF = MQ_OFF + MQ_COLS
IN_COLS = MKV_OFF + MKV_COLS

LANES = 128
SUBLANES = 8
HEAD_SLOT = LANES
QK_COLS = MLA_HEADS * HEAD_SLOT
V_COLS = MLA_HEADS * MLA_V
PQ_OFF = MQ_OFF
PQ_PAD = 256
PKV_OFF = PQ_OFF + PQ_PAD
PKR_OFF = PKV_OFF + MLA_KV_RANK
PROJ_COLS = PKR_OFF + LANES

PAIR_LO = (0, 0, 0, 1, 1, 2)
PAIR_HI = (1, 2, 3, 2, 3, 3)
N_PAIRS = len(PAIR_LO)
N_CLASSES = N_EXPERT_GROUPS * N_PAIRS
CLASS_ROWS = -(-N_CLASSES // SUBLANES) * SUBLANES

ATT_TILE = 512
ATT_SUB = 256
ATT_SCORE_BUFS = 2
MOE_TILE = 256
DFT_ROWS = 256
DFT_COLS = 1024
CONV_HALO = 16
VMEM_LIMIT = 56 * 1024 * 1024


def _cparams(n_axes):
    return pltpu.CompilerParams(dimension_semantics=("arbitrary",) * n_axes, vmem_limit_bytes=VMEM_LIMIT)


def _axial_rope_tables(n_lat):
    rows = n_lat // GRID_W
    row = jnp.repeat(jnp.arange(rows), GRID_W).astype(F32)
    col = jnp.tile(jnp.arange(GRID_W), rows).astype(F32)
    n_freq = MLA_ROPE // 4
    inv = ROPE_BASE ** (-jnp.arange(n_freq, dtype=F32) / n_freq)
    ang = jnp.concatenate([row[:, None] * inv, col[:, None] * inv], axis=-1)
    return jnp.cos(ang), jnp.sin(ang)


def _hyena_filters(L, w1, b1, freq, w2, b2, w3):
    w1, b1, freq, w2, b2, w3 = (a.astype(F32) for a in (w1, b1, freq, w2, b2, w3))
    t_all = jnp.linspace(0.0, 1.0, L, dtype=F32)
    f = jnp.linspace(1e-4, HY_BANDS - 1, HY_BANDS, dtype=F32)[None, :]
    deltas = jnp.abs(jnp.linspace(math.log(HY_TARGET) / HY_SLOW_PCT,
                                  math.log(HY_TARGET) / HY_FAST_PCT, D_GROUP, dtype=F32))

    def taps(pos, direction):
        t = t_all[pos][:, None]
        w = 2.0 * math.pi * pos.astype(F32)[:, None] / L
        z = jnp.concatenate([t, jnp.cos(f * w), -jnp.sin(f * w)], axis=-1)
        h = jnp.sin(freq * (z @ w1 + b1))
        h = jnp.sin(freq * (h @ w2 + b2))
        h = (h @ w3).reshape(L, HY_ORDER, 2, D_GROUP)[:, :, direction]
        return h * jnp.exp(-t * deltas)[:, None, :]

    m = jnp.arange(L, dtype=jnp.int32)
    kf = taps(m, 0)
    kb = jnp.where((m > 0)[:, None, None], taps((L - m) % L, 1), 0.0)
    norm = jnp.sum(jnp.abs(kf), axis=0, keepdims=True) + jnp.sum(jnp.abs(kb), axis=0, keepdims=True)
    w_cols = HY_ORDER * D_GROUP
    return jnp.concatenate([(kf / norm).reshape(L, w_cols), (kb / norm).reshape(L, w_cols)], axis=1)


def _pack_w_in(w_in):
    D = w_in.shape[0]
    z = lambda n: jnp.zeros((D, n), w_in.dtype)
    return jnp.concatenate([
        w_in[:, :MQ_OFF],
        w_in[:, MQ_OFF:MKV_OFF], z(PQ_PAD - MQ_COLS),
        w_in[:, MKV_OFF:MKV_OFF + MLA_KV_RANK],
        w_in[:, MKV_OFF + MLA_KV_RANK:], z(LANES - MLA_ROPE),
    ], axis=1).astype(BF16)


def _pack_w_uq(w_uq):
    w = w_uq.reshape(MLA_Q_RANK, MLA_HEADS, MLA_NOPE + MLA_ROPE)
    w = jnp.pad(w, ((0, PQ_PAD - MLA_Q_RANK), (0, 0), (0, HEAD_SLOT - MLA_NOPE - MLA_ROPE)))
    return w.reshape(PQ_PAD, QK_COLS).astype(BF16)


def _pack_w_ukv(w_ukv):
    w = w_ukv.reshape(MLA_KV_RANK, MLA_HEADS, MLA_NOPE + MLA_V)
    k_part = jnp.pad(w[:, :, :MLA_NOPE], ((0, 0), (0, 0), (0, HEAD_SLOT - MLA_NOPE))).reshape(MLA_KV_RANK, QK_COLS)
    v_part = w[:, :, MLA_NOPE:].reshape(MLA_KV_RANK, V_COLS)
    top = jnp.concatenate([k_part, v_part], axis=1)
    eye = jnp.eye(LANES, dtype=w_ukv.dtype)[:, :MLA_ROPE]
    place = jnp.pad(eye, ((0, 0), (MLA_NOPE, HEAD_SLOT - MLA_NOPE - MLA_ROPE)))
    bot = jnp.concatenate([jnp.tile(place, (1, MLA_HEADS)), jnp.zeros((LANES, V_COLS), w_ukv.dtype)], axis=1)
    return jnp.concatenate([top, bot], axis=0).astype(BF16)


def _rope_slot_tables(n_lat):
    cos, sin = _axial_rope_tables(n_lat)
    half = MLA_ROPE // 2
    tail_w = HEAD_SLOT - MLA_NOPE - MLA_ROPE
    ones = lambda n: jnp.ones((n_lat, n), F32)
    zeros = lambda n: jnp.zeros((n_lat, n), F32)
    cf = jnp.concatenate([ones(MLA_NOPE), cos, cos, ones(tail_w)], axis=1)
    s_up = jnp.concatenate([zeros(MLA_NOPE + half), sin, zeros(tail_w)], axis=1)
    s_dn = jnp.concatenate([zeros(MLA_NOPE), -sin, zeros(half + tail_w)], axis=1)
    return cf, s_up, s_dn


def _identity_slot_tables(rows):
    return (jnp.ones((rows, HEAD_SLOT), F32), jnp.zeros((rows, HEAD_SLOT), F32), jnp.zeros((rows, HEAD_SLOT), F32))


def _rotate_slots(t, cf, s_up, s_dn):
    half = MLA_ROPE // 2
    outs = []
    for h in range(MLA_HEADS):
        tb = t[:, h * HEAD_SLOT:(h + 1) * HEAD_SLOT]
        outs.append(tb * cf + pltpu.roll(tb, half, 1) * s_up + pltpu.roll(tb, HEAD_SLOT - half, 1) * s_dn)
    return jnp.concatenate(outs, axis=1)


def _row_gather_copy(src_hbm, row, buf_ref, slot, j, sem_ref):
    return pltpu.make_async_copy(src_hbm.at[pl.ds(row, 1)], buf_ref.at[slot, pl.ds(j, 1)], sem_ref.at[slot])


def _start_row_gather(src_hbm, row_of, n_rows, buf_ref, slot, sem_ref):
    for j in range(n_rows):
        _row_gather_copy(src_hbm, row_of(j), buf_ref, slot, j, sem_ref).start()


def _wait_row_gather(src_hbm, n_rows, buf_ref, slot, sem_ref):
    pltpu.make_async_copy(src_hbm.at[pl.ds(0, n_rows)], buf_ref.at[slot], sem_ref.at[slot]).wait()


def _gathered_rows(idx_ref, ys_hbm, buf_ref, sem_ref, tm):
    i = pl.program_id(0)
    n = pl.num_programs(0)
    slot = i % 2

    @pl.when(i == 0)
    def _():
        _start_row_gather(ys_hbm, lambda j: idx_ref[j], tm, buf_ref, 0, sem_ref)

    nxt = jnp.minimum(i + 1, n - 1)
    _start_row_gather(ys_hbm, lambda j: idx_ref[nxt * tm + j], tm, buf_ref, 1 - slot, sem_ref)
    _wait_row_gather(ys_hbm, tm, buf_ref, slot, sem_ref)
    return buf_ref[slot]


def _drain_gathered_rows(ys_hbm, buf_ref, sem_ref, tm):
    i = pl.program_id(0)

    @pl.when(i == pl.num_programs(0) - 1)
    def _():
        _wait_row_gather(ys_hbm, tm, buf_ref, 1 - i % 2, sem_ref)


def _proj_gather_kernel(idx_ref, xm_ref, g2_ref, ys_hbm, a_ref, sh_ref, win_ref, qg_ref, wuq_ref, kvg_ref, wkv_ref,
                        cf_ref, su_ref, sd_ref, hy_ref, gm_ref, cv_ref, q_ref, k_ref, v_ref, xn_ref, buf_ref, sem_ref):
    y = _gathered_rows(idx_ref, ys_hbm, buf_ref, sem_ref, xm_ref.shape[0])
    xn_ref[...] = xm_ref[...] + g2_ref[...] * y
    _proj_kernel(xn_ref, a_ref, sh_ref, win_ref, qg_ref, wuq_ref, kvg_ref, wkv_ref, cf_ref, su_ref, sd_ref,
                 hy_ref, gm_ref, cv_ref, q_ref, k_ref, v_ref)
    _drain_gathered_rows(ys_hbm, buf_ref, sem_ref, xm_ref.shape[0])


def _proj_kernel(x_ref, a_ref, sh_ref, win_ref, qg_ref, wuq_ref, kvg_ref, wkv_ref, cf_ref, su_ref, sd_ref,
                 hy_ref, gm_ref, cv_ref, q_ref, k_ref, v_ref):
    x = x_ref[...]
    ms = jnp.mean(x * x, axis=-1, keepdims=True)
    h = x * lax.rsqrt(ms + EPS) * a_ref[...] + sh_ref[...]
    p = jnp.dot(h.astype(BF16), win_ref[...], preferred_element_type=F32)
    hy_ref[...] = p[:, HY_OFF:GM_OFF]
    gm_ref[...] = p[:, GM_OFF:CV_OFF]
    cv_ref[...] = p[:, CV_OFF:MQ_OFF]

    cf, su, sd = cf_ref[...], su_ref[...], sd_ref[...]
    cq = p[:, PQ_OFF:PQ_OFF + PQ_PAD]
    qn = cq * lax.rsqrt(jnp.sum(cq * cq, axis=-1, keepdims=True) * (1.0 / MLA_Q_RANK) + EPS) * qg_ref[...]
    q = jnp.dot(qn.astype(BF16), wuq_ref[...], preferred_element_type=F32)
    q = _rotate_slots(q, cf, su, sd) * (math.log2(math.e) / math.sqrt(MLA_NOPE + MLA_ROPE))
    q_ref[...] = q.astype(BF16)

    ckv = p[:, PKV_OFF:PKV_OFF + MLA_KV_RANK]
    kvn = ckv * lax.rsqrt(jnp.mean(ckv * ckv, axis=-1, keepdims=True) + EPS) * kvg_ref[...]
    kin = jnp.concatenate([kvn, p[:, PKR_OFF:PKR_OFF + LANES]], axis=1).astype(BF16)
    kv = jnp.dot(kin, wkv_ref[...], preferred_element_type=F32)
    k_ref[...] = _rotate_slots(kv[:, :QK_COLS], cf, su, sd).astype(BF16)
    v_ref[...] = kv[:, QK_COLS:].astype(BF16)


def _proj_call(xt, mod_a, mod_sh, mod_map, win, qg, wuq, kvg, wkv, rope_tabs, rope_map, tm, moe_in=None):
    T, D = xt.shape
    const = lambda i, *_: (0, 0)
    tok = lambda i, *_: (i, 0)
    mod = lambda i, *_: mod_map(i)
    rope = lambda i, *_: rope_map(i)
    out_cols = (HY_COLS, GM_COLS, CV_COLS, QK_COLS, QK_COLS, V_COLS)
    out_dtypes = (F32, F32, F32, BF16, BF16, BF16)
    in_specs = [
        pl.BlockSpec((None, 1, D), mod),
        pl.BlockSpec((None, 1, D), mod),
        pl.BlockSpec(win.shape, const),
        pl.BlockSpec(qg.shape, const),
        pl.BlockSpec(wuq.shape, const),
        pl.BlockSpec(kvg.shape, const),
        pl.BlockSpec(wkv.shape, const),
        pl.BlockSpec((tm, HEAD_SLOT), rope),
        pl.BlockSpec((tm, HEAD_SLOT), rope),
        pl.BlockSpec((tm, HEAD_SLOT), rope),
    ]
    out_specs = [pl.BlockSpec((tm, n), tok) for n in out_cols]
    out_shape = [jax.ShapeDtypeStruct((T, n), dt) for n, dt in zip(out_cols, out_dtypes)]
    shared = (mod_a, mod_sh, win, qg, wuq, kvg, wkv, *rope_tabs)
    if moe_in is None:
        return pl.pallas_call(
            _proj_kernel,
            grid=(T // tm,),
            in_specs=[pl.BlockSpec((tm, D), tok)] + in_specs,
            out_specs=out_specs,
            out_shape=out_shape,
            compiler_params=_cparams(1),
            name="proj_qkv",
        )(xt, *shared)
    tok_row, g2, ys = moe_in
    grid_spec = pltpu.PrefetchScalarGridSpec(
        num_scalar_prefetch=1,
        grid=(T // tm,),
        in_specs=[pl.BlockSpec((tm, D), tok), pl.BlockSpec((None, 1, D), mod),
                  pl.BlockSpec(memory_space=pl.ANY)] + in_specs,
        out_specs=out_specs + [pl.BlockSpec((tm, D), tok)],
        scratch_shapes=[pltpu.VMEM((2, tm, ys.shape[1]), ys.dtype), pltpu.SemaphoreType.DMA((2,))],
    )
    return pl.pallas_call(
        _proj_gather_kernel,
        grid_spec=grid_spec,
        out_shape=out_shape + [jax.ShapeDtypeStruct((T, D), F32)],
        compiler_params=_cparams(1),
        name="moe_residual_proj_qkv",
    )(tok_row, xt, g2, ys, *shared)


def _attend_heads(q_ref, key_refs, val_refs, o_ref, s_ref):
    nt = (((1,), (1,)), ((), ()))
    lane = lax.broadcasted_iota(jnp.int32, (1, V_COLS), 1)
    sub = min(ATT_SUB, q_ref.shape[0])
    units = [(r, h) for r in range(q_ref.shape[0] // sub) for h in range(MLA_HEADS)]
    col0 = [0]
    for k_ref in key_refs:
        col0.append(col0[-1] + k_ref.shape[0])

    def scores_into(u):
        r, h = units[u]
        q = q_ref[r * sub:(r + 1) * sub, h * HEAD_SLOT:(h + 1) * HEAD_SLOT]
        for j, k_ref in enumerate(key_refs):
            s_ref[u % n_buf, :, col0[j]:col0[j + 1]] = lax.dot_general(
                q, k_ref[:, h * HEAD_SLOT:(h + 1) * HEAD_SLOT], nt, preferred_element_type=F32)

    n_buf = s_ref.shape[0]
    for u in range(min(n_buf - 1, len(units))):
        scores_into(u)
    acc = None
    for u, (r, h) in enumerate(units):
        if u + n_buf - 1 < len(units):
            scores_into(u + n_buf - 1)
        s = s_ref[u % n_buf]
        m = jnp.max(s, axis=-1, keepdims=True)
        p = jnp.exp2(s - m)
        denom = jnp.sum(p, axis=-1, keepdims=True)
        p16 = p.astype(BF16)
        o = functools.reduce(jnp.add, [jnp.dot(p16[:, col0[j]:col0[j + 1]], v_ref[...], preferred_element_type=F32)
                                       for j, v_ref in enumerate(val_refs)])
        in_head = (lane >= h * MLA_V) & (lane < (h + 1) * MLA_V)
        part = jnp.where(in_head, o / denom, 0.0)
        acc = part if h == 0 else acc + part
        if h == MLA_HEADS - 1:
            o_ref[r * sub:(r + 1) * sub, :] = acc


def _attn_lat_kernel(q_ref, kl_ref, kc_ref, vl_ref, vc_ref, o_ref, s_ref):
    _attend_heads(q_ref, (kl_ref, kc_ref), (vl_ref, vc_ref), o_ref, s_ref)


def _attn_ctx_kernel(q_ref, k_ref, v_ref, o_ref, s_ref):
    _attend_heads(q_ref, (k_ref,), (v_ref,), o_ref, s_ref)


def _attn_lat_call(q, k, v, k_ctx, v_ctx, n_batch, n_lat, n_ctx):
    tq = ATT_TILE
    qt = n_lat // tq
    return pl.pallas_call(
        _attn_lat_kernel,
        grid=(n_batch, qt),
        in_specs=[
            pl.BlockSpec((tq, QK_COLS), lambda b, j: (b * qt + j, 0)),
            pl.BlockSpec((n_lat, QK_COLS), lambda b, j: (b, 0)),
            pl.BlockSpec((n_ctx, QK_COLS), lambda b, j: (b, 0)),
            pl.BlockSpec((n_lat, V_COLS), lambda b, j: (b, 0)),
            pl.BlockSpec((n_ctx, V_COLS), lambda b, j: (b, 0)),
        ],
        out_specs=pl.BlockSpec((tq, V_COLS), lambda b, j: (b * qt + j, 0)),
        out_shape=jax.ShapeDtypeStruct((n_batch * n_lat, V_COLS), F32),
        scratch_shapes=[pltpu.VMEM((ATT_SCORE_BUFS, min(ATT_SUB, tq), n_lat + n_ctx), F32)],
        compiler_params=_cparams(2),
        name="attn_latent",
    )(q, k, k_ctx, v, v_ctx)


def _attn_ctx_call(q, k, v, n_batch, n_ctx):
    blk = lambda b: (b, 0)
    return pl.pallas_call(
        _attn_ctx_kernel,
        grid=(n_batch,),
        in_specs=[pl.BlockSpec((n_ctx, QK_COLS), blk), pl.BlockSpec((n_ctx, QK_COLS), blk),
                  pl.BlockSpec((n_ctx, V_COLS), blk)],
        out_specs=pl.BlockSpec((n_ctx, V_COLS), blk),
        out_shape=jax.ShapeDtypeStruct((n_batch * n_ctx, V_COLS), F32),
        scratch_shapes=[pltpu.VMEM((ATT_SCORE_BUFS, min(ATT_SUB, n_ctx), n_ctx), F32)],
        compiler_params=_cparams(1),
        name="attn_context",
    )(q, k, v)


@functools.lru_cache(maxsize=None)
def _dft_factor_tables_np(L):
    f = np.arange(L, dtype=np.int64)[:, None]
    def trig(t):
        ang = ((f * t[None, :]) % (2 * L)).astype(np.float64) * (np.pi / L)
        return np.cos(ang).astype(np.float32), np.sin(ang).astype(np.float32)
    return trig(np.arange(L // LANES, dtype=np.int64) * LANES) + trig(np.arange(LANES, dtype=np.int64))


def _dft_matrices(L):
    assert L % LANES == 0
    c1, s1, c0, s0 = (jnp.asarray(a) for a in _dft_factor_tables_np(L))
    c1, s1, c0, s0 = c1[:, :, None], s1[:, :, None], c0[:, None, :], s0[:, None, :]
    c = (c1 * c0 - s1 * s0).reshape(L, L)
    s = (s1 * c0 + c1 * s0).reshape(L, L)
    alt = jnp.where(jnp.arange(L) % 2 == 0, 1.0, -1.0).astype(F32)
    row0 = (jnp.arange(L) == 0)
    s_fwd = jnp.where(row0[:, None], alt[None, :], s)
    s_inv = jnp.where(row0[None, :], alt[:, None], s)
    return c.astype(BF16), s_fwd.astype(BF16), s_inv.astype(BF16)


def _half_dft(L):
    assert L % 2 == 0
    return _dft_matrices(L // 2) + tuple(jnp.asarray(t) for t in _twiddle_np(L // 2))


def _half_butterfly(ae, ao, be, bo, c, s):
    return (ae + c * ao - s * bo, be + c * bo + s * ao,
            ae - c * ao + s * bo, -be + c * bo + s * ao)


def _twiddle_np(M):
    ang = np.arange(M, dtype=np.float64)[:, None] * (np.pi / (2 * M))
    return np.cos(ang).astype(np.float32), np.sin(ang).astype(np.float32)


def _spectrum_kernel(c_ref, s_ref, k_ref, tc_ref, ts_ref, krl_ref, kil_ref, krh_ref, kih_ref, sp_ref):
    tr, M = c_ref.shape
    w = k_ref.shape[1] // 4
    inv_l = 1.0 / (2 * M)
    k = k_ref[...]
    a = jnp.dot(c_ref[...], k, preferred_element_type=F32)
    b = jnp.dot(s_ref[...], k, preferred_element_type=F32)
    g = pl.program_id(0) * tr + lax.broadcasted_iota(jnp.int32, (tr, 1), 0)
    sign = jnp.where(g % 2 == 0, 1.0, -1.0)
    tc, ts = tc_ref[...], ts_ref[...]
    cols = lambda x, j: x[:, j * w:(j + 1) * w]
    f_lo_a, f_lo_b, f_hi_a, f_hi_b = _half_butterfly(cols(a, 0), cols(a, 2), cols(b, 0), cols(b, 2), tc, ts)
    b_lo_a, b_lo_b, b_hi_a, b_hi_b = _half_butterfly(cols(a, 1), cols(a, 3), cols(b, 1), cols(b, 3), tc, ts)
    kr_lo = (f_lo_a + sign * b_lo_a) * inv_l
    kr_hi = (f_hi_a + sign * b_hi_a) * inv_l
    krl_ref[...] = kr_lo
    kil_ref[...] = -(f_lo_b + sign * b_lo_b) * inv_l
    krh_ref[...] = kr_hi
    kih_ref[...] = -(f_hi_b + sign * b_hi_b) * inv_l

    @pl.when(pl.program_id(0) == 0)
    def _():
        mid_sign = 1.0 if M % 2 == 0 else -1.0
        sp_ref[...] = jnp.zeros_like(sp_ref)
        sp_ref[0:1, :] = 0.5 * kr_lo[0:1, :]
        sp_ref[1:2, :] = 0.5 * kr_hi[0:1, :]
        sp_ref[2:3, :] = (cols(b, 0)[0:1, :] + mid_sign * cols(b, 1)[0:1, :]) * inv_l
        sp_ref[3:4, :] = -(cols(b, 2)[0:1, :] + mid_sign * cols(b, 3)[0:1, :]) * inv_l


def _spectrum_call(dft, filters):
    cmat, smat, _, tc, ts = dft
    M = cmat.shape[0]
    w2 = filters.shape[1]
    w = w2 // 2
    samples = filters.reshape(M, 2 * w2).astype(BF16)
    tr = min(DFT_ROWS, M)
    row = lambda i: (i, 0)
    return pl.pallas_call(
        _spectrum_kernel,
        grid=(M // tr,),
        in_specs=[pl.BlockSpec((tr, M), row), pl.BlockSpec((tr, M), row),
                  pl.BlockSpec((M, 2 * w2), lambda i: (0, 0), pipeline_mode=pl.Buffered(1)),
                  pl.BlockSpec((tr, 1), row), pl.BlockSpec((tr, 1), row)],
        out_specs=[pl.BlockSpec((tr, w), row)] * 4 + [pl.BlockSpec((SUBLANES, w), lambda i: (0, 0))],
        out_shape=[jax.ShapeDtypeStruct((M, w), F32)] * 4 + [jax.ShapeDtypeStruct((SUBLANES, w), F32)],
        compiler_params=_cparams(1),
        name="hyena_filter_spectrum",
    )(cmat, smat, samples, tc, ts)


def _hy_prep_kernel(p_ref, w_ref, b_ref, z_ref):
    M = z_ref.shape[1]
    pe = p_ref[pl.ds(0, M, stride=2), :]
    po = p_ref[pl.ds(1, M, stride=2), :]
    row = lax.broadcasted_iota(jnp.int32, (M, 1), 0)
    po_prev = jnp.where(row == 0, 0.0, pltpu.roll(po, 1, 0))
    pe_next = jnp.where(row == M - 1, 0.0, pltpu.roll(pe, M - 1, 0))
    w0, w1, w2, bias = w_ref[0:1, :], w_ref[1:2, :], w_ref[2:3, :], b_ref[...]
    ze = po_prev * w0 + pe * w1 + po * w2 + bias
    zo = pe * w0 + po * w1 + pe_next * w2 + bias
    z_ref[0] = ze.astype(z_ref.dtype)
    z_ref[1] = zo.astype(z_ref.dtype)


def _hy_prep_call(p_hy, short_w, short_b, n_batch, L):
    N = n_batch * D_GROUP
    M = L // 2
    hb = D_GROUP // LANES
    return pl.pallas_call(
        _hy_prep_kernel,
        grid=(n_batch, (HY_ORDER + 1) * hb),
        in_specs=[
            pl.BlockSpec((L, LANES), lambda b, j: (b, j)),
            pl.BlockSpec((HY_SHORT, LANES), lambda b, j: (0, j)),
            pl.BlockSpec((1, LANES), lambda b, j: (0, j)),
        ],
        out_specs=pl.BlockSpec((None, 2, M, LANES), lambda b, j: (j // hb, 0, 0, b * hb + j % hb)),
        out_shape=jax.ShapeDtypeStruct((HY_ORDER + 1, 2, M, N), BF16),
        compiler_params=_cparams(2),
        name="hyena_short_conv",
    )(p_hy, short_w, short_b[None, :])


def _dft_fwd_kernel(c_ref, s_ref, ue_ref, uo_ref, tc_ref, ts_ref, krl_ref, kil_ref, krh_ref, kih_ref, sp_ref,
                    e_ref):
    tr = c_ref.shape[0]
    c, s = c_ref[...], s_ref[...]
    ue, uo = ue_ref[...], uo_ref[...]
    ae = jnp.dot(c, ue, preferred_element_type=F32)
    ao = jnp.dot(c, uo, preferred_element_type=F32)
    be = jnp.dot(s, ue, preferred_element_type=F32)
    bo = jnp.dot(s, uo, preferred_element_type=F32)
    tc, ts = tc_ref[...], ts_ref[...]
    first = (pl.program_id(1) * tr + lax.broadcasted_iota(jnp.int32, (tr, 1), 0)) == 0
    krl, kil, krh, kih = krl_ref[...], kil_ref[...], krh_ref[...], kih_ref[...]
    k0h, kLh, krm, kim = sp_ref[0:1, :], sp_ref[1:2, :], sp_ref[2:3, :], sp_ref[3:4, :]
    for g in range(ue.shape[1] // D_GROUP):
        sl = slice(g * D_GROUP, (g + 1) * D_GROUP)
        a1, b1, a2, b2 = _half_butterfly(ae[:, sl], ao[:, sl], be[:, sl], bo[:, sl], tc, ts)
        p1, q1 = krl * a1 + kil * b1, krl * b1 - kil * a1
        p2, q2 = krh * a2 + kih * b2, krh * b2 - kih * a2
        pm, qp = p1 - p2, q1 + q2
        dc, ny = k0h * a1, kLh * a2
        alt_e, alt_o = be[:, sl], bo[:, sl]
        e_ref[0, :, sl] = jnp.where(first, dc + ny, p1 + p2).astype(BF16)
        e_ref[1, :, sl] = jnp.where(first, krm * alt_e + kim * alt_o, q1 - q2).astype(BF16)
        e_ref[2, :, sl] = jnp.where(first, dc - ny, tc * pm + ts * qp).astype(BF16)
        e_ref[3, :, sl] = jnp.where(first, krm * alt_o - kim * alt_e, tc * qp - ts * pm).astype(BF16)


def _dft_fwd_call(dft, u_arr, u_sel, tabs, order):
    cmat, smat, _, tc, ts = dft
    _, _, M, N = u_arr.shape
    tr = min(DFT_ROWS, M)
    tn = min(DFT_COLS, N)
    row = lambda h, i: (i, 0)
    tab = lambda h, i: (i, order)
    u16 = u_arr
    return pl.pallas_call(
        _dft_fwd_kernel,
        grid=(N // tn, M // tr),
        in_specs=[
            pl.BlockSpec((tr, M), row),
            pl.BlockSpec((tr, M), row),
            pl.BlockSpec((None, None, M, tn), lambda h, i: (u_sel, 0, 0, h), pipeline_mode=pl.Buffered(1)),
            pl.BlockSpec((None, None, M, tn), lambda h, i: (u_sel, 1, 0, h), pipeline_mode=pl.Buffered(1)),
            pl.BlockSpec((tr, 1), row), pl.BlockSpec((tr, 1), row),
        ] + [pl.BlockSpec((tr, D_GROUP), tab)] * 4 + [pl.BlockSpec((SUBLANES, D_GROUP), lambda h, i: (0, order))],
        out_specs=pl.BlockSpec((4, tr, tn), lambda h, i: (0, i, h)),
        out_shape=jax.ShapeDtypeStruct((4, M, N), BF16),
        compiler_params=_cparams(2),
        name="hyena_dft_fwd",
    )(cmat, smat, u16, u16, tc, ts, *tabs)


def _dft_inv_kernel(c_ref, st_ref, ec_ref, es_ref, oc_ref, os_ref, u_ref, g_ref, bias_ref, y_ref):
    c, st = c_ref[...], st_ref[...]
    bias = bias_ref[...]
    conv_e = (jnp.dot(c, ec_ref[...], preferred_element_type=F32)
              + jnp.dot(st, es_ref[...], preferred_element_type=F32))
    conv_o = (jnp.dot(c, oc_ref[...], preferred_element_type=F32)
              + jnp.dot(st, os_ref[...], preferred_element_type=F32))
    for par, conv in enumerate((conv_e, conv_o)):
        y = g_ref[par].astype(F32) * (conv + u_ref[par].astype(F32) * bias)
        y_ref[par] = y.astype(y_ref.dtype)


def _dft_inv_call(dft, e16, u_arr, u_sel, g_arr, g_sel, bias_row, out_dtype):
    cmat, _, stmat, _, _ = dft
    _, M, N = e16.shape
    tr = min(DFT_ROWS, M)
    tn = min(DFT_COLS, N)
    row = lambda h, i: (i, 0)
    plane = lambda k: pl.BlockSpec((None, M, tn), lambda h, i: (k, 0, h), pipeline_mode=pl.Buffered(1))
    return pl.pallas_call(
        _dft_inv_kernel,
        grid=(N // tn, M // tr),
        in_specs=[
            pl.BlockSpec((tr, M), row),
            pl.BlockSpec((tr, M), row),
            plane(0), plane(1), plane(2), plane(3),
            pl.BlockSpec((None, 2, tr, tn), lambda h, i: (u_sel, 0, i, h)),
            pl.BlockSpec((None, 2, tr, tn), lambda h, i: (g_sel, 0, i, h)),
            pl.BlockSpec((1, tn), lambda h, i: (0, h)),
        ],
        out_specs=pl.BlockSpec((2, tr, tn), lambda h, i: (0, i, h)),
        out_shape=jax.ShapeDtypeStruct((2, M, N), out_dtype),
        compiler_params=_cparams(2),
        name="hyena_dft_inv",
    )(cmat, stmat, e16, e16, e16, e16, u_arr, g_arr, bias_row)


def _hyena_call(p_hy, short_w, short_b, filters, hy_bias, n_batch, L, dft):
    tabs = _spectrum_call(dft, filters)
    z = _hy_prep_call(p_hy, short_w, short_b, n_batch, L)
    u_arr, u_sel = z, HY_ORDER
    for n in range(HY_ORDER):
        e16 = _dft_fwd_call(dft, u_arr, u_sel, tabs, n)
        bias_row = jnp.tile(hy_bias[n][None, :].astype(F32), (1, n_batch))
        y = _dft_inv_call(dft, e16, u_arr, u_sel, z, n, bias_row, BF16 if n + 1 < HY_ORDER else F32)
        u_arr, u_sel = y[None], 0
    return y


def _mixers_kernel(gm_ref, cv_ref, cvp_ref, cvn_ref, lng_ref, lnb_ref, ws_ref, bsf_ref, dww_ref, dwb_ref,
                   cg_ref, cb_ref, avg_ref, ygm_ref, ycv_ref, glu_ref, *, tiles_per_seq):
    i = pl.program_id(0)
    tm = gm_ref.shape[0]
    lane = lax.broadcasted_iota(jnp.int32, (1, D_GROUP), 1)

    z = jax.nn.gelu(gm_ref[...], approximate=True)
    u, v = z[:, :D_GROUP], z[:, D_GROUP:]
    mu = jnp.mean(v, axis=-1, keepdims=True)
    vc = v - mu
    var = jnp.mean(vc * vc, axis=-1, keepdims=True)
    vn = (vc * lax.rsqrt(var + EPS) * lng_ref[...] + lnb_ref[...]).astype(BF16)
    hd = D_GROUP // GM_HEADS
    for c in range(tm // GM_CHUNK):
        rows = slice(c * GM_CHUNK, (c + 1) * GM_CHUNK)
        s = bsf_ref[...]
        for g in range(GM_HEADS):
            sg = jnp.dot(ws_ref[g], vn[rows, :], preferred_element_type=F32)
            s = s + jnp.where((lane >= g * hd) & (lane < (g + 1) * hd), sg, 0.0)
        ygm_ref[rows, :] = u[rows, :] * s

    def glu(t):
        return t[:, :D_GROUP] * jax.nn.sigmoid(t[:, D_GROUP:])

    first = (i % tiles_per_seq) == 0
    last = (i % tiles_per_seq) == tiles_per_seq - 1
    span = tm + 2 * CONV_HALO
    glu_ref[0, 0:CONV_HALO, :] = jnp.where(first, 0.0, glu(cvp_ref[...]))
    glu_ref[0, CONV_HALO:CONV_HALO + tm, :] = glu(cv_ref[...])
    glu_ref[0, CONV_HALO + tm:span, :] = jnp.where(last, 0.0, glu(cvn_ref[...]))
    glu_ref[0, span:, :] = jnp.zeros((SUBLANES, D_GROUP), F32)
    for b in range(1, SUBLANES):
        glu_ref[b, 0:span, :] = glu_ref[0, b:b + span, :]
    pad = (CV_WIDTH - 1) // 2
    rc = 128

    def group_mean(t):
        hi = t.astype(BF16)
        lo = (t - hi.astype(F32)).astype(BF16)
        return (jnp.dot(hi, avg_ref[...], preferred_element_type=F32)
                + jnp.dot(lo, avg_ref[...], preferred_element_type=F32))

    for c in range(tm // rc):
        acc = jnp.zeros((rc, D_GROUP), F32) + dwb_ref[...]
        for k in range(CV_WIDTH):
            start = c * rc + CONV_HALO - pad + k
            b = start % SUBLANES
            acc = acc + glu_ref[b, start - b:start - b + rc, :] * dww_ref[k:k + 1, :]
        d = acc - group_mean(acc)
        gvar = group_mean(d * d)
        n = d * lax.rsqrt(gvar + EPS) * cg_ref[...] + cb_ref[...]
        ycv_ref[c * rc:(c + 1) * rc, :] = n * jax.nn.sigmoid(n)


def _mixers_call(p_gm, p_cv, gm_ln_g, gm_ln_b, gm_ws, gm_bs, cv_dw_w, cv_dw_b, cv_ln_g, cv_ln_b, L, tm):
    T = p_gm.shape[0]
    tps = L // tm
    hb = tm // CONV_HALO
    n_hblk = T // CONV_HALO
    const2 = lambda i: (0, 0)
    tok = lambda i: (i, 0)
    bs_full = jnp.repeat(gm_bs.T.astype(F32), D_GROUP // GM_HEADS, axis=1)
    gid = np.arange(D_GROUP) // (D_GROUP // CV_GROUPS)
    avg = jnp.asarray((gid[:, None] == gid[None, :]).astype(np.float32) / (D_GROUP // CV_GROUPS)).astype(BF16)
    row = lambda a: a[None, :].astype(F32)
    return pl.pallas_call(
        functools.partial(_mixers_kernel, tiles_per_seq=tps),
        grid=(T // tm,),
        in_specs=[
            pl.BlockSpec((tm, GM_COLS), tok),
            pl.BlockSpec((tm, CV_COLS), tok),
            pl.BlockSpec((CONV_HALO, CV_COLS), lambda i: (jnp.maximum(i * hb - 1, 0), 0)),
            pl.BlockSpec((CONV_HALO, CV_COLS), lambda i: (jnp.minimum((i + 1) * hb, n_hblk - 1), 0)),
            pl.BlockSpec((1, D_GROUP), const2),
            pl.BlockSpec((1, D_GROUP), const2),
            pl.BlockSpec((GM_HEADS, GM_CHUNK, GM_CHUNK), lambda i: (0, 0, 0)),
            pl.BlockSpec((GM_CHUNK, D_GROUP), const2),
            pl.BlockSpec((CV_WIDTH, D_GROUP), const2),
            pl.BlockSpec((1, D_GROUP), const2),
            pl.BlockSpec((1, D_GROUP), const2),
            pl.BlockSpec((1, D_GROUP), const2),
            pl.BlockSpec((D_GROUP, D_GROUP), const2),
        ],
        out_specs=[pl.BlockSpec((tm, D_GROUP), tok)] * 2,
        out_shape=[jax.ShapeDtypeStruct((T, D_GROUP), F32)] * 2,
        scratch_shapes=[pltpu.VMEM((SUBLANES, tm + 2 * CONV_HALO + SUBLANES, D_GROUP), F32)],
        compiler_params=_cparams(1),
        name="gmlp_conv_mixers",
    )(p_gm, p_cv, p_cv, p_cv, row(gm_ln_g), row(gm_ln_b), gm_ws.astype(BF16), bs_full, cv_dw_w.astype(F32),
      row(cv_dw_b), row(cv_ln_g), row(cv_ln_b), avg)


def _out_kernel(hye_ref, hyo_ref, gm_ref, cv_ref, at_ref, x_ref, mg_ref, wo_ref, g1_ref, a2_ref, sh2_ref, wrh_ref,
                wrl_ref, rb_ref, tri_ref, cin_ref, xm_ref, hfx_ref, cls_ref, rank_ref, cnt_ref, carry_ref, hy_ref):
    i = pl.program_id(0)
    tm = x_ref.shape[0]
    for k in range(D_GROUP // LANES):
        hy_ref[k, pl.ds(0, tm // 2, stride=2), :] = hye_ref[:, k * LANES:(k + 1) * LANES]
        hy_ref[k, pl.ds(1, tm // 2, stride=2), :] = hyo_ref[:, k * LANES:(k + 1) * LANES]
    y_hy = jnp.concatenate([hy_ref[k] for k in range(D_GROUP // LANES)], axis=1)
    o = None
    for g, y_src in enumerate((y_hy, gm_ref, cv_ref, at_ref)):
        y = y_src[...]
        n = y * lax.rsqrt(jnp.mean(y * y, axis=-1, keepdims=True) + EPS) * mg_ref[:, g * D_GROUP:(g + 1) * D_GROUP]
        part = jnp.dot(n.astype(BF16), wo_ref[g * D_GROUP:(g + 1) * D_GROUP, :], preferred_element_type=F32)
        o = part if o is None else o + part
    xm = x_ref[...] + g1_ref[...] * o
    xm_ref[...] = xm
    hf = xm * lax.rsqrt(jnp.mean(xm * xm, axis=-1, keepdims=True) + EPS) * a2_ref[...] + sh2_ref[...]

    hf_hi = hf.astype(BF16)
    hf_lo = (hf - hf_hi.astype(F32)).astype(BF16)
    logits = (jnp.dot(hf_hi, wrh_ref[...], preferred_element_type=F32)
              + jnp.dot(hf_lo, wrh_ref[...], preferred_element_type=F32)
              + jnp.dot(hf_hi, wrl_ref[...], preferred_element_type=F32))
    cls, g_lo, g_hi = _route_top2(jnp.transpose(logits), rb_ref[...])
    cls_ref[...] = cls

    d_model = hf.shape[1]
    hfx_ref[:, :d_model] = hf
    gate_rows = jnp.concatenate([g_lo, g_hi, jnp.zeros((LANES - TOP_K, tm), F32)], axis=0)
    hfx_ref[:, d_model:] = jnp.transpose(gate_rows)

    @pl.when(i == 0)
    def _():
        carry_ref[...] = cin_ref[...]

    sub = lax.broadcasted_iota(jnp.int32, (CLASS_ROWS, tm), 0)
    onehot = sub == cls
    prefix = jnp.dot(onehot.astype(BF16), tri_ref[...], preferred_element_type=F32)
    carry = carry_ref[...]
    rank = jnp.sum(jnp.where(onehot, prefix + carry[:, 0:1], 0.0), axis=0, keepdims=True)
    rank_ref[...] = rank.astype(jnp.int32)
    carry = carry + jnp.sum(onehot.astype(F32), axis=1, keepdims=True)
    carry_ref[...] = carry
    cnt_ref[...] = carry


def _out_alias_kernel(*refs):
    _out_kernel(*refs[1:])


def _out_call(y_hy_t, y_gm, y_cv, y_at, xt, mixg, wo, g1, a2, sh2, mod_map, wr_hi, wr_lo, rbias, counts_in, L, tm,
              hfx_prev, t_total, row0):
    T, D = xt.shape
    tps = L // tm
    blk0 = row0 // tm
    const = lambda i: (0, 0)
    tok = lambda i: (i, 0)
    lane_tok = lambda i: (0, i)
    tri = jnp.asarray(np.triu(np.ones((tm, tm), np.float32), k=1)).astype(BF16)
    W = D + LANES
    parity = lambda par: pl.BlockSpec((None, tm // 2, D_GROUP), lambda i: (par, i % tps, i // tps))
    in_specs = [parity(0), parity(1)] + [pl.BlockSpec((tm, D_GROUP), tok)] * 3 + [
        pl.BlockSpec((tm, D), tok),
        pl.BlockSpec(mixg.shape, const),
        pl.BlockSpec(wo.shape, const),
        pl.BlockSpec((None, 1, D), mod_map),
        pl.BlockSpec((None, 1, D), mod_map),
        pl.BlockSpec((None, 1, D), mod_map),
        pl.BlockSpec(wr_hi.shape, const),
        pl.BlockSpec(wr_lo.shape, const),
        pl.BlockSpec(rbias.shape, const),
        pl.BlockSpec((tm, tm), const),
        pl.BlockSpec((CLASS_ROWS, LANES), const),
    ]
    args = (y_hy_t, y_hy_t, y_gm, y_cv, y_at, xt, mixg, wo, g1, a2, sh2, wr_hi, wr_lo, rbias, tri, counts_in)
    aliased = hfx_prev is not None
    return pl.pallas_call(
        _out_alias_kernel if aliased else _out_kernel,
        grid=(T // tm,),
        in_specs=([pl.BlockSpec(memory_space=pl.ANY)] if aliased else []) + in_specs,
        out_specs=[pl.BlockSpec((tm, D), tok), pl.BlockSpec((tm, W), lambda i: (blk0 + i, 0)),
                   pl.BlockSpec((1, tm), lane_tok), pl.BlockSpec((1, tm), lane_tok),
                   pl.BlockSpec((CLASS_ROWS, LANES), const)],
        out_shape=[jax.ShapeDtypeStruct((T, D), F32), jax.ShapeDtypeStruct((t_total, W), F32),
                   jax.ShapeDtypeStruct((1, T), jnp.int32), jax.ShapeDtypeStruct((1, T), jnp.int32),
                   jax.ShapeDtypeStruct((CLASS_ROWS, LANES), F32)],
        scratch_shapes=[pltpu.VMEM((CLASS_ROWS, LANES), F32), pltpu.VMEM((D_GROUP // LANES, tm, LANES), F32)],
        input_output_aliases={0: 1} if aliased else {},
        compiler_params=_cparams(1),
        name="mix_out_norm2_route",
    )(*(((hfx_prev,) if aliased else ()) + args))


def _first_max_flags(vals):
    m = functools.reduce(jnp.maximum, vals)
    flags, taken = [], None
    for v in vals:
        f = v >= m
        if taken is not None:
            f = f & jnp.logical_not(taken)
        flags.append(f)
        taken = f if taken is None else taken | f
    return flags, m


def _pick(flags, vals):
    out = vals[-1]
    for f, v in zip(flags[-2::-1], vals[-2::-1]):
        out = jnp.where(f, v, out)
    return out


def _route_top2(lt, bias_col):
    s_all = jax.nn.sigmoid(lt[:N_EXPERTS, :])
    sel_all = s_all + bias_col
    s = [s_all[e:e + 1, :] for e in range(N_EXPERTS)]
    sel = [sel_all[e:e + 1, :] for e in range(N_EXPERTS)]
    neg = -jnp.inf
    E = EXPERTS_PER_GROUP

    def top2(vals):
        f1, m1 = _first_max_flags(vals)
        rest = [jnp.where(f, neg, v) for f, v in zip(f1, vals)]
        f2, m2 = _first_max_flags(rest)
        return f1, m1, f2, m2

    scores = []
    for g in range(N_EXPERT_GROUPS):
        _, m1, _, m2 = top2(sel[g * E:(g + 1) * E])
        scores.append(m1 + m2)
    gflags, _ = _first_max_flags(scores)
    bsel = [_pick(gflags, [sel[g * E + j] for g in range(N_EXPERT_GROUPS)]) for j in range(E)]
    bs = [_pick(gflags, [s[g * E + j] for g in range(N_EXPERT_GROUPS)]) for j in range(E)]
    f1, _, f2, _ = top2(bsel)
    zero = jnp.zeros_like(bs[0])
    w1 = functools.reduce(jnp.add, [jnp.where(f, v, zero) for f, v in zip(f1, bs)])
    w2 = functools.reduce(jnp.add, [jnp.where(f, v, zero) for f, v in zip(f2, bs)])
    izero = jnp.zeros(w1.shape, jnp.int32)
    j1 = functools.reduce(jnp.add, [jnp.where(f, j, izero) for j, f in enumerate(f1)])
    j2 = functools.reduce(jnp.add, [jnp.where(f, j, izero) for j, f in enumerate(f2)])
    gi = functools.reduce(jnp.add, [jnp.where(f, g, izero) for g, f in enumerate(gflags)])
    lo, hi = jnp.minimum(j1, j2), jnp.maximum(j1, j2)
    pair = jnp.where(lo == 0, 0, jnp.where(lo == 1, 3, 5)) + hi - lo - 1
    tot = w1 + w2
    first_is_lo = j1 < j2
    return gi * N_PAIRS + pair, jnp.where(first_is_lo, w1, w2) / tot, jnp.where(first_is_lo, w2, w1) / tot


def _table_lookup(table, idx):
    n = table.shape[0]
    hit = idx[..., None] == jnp.arange(n, dtype=jnp.int32)
    return jnp.sum(jnp.where(hit, table, 0), axis=-1)


def _class_plan(cls, rank, counts, tile):
    T = cls.shape[0]
    n_tiles = (T + N_CLASSES * tile) // tile
    order = jnp.argsort(cls, stable=True).astype(jnp.int32)
    padded = ((counts + tile - 1) // tile) * tile
    seg_end = jnp.cumsum(padded)
    seg_start = seg_end - padded
    src_start = jnp.cumsum(counts) - counts
    tok_row = _table_lookup(seg_start, cls) + rank

    tile_first = jnp.arange(n_tiles, dtype=jnp.int32) * tile
    tile_class = jnp.minimum(jnp.sum((seg_end[None, :] <= tile_first[:, None]).astype(jnp.int32), axis=1),
                             N_CLASSES - 1)
    off = tile_first - _table_lookup(seg_start, tile_class)
    tile_cnt = jnp.clip(_table_lookup(counts, tile_class) - off, 0, tile)
    tile_src = jnp.clip(_table_lookup(src_start, tile_class) + off, 0, T - 1)
    grp = tile_class // N_PAIRS
    pair = tile_class % N_PAIRS
    tile_lo = grp * EXPERTS_PER_GROUP + _table_lookup(jnp.asarray(PAIR_LO, jnp.int32), pair)
    tile_hi = grp * EXPERTS_PER_GROUP + _table_lookup(jnp.asarray(PAIR_HI, jnp.int32), pair)
    return order, tile_lo, tile_hi, tile_src, tile_cnt, tok_row


def _moe_pair_kernel(order_ref, lo_ref, hi_ref, src_ref, cnt_ref, hfx_hbm, wga_ref, wua_ref, wda_ref, wgb_ref,
                     wub_ref, wdb_ref, o_ref, buf_ref, sem_ref):
    i = pl.program_id(0)
    n = pl.num_programs(0)
    tile = o_ref.shape[0]
    n_tok = order_ref.shape[0]
    slot = i % 2

    def start(t, s):
        base = src_ref[t]
        _start_row_gather(hfx_hbm, lambda j: order_ref[jnp.minimum(base + j, n_tok - 1)], tile, buf_ref, s, sem_ref)

    @pl.when((i == 0) & (cnt_ref[0] > 0))
    def _():
        start(0, 0)

    @pl.when((cnt_ref[i] <= 0) & (i > 0) & (cnt_ref[jnp.maximum(i - 1, 0)] > 0))
    def _():
        _wait_row_gather(hfx_hbm, tile, buf_ref, slot, sem_ref)

    @pl.when(cnt_ref[i] > 0)
    def _():
        start(jnp.minimum(i + 1, n - 1), 1 - slot)
        _wait_row_gather(hfx_hbm, tile, buf_ref, slot, sem_ref)
        d_model = buf_ref.shape[2] - LANES
        x = buf_ref[slot, :, :d_model].astype(BF16)
        live = lax.broadcasted_iota(jnp.int32, (tile, 1), 0) < cnt_ref[i]
        gates = buf_ref[slot, :, d_model:]
        g_lo = jnp.where(live, gates[:, 0:1], 0.0)
        g_hi = jnp.where(live, gates[:, 1:2], 0.0)

        def hidden(wg_ref, wu_ref, gate):
            hg = jnp.dot(x, wg_ref[...], preferred_element_type=F32)
            hu = jnp.dot(x, wu_ref[...], preferred_element_type=F32)
            return (hg * jax.nn.sigmoid(hg) * hu * gate).astype(BF16)

        o_ref[...] = (jnp.dot(hidden(wga_ref, wua_ref, g_lo), wda_ref[...], preferred_element_type=F32)
                      + jnp.dot(hidden(wgb_ref, wub_ref, g_hi), wdb_ref[...], preferred_element_type=F32))

    @pl.when(cnt_ref[i] <= 0)
    def _():
        o_ref[...] = jnp.zeros_like(o_ref)


def _moe_pair_call(hfx, order, tile_lo, tile_hi, tile_src, tile_cnt, wg, wu, wd, tile):
    T, W = hfx.shape
    D = W - LANES
    F = wg.shape[-1]
    n_tiles = tile_lo.shape[0]
    lo = lambda i, o, tl, th, ts, tc: (tl[i], 0, 0)
    hi = lambda i, o, tl, th, ts, tc: (th[i], 0, 0)
    grid_spec = pltpu.PrefetchScalarGridSpec(
        num_scalar_prefetch=5,
        grid=(n_tiles,),
        in_specs=[
            pl.BlockSpec(memory_space=pl.ANY),
            pl.BlockSpec((None, D, F), lo), pl.BlockSpec((None, D, F), lo), pl.BlockSpec((None, F, D), lo),
            pl.BlockSpec((None, D, F), hi), pl.BlockSpec((None, D, F), hi), pl.BlockSpec((None, F, D), hi),
        ],
        out_specs=pl.BlockSpec((tile, D), lambda i, o, tl, th, ts, tc: (i, 0)),
        scratch_shapes=[pltpu.VMEM((2, tile, W), F32), pltpu.SemaphoreType.DMA((2,))],
    )
    return pl.pallas_call(
        _moe_pair_kernel,
        grid_spec=grid_spec,
        out_shape=jax.ShapeDtypeStruct((n_tiles * tile, D), F32),
        compiler_params=_cparams(1),
        name="moe_pair_grouped",
    )(order, tile_lo, tile_hi, tile_src, tile_cnt, hfx, wg, wu, wd, wg, wu, wd)


def _final_kernel(idx_ref, xm_ref, g2_ref, ys_hbm, g_ref, o_ref, buf_ref, sem_ref):
    x = xm_ref[...] + g2_ref[...] * _gathered_rows(idx_ref, ys_hbm, buf_ref, sem_ref, xm_ref.shape[0])
    o_ref[...] = x * lax.rsqrt(jnp.mean(x * x, axis=-1, keepdims=True) + EPS) * g_ref[...]
    _drain_gathered_rows(ys_hbm, buf_ref, sem_ref, xm_ref.shape[0])


def _final_call(x_mid, tok_row, g2, mod_map, ys, g, tm):
    T, D = x_mid.shape
    tok = lambda i, *_: (i, 0)
    grid_spec = pltpu.PrefetchScalarGridSpec(
        num_scalar_prefetch=1,
        grid=(T // tm,),
        in_specs=[pl.BlockSpec((tm, D), tok), pl.BlockSpec((None, 1, D), lambda i, *_: mod_map(i)),
                  pl.BlockSpec(memory_space=pl.ANY), pl.BlockSpec((1, D), lambda i, *_: (0, 0))],
        out_specs=pl.BlockSpec((tm, D), tok),
        scratch_shapes=[pltpu.VMEM((2, tm, ys.shape[1]), ys.dtype), pltpu.SemaphoreType.DMA((2,))],
    )
    return pl.pallas_call(
        _final_kernel,
        grid_spec=grid_spec,
        out_shape=jax.ShapeDtypeStruct((T, D), F32),
        compiler_params=_cparams(1),
        name="moe_residual_final_norm",
    )(tok_row, x_mid, g2, ys, g[None, :].astype(F32))


def kernel(x, c, ctx, c_ctx, ada_w, ada_b, norm1_g, norm2_g, w_in, hy_short_w, hy_short_b, hy_f_w1, hy_f_b1, hy_f_freq, hy_f_w2, hy_f_b2, hy_f_w3, hy_bias, gm_ln_g, gm_ln_b, gm_ws, gm_bs, cv_dw_w, cv_dw_b, cv_ln_g, cv_ln_b, mla_qa_norm, w_uq, mla_kva_norm, w_ukv, mix_norm_g, w_out, w_router, router_bias, exp_w_gate, exp_w_up, exp_w_down, final_norm_g):
    B, n_lat, D = x.shape
    n_ctx = ctx.shape[1]
    T_lat, T_ctx = B * n_lat, B * n_ctx
    tm_lat = min(512, n_lat)
    tm_ctx = min(256, n_ctx)
    assert n_lat % tm_lat == 0 and n_ctx % tm_ctx == 0 and n_lat % ATT_TILE == 0
    assert tm_lat % GM_CHUNK == 0 and tm_ctx % GM_CHUNK == 0 and T_lat % MOE_TILE == 0 and T_ctx % MOE_TILE == 0

    tps_lat, tps_ctx = n_lat // tm_lat, n_ctx // tm_ctx
    dft = {L: _half_dft(L) for L in {n_lat, n_ctx}}
    rope_lat = _rope_slot_tables(n_lat)
    rope_ctx = _identity_slot_tables(tm_ctx)
    wr_pad = jnp.pad(w_router.astype(F32), ((0, 0), (0, LANES - N_EXPERTS)))
    wr_hi = wr_pad.astype(BF16)
    wr_lo = (wr_pad - wr_hi.astype(F32)).astype(BF16)
    rbias = router_bias.astype(F32)[:, None]
    cond = jnp.concatenate([c, c_ctx[None, :]], axis=0)
    streams = {
        'lat': dict(x=x.reshape(T_lat, D), L=n_lat, tm=tm_lat, mod_map=lambda i: (i // tps_lat, 0, 0),
                    rope=rope_lat, rope_map=lambda i: (i % tps_lat, 0), row0=0, moe_in=None),
        'ctx': dict(x=ctx.reshape(T_ctx, D), L=n_ctx, tm=tm_ctx, mod_map=lambda i: (B, 0, 0),
                    rope=rope_ctx, rope_map=lambda i: (0, 0), row0=T_lat, moe_in=None),
    }

    for l in range(DEPTH):
        last = l == DEPTH - 1
        m = jax.nn.silu(cond) @ ada_w[l] + ada_b[l]
        sh1, sc1, g1, sh2, sc2, g2 = [t[:, None, :] for t in jnp.split(m, 6, axis=-1)]
        a1 = norm1_g[l][None, None, :] * (1.0 + sc1)
        a2 = norm2_g[l][None, None, :] * (1.0 + sc2)
        qg = jnp.pad(mla_qa_norm[l], (0, PQ_PAD - MLA_Q_RANK))[None, :].astype(F32)
        kvg = mla_kva_norm[l][None, :].astype(F32)
        win, wuq, wkv = _pack_w_in(w_in[l]), _pack_w_uq(w_uq[l]), _pack_w_ukv(w_ukv[l])
        filt = (hy_f_w1[l], hy_f_b1[l], hy_f_freq[l], hy_f_w2[l], hy_f_b2[l], hy_f_w3[l])
        wg, wu, wd = exp_w_gate[l].astype(BF16), exp_w_up[l].astype(BF16), exp_w_down[l].astype(BF16)

        proj = {name: _proj_call(s['x'], a1, sh1, s['mod_map'], win, qg, wuq, kvg, wkv, s['rope'], s['rope_map'],
                                 s['tm'], s['moe_in']) for name, s in streams.items()}
        for name, s in streams.items():
            if s['moe_in'] is not None:
                s['x'] = proj[name][6]
        k_ctx, v_ctx = proj['ctx'][4], proj['ctx'][5]

        active = ('lat',) if last else ('lat', 'ctx')
        x_mid, cls, rank = {}, {}, {}
        t_moe = T_lat if last else T_lat + T_ctx
        hfx = None if last else jnp.zeros((t_moe, D + LANES), F32)
        counts = jnp.zeros((CLASS_ROWS, LANES), F32)
        for name in active:
            s = streams[name]
            p_hy, p_gm, p_cv, q, k, v = proj[name][:6]
            if name == 'lat':
                y_at = _attn_lat_call(q, k, v, k_ctx, v_ctx, B, n_lat, n_ctx)
            else:
                y_at = _attn_ctx_call(q, k, v, B, n_ctx)
            filters = _hyena_filters(s['L'], *filt)
            y_hy_t = _hyena_call(p_hy, hy_short_w[l], hy_short_b[l], filters, hy_bias[l], B, s['L'], dft[s['L']])
            y_gm, y_cv = _mixers_call(p_gm, p_cv, gm_ln_g[l], gm_ln_b[l], gm_ws[l], gm_bs[l], cv_dw_w[l],
                                      cv_dw_b[l], cv_ln_g[l], cv_ln_b[l], s['L'], s['tm'])
            x_mid[name], hfx, cls[name], rank[name], counts = _out_call(
                y_hy_t, y_gm, y_cv, y_at, s['x'], mix_norm_g[l][None, :].astype(F32), w_out[l].astype(BF16),
                g1, a2, sh2, s['mod_map'], wr_hi, wr_lo, rbias, counts, s['L'], s['tm'],
                hfx, t_moe, s['row0'])

        cat = lambda d: jnp.concatenate([d[name][0] for name in active], axis=0)
        order, tile_lo, tile_hi, tile_src, tile_cnt, tok_row = _class_plan(
            cat(cls), cat(rank), counts[:N_CLASSES, 0].astype(jnp.int32), MOE_TILE)
        ys = _moe_pair_call(hfx, order, tile_lo, tile_hi, tile_src, tile_cnt, wg, wu, wd, MOE_TILE)
        for name in active:
            s = streams[name]
            s['x'] = x_mid[name]
            s['moe_in'] = (lax.slice_in_dim(tok_row, s['row0'], s['row0'] + x_mid[name].shape[0]), g2, ys)

    s = streams['lat']
    tok_row, g2, ys = s['moe_in']
    return _final_call(s['x'], tok_row, g2, s['mod_map'], ys, final_norm_g, s['tm']).reshape(B, n_lat, D)
```

```python
import functools
import math

import jax
import jax.numpy as jnp
import numpy as np
from jax import lax
from jax.experimental import pallas as pl
from jax.experimental.pallas import tpu as pltpu

F32 = jnp.float32
BF16 = jnp.bfloat16

D_MODEL = 1024
DEPTH = 2
GRID_W = 64
EPS = 1e-6

D_GROUP = 256
N_MIXERS = 4
HY_ORDER = 2
HY_SHORT = 3
HY_EMB = 33
HY_BANDS = (HY_EMB - 1) // 2
HY_TARGET = 1e-2
HY_FAST_PCT = 0.3
HY_SLOW_PCT = 1.5
GM_CHUNK = 128
GM_HEADS = 4
CV_WIDTH = 31
CV_GROUPS = 4
MLA_HEADS = 4
MLA_NOPE = 64
MLA_ROPE = 32
MLA_V = 64
MLA_Q_RANK = 192
MLA_KV_RANK = 128
ROPE_BASE = 10000.0
N_EXPERTS = 16
N_EXPERT_GROUPS = 4
EXPERTS_PER_GROUP = N_EXPERTS // N_EXPERT_GROUPS
TOP_K = 2
D_EXPERT = 512

HY_COLS = (HY_ORDER + 1) * D_GROUP
GM_COLS = 2 * D_GROUP
CV_COLS = 2 * D_GROUP
MQ_COLS = MLA_Q_RANK
MKV_COLS = MLA_KV_RANK + MLA_ROPE
HY_OFF = 0
GM_OFF = HY_OFF + HY_COLS
CV_OFF = GM_OFF + GM_COLS
MQ_OFF = CV_OFF + CV_COLS
MKV_OFF = MQ_OFF + MQ_COLS
IN_COLS = MKV_OFF + MKV_COLS

LANES = 128
SUBLANES = 8
HEAD_SLOT = LANES
QK_COLS = MLA_HEADS * HEAD_SLOT
V_COLS = MLA_HEADS * MLA_V
PQ_OFF = MQ_OFF
PQ_PAD = 256
PKV_OFF = PQ_OFF + PQ_PAD
PKR_OFF = PKV_OFF + MLA_KV_RANK
PROJ_COLS = PKR_OFF + LANES

PAIR_LO = (0, 0, 0, 1, 1, 2)
PAIR_HI = (1, 2, 3, 2, 3, 3)
N_PAIRS = len(PAIR_LO)
N_CLASSES = N_EXPERT_GROUPS * N_PAIRS
CLASS_ROWS = -(-N_CLASSES // SUBLANES) * SUBLANES

ATT_TILE = 512
ATT_SUB = 256
ATT_SCORE_BUFS = 2
MOE_TILE = 256
DFT_ROWS = 256
DFT_COLS = 1024
CONV_HALO = 16
VMEM_LIMIT = 56 * 1024 * 1024


def _cparams(n_axes):
    return pltpu.CompilerParams(dimension_semantics=("arbitrary",) * n_axes, vmem_limit_bytes=VMEM_LIMIT)


def _axial_rope_tables(n_lat):
    rows = n_lat // GRID_W
    row = jnp.repeat(jnp.arange(rows), GRID_W).astype(F32)
    col = jnp.tile(jnp.arange(GRID_W), rows).astype(F32)
    n_freq = MLA_ROPE // 4
    inv = ROPE_BASE ** (-jnp.arange(n_freq, dtype=F32) / n_freq)
    ang = jnp.concatenate([row[:, None] * inv, col[:, None] * inv], axis=-1)
    return jnp.cos(ang), jnp.sin(ang)


def _hyena_filters(L, w1, b1, freq, w2, b2, w3):
    w1, b1, freq, w2, b2, w3 = (a.astype(F32) for a in (w1, b1, freq, w2, b2, w3))
    t_all = jnp.linspace(0.0, 1.0, L, dtype=F32)
    f = jnp.linspace(1e-4, HY_BANDS - 1, HY_BANDS, dtype=F32)[None, :]
    deltas = jnp.abs(jnp.linspace(math.log(HY_TARGET) / HY_SLOW_PCT,
                                  math.log(HY_TARGET) / HY_FAST_PCT, D_GROUP, dtype=F32))

    def taps(pos, direction):
        t = t_all[pos][:, None]
        w = 2.0 * math.pi * pos.astype(F32)[:, None] / L
        z = jnp.concatenate([t, jnp.cos(f * w), -jnp.sin(f * w)], axis=-1)
        h = jnp.sin(freq * (z @ w1 + b1))
        h = jnp.sin(freq * (h @ w2 + b2))
        h = (h @ w3).reshape(L, HY_ORDER, 2, D_GROUP)[:, :, direction]
        return h * jnp.exp(-t * deltas)[:, None, :]

    m = jnp.arange(L, dtype=jnp.int32)
    kf = taps(m, 0)
    kb = jnp.where((m > 0)[:, None, None], taps((L - m) % L, 1), 0.0)
    norm = jnp.sum(jnp.abs(kf), axis=0, keepdims=True) + jnp.sum(jnp.abs(kb), axis=0, keepdims=True)
    w_cols = HY_ORDER * D_GROUP
    return jnp.concatenate([(kf / norm).reshape(L, w_cols), (kb / norm).reshape(L, w_cols)], axis=1)


def _pack_w_in(w_in):
    D = w_in.shape[0]
    z = lambda n: jnp.zeros((D, n), w_in.dtype)
    return jnp.concatenate([
        w_in[:, :MQ_OFF],
        w_in[:, MQ_OFF:MKV_OFF], z(PQ_PAD - MQ_COLS),
        w_in[:, MKV_OFF:MKV_OFF + MLA_KV_RANK],
        w_in[:, MKV_OFF + MLA_KV_RANK:], z(LANES - MLA_ROPE),
    ], axis=1).astype(BF16)


def _pack_w_uq(w_uq):
    w = w_uq.reshape(MLA_Q_RANK, MLA_HEADS, MLA_NOPE + MLA_ROPE)
    w = jnp.pad(w, ((0, PQ_PAD - MLA_Q_RANK), (0, 0), (0, HEAD_SLOT - MLA_NOPE - MLA_ROPE)))
    return w.reshape(PQ_PAD, QK_COLS).astype(BF16)


def _pack_w_ukv(w_ukv):
    w = w_ukv.reshape(MLA_KV_RANK, MLA_HEADS, MLA_NOPE + MLA_V)
    k_part = jnp.pad(w[:, :, :MLA_NOPE], ((0, 0), (0, 0), (0, HEAD_SLOT - MLA_NOPE))).reshape(MLA_KV_RANK, QK_COLS)
    v_part = w[:, :, MLA_NOPE:].reshape(MLA_KV_RANK, V_COLS)
    top = jnp.concatenate([k_part, v_part], axis=1)
    eye = jnp.eye(LANES, dtype=w_ukv.dtype)[:, :MLA_ROPE]
    place = jnp.pad(eye, ((0, 0), (MLA_NOPE, HEAD_SLOT - MLA_NOPE - MLA_ROPE)))
    bot = jnp.concatenate([jnp.tile(place, (1, MLA_HEADS)), jnp.zeros((LANES, V_COLS), w_ukv.dtype)], axis=1)
    return jnp.concatenate([top, bot], axis=0).astype(BF16)


def _rope_slot_tables(n_lat):
    cos, sin = _axial_rope_tables(n_lat)
    half = MLA_ROPE // 2
    tail_w = HEAD_SLOT - MLA_NOPE - MLA_ROPE
    ones = lambda n: jnp.ones((n_lat, n), F32)
    zeros = lambda n: jnp.zeros((n_lat, n), F32)
    cf = jnp.concatenate([ones(MLA_NOPE), cos, cos, ones(tail_w)], axis=1)
    s_up = jnp.concatenate([zeros(MLA_NOPE + half), sin, zeros(tail_w)], axis=1)
    s_dn = jnp.concatenate([zeros(MLA_NOPE), -sin, zeros(half + tail_w)], axis=1)
    return cf, s_up, s_dn


def _identity_slot_tables(rows):
    return (jnp.ones((rows, HEAD_SLOT), F32), jnp.zeros((rows, HEAD_SLOT), F32), jnp.zeros((rows, HEAD_SLOT), F32))


def _rotate_slots(t, cf, s_up, s_dn):
    half = MLA_ROPE // 2
    outs = []
    for h in range(MLA_HEADS):
        tb = t[:, h * HEAD_SLOT:(h + 1) * HEAD_SLOT]
        outs.append(tb * cf + pltpu.roll(tb, half, 1) * s_up + pltpu.roll(tb, HEAD_SLOT - half, 1) * s_dn)
    return jnp.concatenate(outs, axis=1)


def _row_gather_copy(src_hbm, row, buf_ref, slot, j, sem_ref):
    return pltpu.make_async_copy(src_hbm.at[pl.ds(row, 1)], buf_ref.at[slot, pl.ds(j, 1)], sem_ref.at[slot])


def _start_row_gather(src_hbm, row_of, n_rows, buf_ref, slot, sem_ref):
    for j in range(n_rows):
        _row_gather_copy(src_hbm, row_of(j), buf_ref, slot, j, sem_ref).start()


def _wait_row_gather(src_hbm, n_rows, buf_ref, slot, sem_ref):
    pltpu.make_async_copy(src_hbm.at[pl.ds(0, n_rows)], buf_ref.at[slot], sem_ref.at[slot]).wait()


def _gathered_rows(idx_ref, ys_hbm, buf_ref, sem_ref, tm):
    i = pl.program_id(0)
    n = pl.num_programs(0)
    slot = i % 2

    @pl.when(i == 0)
    def _():
        _start_row_gather(ys_hbm, lambda j: idx_ref[j], tm, buf_ref, 0, sem_ref)

    nxt = jnp.minimum(i + 1, n - 1)
    _start_row_gather(ys_hbm, lambda j: idx_ref[nxt * tm + j], tm, buf_ref, 1 - slot, sem_ref)
    _wait_row_gather(ys_hbm, tm, buf_ref, slot, sem_ref)
    return buf_ref[slot]


def _drain_gathered_rows(ys_hbm, buf_ref, sem_ref, tm):
    i = pl.program_id(0)

    @pl.when(i == pl.num_programs(0) - 1)
    def _():
        _wait_row_gather(ys_hbm, tm, buf_ref, 1 - i % 2, sem_ref)


def _proj_gather_kernel(idx_ref, xm_ref, g2_ref, ys_hbm, a_ref, sh_ref, win_ref, qg_ref, wuq_ref, kvg_ref, wkv_ref,
                        cf_ref, su_ref, sd_ref, hy_ref, gm_ref, cv_ref, q_ref, k_ref, v_ref, xn_ref, buf_ref, sem_ref,
                        p_ref):
    y = _gathered_rows(idx_ref, ys_hbm, buf_ref, sem_ref, xm_ref.shape[0])
    xn_ref[...] = xm_ref[...] + g2_ref[...] * y
    _proj_kernel(xn_ref, a_ref, sh_ref, win_ref, qg_ref, wuq_ref, kvg_ref, wkv_ref, cf_ref, su_ref, sd_ref,
                 hy_ref, gm_ref, cv_ref, q_ref, k_ref, v_ref, p_ref)
    _drain_gathered_rows(ys_hbm, buf_ref, sem_ref, xm_ref.shape[0])


def _proj_kernel(x_ref, a_ref, sh_ref, win_ref, qg_ref, wuq_ref, kvg_ref, wkv_ref, cf_ref, su_ref, sd_ref,
                 hy_ref, gm_ref, cv_ref, q_ref, k_ref, v_ref, p_ref):
    hs = x_ref.shape[0] // 2
    for half in range(2):
        x = x_ref[half * hs:(half + 1) * hs, :]
        ms = jnp.mean(x * x, axis=-1, keepdims=True)
        h = x * lax.rsqrt(ms + EPS) * a_ref[...] + sh_ref[...]
        p_ref[half] = jnp.dot(h.astype(BF16), win_ref[...], preferred_element_type=F32)

    for half in range(2):
        rows = slice(half * hs, (half + 1) * hs)
        hy_ref[rows, :] = p_ref[half, :, HY_OFF:GM_OFF]
        gm_ref[rows, :] = p_ref[half, :, GM_OFF:CV_OFF]
        cv_ref[rows, :] = p_ref[half, :, CV_OFF:MQ_OFF]

        cf, su, sd = cf_ref[rows, :], su_ref[rows, :], sd_ref[rows, :]
        cq = p_ref[half, :, PQ_OFF:PQ_OFF + PQ_PAD]
        qn = cq * lax.rsqrt(jnp.sum(cq * cq, axis=-1, keepdims=True) * (1.0 / MLA_Q_RANK) + EPS) * qg_ref[...]
        q = jnp.dot(qn.astype(BF16), wuq_ref[...], preferred_element_type=F32)
        q = _rotate_slots(q, cf, su, sd) * (math.log2(math.e) / math.sqrt(MLA_NOPE + MLA_ROPE))
        q_ref[rows, :] = q.astype(BF16)

        ckv = p_ref[half, :, PKV_OFF:PKV_OFF + MLA_KV_RANK]
        kvn = ckv * lax.rsqrt(jnp.mean(ckv * ckv, axis=-1, keepdims=True) + EPS) * kvg_ref[...]
        kin = jnp.concatenate([kvn, p_ref[half, :, PKR_OFF:PKR_OFF + LANES]], axis=1).astype(BF16)
        kv = jnp.dot(kin, wkv_ref[...], preferred_element_type=F32)
        k_ref[rows, :] = _rotate_slots(kv[:, :QK_COLS], cf, su, sd).astype(BF16)
        v_ref[rows, :] = kv[:, QK_COLS:].astype(BF16)


def _proj_call(xt, mod_a, mod_sh, mod_map, win, qg, wuq, kvg, wkv, rope_tabs, rope_map, tm, moe_in=None):
    T, D = xt.shape
    const = lambda i, *_: (0, 0)
    tok = lambda i, *_: (i, 0)
    mod = lambda i, *_: mod_map(i)
    rope = lambda i, *_: rope_map(i)
    out_cols = (HY_COLS, GM_COLS, CV_COLS, QK_COLS, QK_COLS, V_COLS)
    out_dtypes = (F32, F32, F32, BF16, BF16, BF16)
    in_specs = [
        pl.BlockSpec((None, 1, D), mod),
        pl.BlockSpec((None, 1, D), mod),
        pl.BlockSpec(win.shape, const),
        pl.BlockSpec(qg.shape, const),
        pl.BlockSpec(wuq.shape, const),
        pl.BlockSpec(kvg.shape, const),
        pl.BlockSpec(wkv.shape, const),
        pl.BlockSpec((tm, HEAD_SLOT), rope),
        pl.BlockSpec((tm, HEAD_SLOT), rope),
        pl.BlockSpec((tm, HEAD_SLOT), rope),
    ]
    out_specs = [pl.BlockSpec((tm, n), tok) for n in out_cols]
    out_shape = [jax.ShapeDtypeStruct((T, n), dt) for n, dt in zip(out_cols, out_dtypes)]
    shared = (mod_a, mod_sh, win, qg, wuq, kvg, wkv, *rope_tabs)
    stage = pltpu.VMEM((2, tm // 2, PROJ_COLS), F32)
    if moe_in is None:
        return pl.pallas_call(
            _proj_kernel,
            grid=(T // tm,),
            in_specs=[pl.BlockSpec((tm, D), tok)] + in_specs,
            out_specs=out_specs,
            out_shape=out_shape,
            scratch_shapes=[stage],
            compiler_params=_cparams(1),
            name="proj_qkv",
        )(xt, *shared)
    tok_row, g2, ys = moe_in
    grid_spec = pltpu.PrefetchScalarGridSpec(
        num_scalar_prefetch=1,
        grid=(T // tm,),
        in_specs=[pl.BlockSpec((tm, D), tok), pl.BlockSpec((None, 1, D), mod),
                  pl.BlockSpec(memory_space=pl.ANY)] + in_specs,
        out_specs=out_specs + [pl.BlockSpec((tm, D), tok)],
        scratch_shapes=[pltpu.VMEM((2, tm, ys.shape[1]), ys.dtype), pltpu.SemaphoreType.DMA((2,)), stage],
    )
    return pl.pallas_call(
        _proj_gather_kernel,
        grid_spec=grid_spec,
        out_shape=out_shape + [jax.ShapeDtypeStruct((T, D), F32)],
        compiler_params=_cparams(1),
        name="moe_residual_proj_qkv",
    )(tok_row, xt, g2, ys, *shared)


def _attend_heads(q_ref, key_refs, val_refs, o_ref, s_ref):
    nt = (((1,), (1,)), ((), ()))
    lane = lax.broadcasted_iota(jnp.int32, (1, V_COLS), 1)
    sub = min(ATT_SUB, q_ref.shape[0])
    units = [(r, h) for r in range(q_ref.shape[0] // sub) for h in range(MLA_HEADS)]
    col0 = [0]
    for k_ref in key_refs:
        col0.append(col0[-1] + k_ref.shape[0])

    def scores_into(u):
        r, h = units[u]
        q = q_ref[r * sub:(r + 1) * sub, h * HEAD_SLOT:(h + 1) * HEAD_SLOT]
        for j, k_ref in enumerate(key_refs):
            s_ref[u % n_buf, :, col0[j]:col0[j + 1]] = lax.dot_general(
                q, k_ref[:, h * HEAD_SLOT:(h + 1) * HEAD_SLOT], nt, preferred_element_type=F32)

    n_buf = s_ref.shape[0]
    for u in range(min(n_buf - 1, len(units))):
        scores_into(u)
    acc = None
    for u, (r, h) in enumerate(units):
        if u + n_buf - 1 < len(units):
            scores_into(u + n_buf - 1)
        s = s_ref[u % n_buf]
        m = jnp.max(s, axis=-1, keepdims=True)
        p = jnp.exp2(s - m)
        denom = jnp.sum(p, axis=-1, keepdims=True)
        p16 = p.astype(BF16)
        o = functools.reduce(jnp.add, [jnp.dot(p16[:, col0[j]:col0[j + 1]], v_ref[...], preferred_element_type=F32)
                                       for j, v_ref in enumerate(val_refs)])
        in_head = (lane >= h * MLA_V) & (lane < (h + 1) * MLA_V)
        part = jnp.where(in_head, o / denom, 0.0)
        acc = part if h == 0 else acc + part
        if h == MLA_HEADS - 1:
            o_ref[r * sub:(r + 1) * sub, :] = acc


def _attn_lat_kernel(q_ref, kl_ref, kc_ref, vl_ref, vc_ref, o_ref, s_ref):
    _attend_heads(q_ref, (kl_ref, kc_ref), (vl_ref, vc_ref), o_ref, s_ref)


def _attn_ctx_kernel(q_ref, k_ref, v_ref, o_ref, s_ref):
    _attend_heads(q_ref, (k_ref,), (v_ref,), o_ref, s_ref)


def _attn_lat_call(q, k, v, k_ctx, v_ctx, n_batch, n_lat, n_ctx):
    tq = ATT_TILE
    qt = n_lat // tq
    return pl.pallas_call(
        _attn_lat_kernel,
        grid=(n_batch, qt),
        in_specs=[
            pl.BlockSpec((tq, QK_COLS), lambda b, j: (b * qt + j, 0)),
            pl.BlockSpec((n_lat, QK_COLS), lambda b, j: (b, 0)),
            pl.BlockSpec((n_ctx, QK_COLS), lambda b, j: (b, 0)),
            pl.BlockSpec((n_lat, V_COLS), lambda b, j: (b, 0)),
            pl.BlockSpec((n_ctx, V_COLS), lambda b, j: (b, 0)),
        ],
        out_specs=pl.BlockSpec((tq, V_COLS), lambda b, j: (b * qt + j, 0)),
        out_shape=jax.ShapeDtypeStruct((n_batch * n_lat, V_COLS), F32),
        scratch_shapes=[pltpu.VMEM((ATT_SCORE_BUFS, min(ATT_SUB, tq), n_lat + n_ctx), F32)],
        compiler_params=_cparams(2),
        name="attn_latent",
    )(q, k, k_ctx, v, v_ctx)


def _attn_ctx_call(q, k, v, n_batch, n_ctx):
    blk = lambda b: (b, 0)
    return pl.pallas_call(
        _attn_ctx_kernel,
        grid=(n_batch,),
        in_specs=[pl.BlockSpec((n_ctx, QK_COLS), blk), pl.BlockSpec((n_ctx, QK_COLS), blk),
                  pl.BlockSpec((n_ctx, V_COLS), blk)],
        out_specs=pl.BlockSpec((n_ctx, V_COLS), blk),
        out_shape=jax.ShapeDtypeStruct((n_batch * n_ctx, V_COLS), F32),
        scratch_shapes=[pltpu.VMEM((ATT_SCORE_BUFS, min(ATT_SUB, n_ctx), n_ctx), F32)],
        compiler_params=_cparams(1),
        name="attn_context",
    )(q, k, v)


@functools.lru_cache(maxsize=None)
def _dft_factor_tables_np(L):
    f = np.arange(L, dtype=np.int64)[:, None]
    def trig(t):
        ang = ((f * t[None, :]) % (2 * L)).astype(np.float64) * (np.pi / L)
        return np.cos(ang).astype(np.float32), np.sin(ang).astype(np.float32)
    return trig(np.arange(L // LANES, dtype=np.int64) * LANES) + trig(np.arange(LANES, dtype=np.int64))


def _dft_matrices(L):
    assert L % LANES == 0
    c1, s1, c0, s0 = (jnp.asarray(a) for a in _dft_factor_tables_np(L))
    c1, s1, c0, s0 = c1[:, :, None], s1[:, :, None], c0[:, None, :], s0[:, None, :]
    c = (c1 * c0 - s1 * s0).reshape(L, L)
    s = (s1 * c0 + c1 * s0).reshape(L, L)
    alt = jnp.where(jnp.arange(L) % 2 == 0, 1.0, -1.0).astype(F32)
    row0 = (jnp.arange(L) == 0)
    s_fwd = jnp.where(row0[:, None], alt[None, :], s)
    s_inv = jnp.where(row0[None, :], alt[:, None], s)
    return c.astype(BF16), s_fwd.astype(BF16), s_inv.astype(BF16)


def _half_dft(L):
    assert L % 2 == 0
    return _dft_matrices(L // 2) + tuple(jnp.asarray(t) for t in _twiddle_np(L // 2))


def _half_butterfly(ae, ao, be, bo, c, s):
    return (ae + c * ao - s * bo, be + c * bo + s * ao,
            ae - c * ao + s * bo, -be + c * bo + s * ao)


def _twiddle_np(M):
    ang = np.arange(M, dtype=np.float64)[:, None] * (np.pi / (2 * M))
    return np.cos(ang).astype(np.float32), np.sin(ang).astype(np.float32)


def _spectrum_kernel(c_ref, s_ref, k_ref, tc_ref, ts_ref, krl_ref, kil_ref, krh_ref, kih_ref, sp_ref):
    tr, M = c_ref.shape
    w = k_ref.shape[1] // 4
    inv_l = 1.0 / (2 * M)
    k = k_ref[...]
    a = jnp.dot(c_ref[...], k, preferred_element_type=F32)
    b = jnp.dot(s_ref[...], k, preferred_element_type=F32)
    g = pl.program_id(0) * tr + lax.broadcasted_iota(jnp.int32, (tr, 1), 0)
    sign = jnp.where(g % 2 == 0, 1.0, -1.0)
    tc, ts = tc_ref[...], ts_ref[...]
    cols = lambda x, j: x[:, j * w:(j + 1) * w]
    f_lo_a, f_lo_b, f_hi_a, f_hi_b = _half_butterfly(cols(a, 0), cols(a, 2), cols(b, 0), cols(b, 2), tc, ts)
    b_lo_a, b_lo_b, b_hi_a, b_hi_b = _half_butterfly(cols(a, 1), cols(a, 3), cols(b, 1), cols(b, 3), tc, ts)
    kr_lo = (f_lo_a + sign * b_lo_a) * inv_l
    kr_hi = (f_hi_a + sign * b_hi_a) * inv_l
    krl_ref[...] = kr_lo
    kil_ref[...] = -(f_lo_b + sign * b_lo_b) * inv_l
    krh_ref[...] = kr_hi
    kih_ref[...] = -(f_hi_b + sign * b_hi_b) * inv_l

    @pl.when(pl.program_id(0) == 0)
    def _():
        mid_sign = 1.0 if M % 2 == 0 else -1.0
        sp_ref[...] = jnp.zeros_like(sp_ref)
        sp_ref[0:1, :] = 0.5 * kr_lo[0:1, :]
        sp_ref[1:2, :] = 0.5 * kr_hi[0:1, :]
        sp_ref[2:3, :] = (cols(b, 0)[0:1, :] + mid_sign * cols(b, 1)[0:1, :]) * inv_l
        sp_ref[3:4, :] = -(cols(b, 2)[0:1, :] + mid_sign * cols(b, 3)[0:1, :]) * inv_l


def _spectrum_call(dft, filters):
    cmat, smat, _, tc, ts = dft
    M = cmat.shape[0]
    w2 = filters.shape[1]
    w = w2 // 2
    samples = filters.reshape(M, 2 * w2).astype(BF16)
    tr = min(DFT_ROWS, M)
    row = lambda i: (i, 0)
    return pl.pallas_call(
        _spectrum_kernel,
        grid=(M // tr,),
        in_specs=[pl.BlockSpec((tr, M), row), pl.BlockSpec((tr, M), row),
                  pl.BlockSpec((M, 2 * w2), lambda i: (0, 0), pipeline_mode=pl.Buffered(1)),
                  pl.BlockSpec((tr, 1), row), pl.BlockSpec((tr, 1), row)],
        out_specs=[pl.BlockSpec((tr, w), row)] * 4 + [pl.BlockSpec((SUBLANES, w), lambda i: (0, 0))],
        out_shape=[jax.ShapeDtypeStruct((M, w), F32)] * 4 + [jax.ShapeDtypeStruct((SUBLANES, w), F32)],
        compiler_params=_cparams(1),
        name="hyena_filter_spectrum",
    )(cmat, smat, samples, tc, ts)


def _hy_prep_kernel(p_ref, w_ref, b_ref, z_ref):
    M = z_ref.shape[1]
    pe = p_ref[pl.ds(0, M, stride=2), :]
    po = p_ref[pl.ds(1, M, stride=2), :]
    row = lax.broadcasted_iota(jnp.int32, (M, 1), 0)
    po_prev = jnp.where(row == 0, 0.0, pltpu.roll(po, 1, 0))
    pe_next = jnp.where(row == M - 1, 0.0, pltpu.roll(pe, M - 1, 0))
    w0, w1, w2, bias = w_ref[0:1, :], w_ref[1:2, :], w_ref[2:3, :], b_ref[...]
    ze = po_prev * w0 + pe * w1 + po * w2 + bias
    zo = pe * w0 + po * w1 + pe_next * w2 + bias
    z_ref[0] = ze.astype(z_ref.dtype)
    z_ref[1] = zo.astype(z_ref.dtype)


def _hy_prep_call(p_hy, short_w, short_b, n_batch, L):
    N = n_batch * D_GROUP
    M = L // 2
    hb = D_GROUP // LANES
    return pl.pallas_call(
        _hy_prep_kernel,
        grid=(n_batch, (HY_ORDER + 1) * hb),
        in_specs=[
            pl.BlockSpec((L, LANES), lambda b, j: (b, j)),
            pl.BlockSpec((HY_SHORT, LANES), lambda b, j: (0, j)),
            pl.BlockSpec((1, LANES), lambda b, j: (0, j)),
        ],
        out_specs=pl.BlockSpec((None, 2, M, LANES), lambda b, j: (j // hb, 0, 0, b * hb + j % hb)),
        out_shape=jax.ShapeDtypeStruct((HY_ORDER + 1, 2, M, N), BF16),
        compiler_params=_cparams(2),
        name="hyena_short_conv",
    )(p_hy, short_w, short_b[None, :])


def _dft_fwd_kernel(c_ref, s_ref, ue_ref, uo_ref, tc_ref, ts_ref, krl_ref, kil_ref, krh_ref, kih_ref, sp_ref,
                    e_ref):
    tr = c_ref.shape[0]
    c, s = c_ref[...], s_ref[...]
    ue, uo = ue_ref[...], uo_ref[...]
    ae = jnp.dot(c, ue, preferred_element_type=F32)
    ao = jnp.dot(c, uo, preferred_element_type=F32)
    be = jnp.dot(s, ue, preferred_element_type=F32)
    bo = jnp.dot(s, uo, preferred_element_type=F32)
    tc, ts = tc_ref[...], ts_ref[...]
    first = (pl.program_id(1) * tr + lax.broadcasted_iota(jnp.int32, (tr, 1), 0)) == 0
    krl, kil, krh, kih = krl_ref[...], kil_ref[...], krh_ref[...], kih_ref[...]
    k0h, kLh, krm, kim = sp_ref[0:1, :], sp_ref[1:2, :], sp_ref[2:3, :], sp_ref[3:4, :]
    for g in range(ue.shape[1] // D_GROUP):
        sl = slice(g * D_GROUP, (g + 1) * D_GROUP)
        a1, b1, a2, b2 = _half_butterfly(ae[:, sl], ao[:, sl], be[:, sl], bo[:, sl], tc, ts)
        p1, q1 = krl * a1 + kil * b1, krl * b1 - kil * a1
        p2, q2 = krh * a2 + kih * b2, krh * b2 - kih * a2
        pm, qp = p1 - p2, q1 + q2
        dc, ny = k0h * a1, kLh * a2
        alt_e, alt_o = be[:, sl], bo[:, sl]
        e_ref[0, :, sl] = jnp.where(first, dc + ny, p1 + p2).astype(BF16)
        e_ref[1, :, sl] = jnp.where(first, krm * alt_e + kim * alt_o, q1 - q2).astype(BF16)
        e_ref[2, :, sl] = jnp.where(first, dc - ny, tc * pm + ts * qp).astype(BF16)
        e_ref[3, :, sl] = jnp.where(first, krm * alt_o - kim * alt_e, tc * qp - ts * pm).astype(BF16)


def _dft_fwd_call(dft, u_arr, u_sel, tabs, order):
    cmat, smat, _, tc, ts = dft
    _, _, M, N = u_arr.shape
    tr = min(DFT_ROWS, M)
    tn = min(DFT_COLS, N)
    row = lambda h, i: (i, 0)
    tab = lambda h, i: (i, order)
    u16 = u_arr
    return pl.pallas_call(
        _dft_fwd_kernel,
        grid=(N // tn, M // tr),
        in_specs=[
            pl.BlockSpec((tr, M), row),
            pl.BlockSpec((tr, M), row),
            pl.BlockSpec((None, None, M, tn), lambda h, i: (u_sel, 0, 0, h), pipeline_mode=pl.Buffered(1)),
            pl.BlockSpec((None, None, M, tn), lambda h, i: (u_sel, 1, 0, h), pipeline_mode=pl.Buffered(1)),
            pl.BlockSpec((tr, 1), row), pl.BlockSpec((tr, 1), row),
        ] + [pl.BlockSpec((tr, D_GROUP), tab)] * 4 + [pl.BlockSpec((SUBLANES, D_GROUP), lambda h, i: (0, order))],
        out_specs=pl.BlockSpec((4, tr, tn), lambda h, i: (0, i, h)),
        out_shape=jax.ShapeDtypeStruct((4, M, N), BF16),
        compiler_params=_cparams(2),
        name="hyena_dft_fwd",
    )(cmat, smat, u16, u16, tc, ts, *tabs)


def _dft_inv_kernel(c_ref, st_ref, ec_ref, es_ref, oc_ref, os_ref, u_ref, g_ref, bias_ref, y_ref):
    c, st = c_ref[...], st_ref[...]
    bias = bias_ref[...]
    conv_e = (jnp.dot(c, ec_ref[...], preferred_element_type=F32)
              + jnp.dot(st, es_ref[...], preferred_element_type=F32))
    conv_o = (jnp.dot(c, oc_ref[...], preferred_element_type=F32)
              + jnp.dot(st, os_ref[...], preferred_element_type=F32))
    for par, conv in enumerate((conv_e, conv_o)):
        y = g_ref[par].astype(F32) * (conv + u_ref[par].astype(F32) * bias)
        y_ref[par] = y.astype(y_ref.dtype)


def _dft_inv_call(dft, e16, u_arr, u_sel, g_arr, g_sel, bias_row, out_dtype):
    cmat, _, stmat, _, _ = dft
    _, M, N = e16.shape
    tr = min(DFT_ROWS, M)
    tn = min(DFT_COLS, N)
    row = lambda h, i: (i, 0)
    plane = lambda k: pl.BlockSpec((None, M, tn), lambda h, i: (k, 0, h), pipeline_mode=pl.Buffered(1))
    return pl.pallas_call(
        _dft_inv_kernel,
        grid=(N // tn, M // tr),
        in_specs=[
            pl.BlockSpec((tr, M), row),
            pl.BlockSpec((tr, M), row),
            plane(0), plane(1), plane(2), plane(3),
            pl.BlockSpec((None, 2, tr, tn), lambda h, i: (u_sel, 0, i, h)),
            pl.BlockSpec((None, 2, tr, tn), lambda h, i: (g_sel, 0, i, h)),
            pl.BlockSpec((1, tn), lambda h, i: (0, h)),
        ],
        out_specs=pl.BlockSpec((2, tr, tn), lambda h, i: (0, i, h)),
        out_shape=jax.ShapeDtypeStruct((2, M, N), out_dtype),
        compiler_params=_cparams(2),
        name="hyena_dft_inv",
    )(cmat, stmat, e16, e16, e16, e16, u_arr, g_arr, bias_row)


def _hyena_call(p_hy, short_w, short_b, filters, hy_bias, n_batch, L, dft):
    tabs = _spectrum_call(dft, filters)
    z = _hy_prep_call(p_hy, short_w, short_b, n_batch, L)
    u_arr, u_sel = z, HY_ORDER
    for n in range(HY_ORDER):
        e16 = _dft_fwd_call(dft, u_arr, u_sel, tabs, n)
        bias_row = jnp.tile(hy_bias[n][None, :].astype(F32), (1, n_batch))
        y = _dft_inv_call(dft, e16, u_arr, u_sel, z, n, bias_row, BF16 if n + 1 < HY_ORDER else F32)
        u_arr, u_sel = y[None], 0
    return y


def _mixers_kernel(gm_ref, cv_ref, cvp_ref, cvn_ref, lng_ref, lnb_ref, ws_ref, bsf_ref, dww_ref, dwb_ref,
                   cg_ref, cb_ref, avg_ref, ygm_ref, ycv_ref, glu_ref, *, tiles_per_seq):
    i = pl.program_id(0)
    tm = gm_ref.shape[0]
    lane = lax.broadcasted_iota(jnp.int32, (1, D_GROUP), 1)

    z = jax.nn.gelu(gm_ref[...], approximate=True)
    u, v = z[:, :D_GROUP], z[:, D_GROUP:]
    mu = jnp.mean(v, axis=-1, keepdims=True)
    vc = v - mu
    var = jnp.mean(vc * vc, axis=-1, keepdims=True)
    vn = (vc * lax.rsqrt(var + EPS) * lng_ref[...] + lnb_ref[...]).astype(BF16)
    hd = D_GROUP // GM_HEADS
    for c in range(tm // GM_CHUNK):
        rows = slice(c * GM_CHUNK, (c + 1) * GM_CHUNK)
        s = bsf_ref[...]
        for g in range(GM_HEADS):
            sg = jnp.dot(ws_ref[g], vn[rows, :], preferred_element_type=F32)
            s = s + jnp.where((lane >= g * hd) & (lane < (g + 1) * hd), sg, 0.0)
        ygm_ref[rows, :] = u[rows, :] * s

    def glu(t):
        return t[:, :D_GROUP] * jax.nn.sigmoid(t[:, D_GROUP:])

    first = (i % tiles_per_seq) == 0
    last = (i % tiles_per_seq) == tiles_per_seq - 1
    span = tm + 2 * CONV_HALO
    glu_ref[0, 0:CONV_HALO, :] = jnp.where(first, 0.0, glu(cvp_ref[...]))
    glu_ref[0, CONV_HALO:CONV_HALO + tm, :] = glu(cv_ref[...])
    glu_ref[0, CONV_HALO + tm:span, :] = jnp.where(last, 0.0, glu(cvn_ref[...]))
    glu_ref[0, span:, :] = jnp.zeros((SUBLANES, D_GROUP), F32)
    for b in range(1, SUBLANES):
        glu_ref[b, 0:span, :] = glu_ref[0, b:b + span, :]
    pad = (CV_WIDTH - 1) // 2
    rc = 128

    def group_mean(t):
        hi = t.astype(BF16)
        lo = (t - hi.astype(F32)).astype(BF16)
        return (jnp.dot(hi, avg_ref[...], preferred_element_type=F32)
                + jnp.dot(lo, avg_ref[...], preferred_element_type=F32))

    for c in range(tm // rc):
        acc = jnp.zeros((rc, D_GROUP), F32) + dwb_ref[...]
        for k in range(CV_WIDTH):
            start = c * rc + CONV_HALO - pad + k
            b = start % SUBLANES
            acc = acc + glu_ref[b, start - b:start - b + rc, :] * dww_ref[k:k + 1, :]
        d = acc - group_mean(acc)
        gvar = group_mean(d * d)
        n = d * lax.rsqrt(gvar + EPS) * cg_ref[...] + cb_ref[...]
        ycv_ref[c * rc:(c + 1) * rc, :] = n * jax.nn.sigmoid(n)


def _mixers_call(p_gm, p_cv, gm_ln_g, gm_ln_b, gm_ws, gm_bs, cv_dw_w, cv_dw_b, cv_ln_g, cv_ln_b, L, tm):
    T = p_gm.shape[0]
    tps = L // tm
    hb = tm // CONV_HALO
    n_hblk = T // CONV_HALO
    const2 = lambda i: (0, 0)
    tok = lambda i: (i, 0)
    bs_full = jnp.repeat(gm_bs.T.astype(F32), D_GROUP // GM_HEADS, axis=1)
    gid = np.arange(D_GROUP) // (D_GROUP // CV_GROUPS)
    avg = jnp.asarray((gid[:, None] == gid[None, :]).astype(np.float32) / (D_GROUP // CV_GROUPS)).astype(BF16)
    row = lambda a: a[None, :].astype(F32)
    return pl.pallas_call(
        functools.partial(_mixers_kernel, tiles_per_seq=tps),
        grid=(T // tm,),
        in_specs=[
            pl.BlockSpec((tm, GM_COLS), tok),
            pl.BlockSpec((tm, CV_COLS), tok),
            pl.BlockSpec((CONV_HALO, CV_COLS), lambda i: (jnp.maximum(i * hb - 1, 0), 0)),
            pl.BlockSpec((CONV_HALO, CV_COLS), lambda i: (jnp.minimum((i + 1) * hb, n_hblk - 1), 0)),
            pl.BlockSpec((1, D_GROUP), const2),
            pl.BlockSpec((1, D_GROUP), const2),
            pl.BlockSpec((GM_HEADS, GM_CHUNK, GM_CHUNK), lambda i: (0, 0, 0)),
            pl.BlockSpec((GM_CHUNK, D_GROUP), const2),
            pl.BlockSpec((CV_WIDTH, D_GROUP), const2),
            pl.BlockSpec((1, D_GROUP), const2),
            pl.BlockSpec((1, D_GROUP), const2),
            pl.BlockSpec((1, D_GROUP), const2),
            pl.BlockSpec((D_GROUP, D_GROUP), const2),
        ],
        out_specs=[pl.BlockSpec((tm, D_GROUP), tok)] * 2,
        out_shape=[jax.ShapeDtypeStruct((T, D_GROUP), F32)] * 2,
        scratch_shapes=[pltpu.VMEM((SUBLANES, tm + 2 * CONV_HALO + SUBLANES, D_GROUP), F32)],
        compiler_params=_cparams(1),
        name="gmlp_conv_mixers",
    )(p_gm, p_cv, p_cv, p_cv, row(gm_ln_g), row(gm_ln_b), gm_ws.astype(BF16), bs_full, cv_dw_w.astype(F32),
      row(cv_dw_b), row(cv_ln_g), row(cv_ln_b), avg)


def _out_kernel(hye_ref, hyo_ref, gm_ref, cv_ref, at_ref, x_ref, mg_ref, wo_ref, g1_ref, a2_ref, sh2_ref, wrh_ref,
                wrl_ref, rb_ref, tri_ref, cin_ref, xm_ref, hfx_ref, cls_ref, rank_ref, cnt_ref, carry_ref, hy_ref):
    i = pl.program_id(0)
    tm = x_ref.shape[0]
    for k in range(D_GROUP // LANES):
        hy_ref[k, pl.ds(0, tm // 2, stride=2), :] = hye_ref[:, k * LANES:(k + 1) * LANES]
        hy_ref[k, pl.ds(1, tm // 2, stride=2), :] = hyo_ref[:, k * LANES:(k + 1) * LANES]
    y_hy = jnp.concatenate([hy_ref[k] for k in range(D_GROUP // LANES)], axis=1)
    o = None
    for g, y_src in enumerate((y_hy, gm_ref, cv_ref, at_ref)):
        y = y_src[...]
        n = y * lax.rsqrt(jnp.mean(y * y, axis=-1, keepdims=True) + EPS) * mg_ref[:, g * D_GROUP:(g + 1) * D_GROUP]
        part = jnp.dot(n.astype(BF16), wo_ref[g * D_GROUP:(g + 1) * D_GROUP, :], preferred_element_type=F32)
        o = part if o is None else o + part
    xm = x_ref[...] + g1_ref[...] * o
    xm_ref[...] = xm
    hf = xm * lax.rsqrt(jnp.mean(xm * xm, axis=-1, keepdims=True) + EPS) * a2_ref[...] + sh2_ref[...]

    hf_hi = hf.astype(BF16)
    hf_lo = (hf - hf_hi.astype(F32)).astype(BF16)
    logits = (jnp.dot(hf_hi, wrh_ref[...], preferred_element_type=F32)
              + jnp.dot(hf_lo, wrh_ref[...], preferred_element_type=F32)
              + jnp.dot(hf_hi, wrl_ref[...], preferred_element_type=F32))
    cls, g_lo, g_hi = _route_top2(jnp.transpose(logits), rb_ref[...])
    cls_ref[...] = cls

    d_model = hf.shape[1]
    hfx_ref[:, :d_model] = hf
    gate_rows = jnp.concatenate([g_lo, g_hi, jnp.zeros((LANES - TOP_K, tm), F32)], axis=0)
    hfx_ref[:, d_model:] = jnp.transpose(gate_rows)

    @pl.when(i == 0)
    def _():
        carry_ref[...] = cin_ref[...]

    sub = lax.broadcasted_iota(jnp.int32, (CLASS_ROWS, tm), 0)
    onehot = sub == cls
    prefix = jnp.dot(onehot.astype(BF16), tri_ref[...], preferred_element_type=F32)
    carry = carry_ref[...]
    rank = jnp.sum(jnp.where(onehot, prefix + carry[:, 0:1], 0.0), axis=0, keepdims=True)
    rank_ref[...] = rank.astype(jnp.int32)
    carry = carry + jnp.sum(onehot.astype(F32), axis=1, keepdims=True)
    carry_ref[...] = carry
    cnt_ref[...] = carry


def _out_alias_kernel(*refs):
    _out_kernel(*refs[1:])


def _out_call(y_hy_t, y_gm, y_cv, y_at, xt, mixg, wo, g1, a2, sh2, mod_map, wr_hi, wr_lo, rbias, counts_in, L, tm,
              hfx_prev, t_total, row0):
    T, D = xt.shape
    tps = L // tm
    blk0 = row0 // tm
    const = lambda i: (0, 0)
    tok = lambda i: (i, 0)
    lane_tok = lambda i: (0, i)
    tri = jnp.asarray(np.triu(np.ones((tm, tm), np.float32), k=1)).astype(BF16)
    W = D + LANES
    parity = lambda par: pl.BlockSpec((None, tm // 2, D_GROUP), lambda i: (par, i % tps, i // tps))
    in_specs = [parity(0), parity(1)] + [pl.BlockSpec((tm, D_GROUP), tok)] * 3 + [
        pl.BlockSpec((tm, D), tok),
        pl.BlockSpec(mixg.shape, const),
        pl.BlockSpec(wo.shape, const),
        pl.BlockSpec((None, 1, D), mod_map),
        pl.BlockSpec((None, 1, D), mod_map),
        pl.BlockSpec((None, 1, D), mod_map),
        pl.BlockSpec(wr_hi.shape, const),
        pl.BlockSpec(wr_lo.shape, const),
        pl.BlockSpec(rbias.shape, const),
        pl.BlockSpec((tm, tm), const),
        pl.BlockSpec((CLASS_ROWS, LANES), const),
    ]
    args = (y_hy_t, y_hy_t, y_gm, y_cv, y_at, xt, mixg, wo, g1, a2, sh2, wr_hi, wr_lo, rbias, tri, counts_in)
    aliased = hfx_prev is not None
    return pl.pallas_call(
        _out_alias_kernel if aliased else _out_kernel,
        grid=(T // tm,),
        in_specs=([pl.BlockSpec(memory_space=pl.ANY)] if aliased else []) + in_specs,
        out_specs=[pl.BlockSpec((tm, D), tok), pl.BlockSpec((tm, W), lambda i: (blk0 + i, 0)),
                   pl.BlockSpec((1, tm), lane_tok), pl.BlockSpec((1, tm), lane_tok),
                   pl.BlockSpec((CLASS_ROWS, LANES), const)],
        out_shape=[jax.ShapeDtypeStruct((T, D), F32), jax.ShapeDtypeStruct((t_total, W), F32),
                   jax.ShapeDtypeStruct((1, T), jnp.int32), jax.ShapeDtypeStruct((1, T), jnp.int32),
                   jax.ShapeDtypeStruct((CLASS_ROWS, LANES), F32)],
        scratch_shapes=[pltpu.VMEM((CLASS_ROWS, LANES), F32), pltpu.VMEM((D_GROUP // LANES, tm, LANES), F32)],
        input_output_aliases={0: 1} if aliased else {},
        compiler_params=_cparams(1),
        name="mix_out_norm2_route",
    )(*(((hfx_prev,) if aliased else ()) + args))


def _first_max_flags(vals):
    m = functools.reduce(jnp.maximum, vals)
    flags, taken = [], None
    for v in vals:
        f = v >= m
        if taken is not None:
            f = f & jnp.logical_not(taken)
        flags.append(f)
        taken = f if taken is None else taken | f
    return flags, m


def _pick(flags, vals):
    out = vals[-1]
    for f, v in zip(flags[-2::-1], vals[-2::-1]):
        out = jnp.where(f, v, out)
    return out


def _route_top2(lt, bias_col):
    s_all = jax.nn.sigmoid(lt[:N_EXPERTS, :])
    sel_all = s_all + bias_col
    s = [s_all[e:e + 1, :] for e in range(N_EXPERTS)]
    sel = [sel_all[e:e + 1, :] for e in range(N_EXPERTS)]
    neg = -jnp.inf
    E = EXPERTS_PER_GROUP

    def top2(vals):
        f1, m1 = _first_max_flags(vals)
        rest = [jnp.where(f, neg, v) for f, v in zip(f1, vals)]
        f2, m2 = _first_max_flags(rest)
        return f1, m1, f2, m2

    scores = []
    for g in range(N_EXPERT_GROUPS):
        _, m1, _, m2 = top2(sel[g * E:(g + 1) * E])
        scores.append(m1 + m2)
    gflags, _ = _first_max_flags(scores)
    bsel = [_pick(gflags, [sel[g * E + j] for g in range(N_EXPERT_GROUPS)]) for j in range(E)]
    bs = [_pick(gflags, [s[g * E + j] for g in range(N_EXPERT_GROUPS)]) for j in range(E)]
    f1, _, f2, _ = top2(bsel)
    zero = jnp.zeros_like(bs[0])
    w1 = functools.reduce(jnp.add, [jnp.where(f, v, zero) for f, v in zip(f1, bs)])
    w2 = functools.reduce(jnp.add, [jnp.where(f, v, zero) for f, v in zip(f2, bs)])
    izero = jnp.zeros(w1.shape, jnp.int32)
    j1 = functools.reduce(jnp.add, [jnp.where(f, j, izero) for j, f in enumerate(f1)])
    j2 = functools.reduce(jnp.add, [jnp.where(f, j, izero) for j, f in enumerate(f2)])
    gi = functools.reduce(jnp.add, [jnp.where(f, g, izero) for g, f in enumerate(gflags)])
    lo, hi = jnp.minimum(j1, j2), jnp.maximum(j1, j2)
    pair = jnp.where(lo == 0, 0, jnp.where(lo == 1, 3, 5)) + hi - lo - 1
    tot = w1 + w2
    first_is_lo = j1 < j2
    return gi * N_PAIRS + pair, jnp.where(first_is_lo, w1, w2) / tot, jnp.where(first_is_lo, w2, w1) / tot


def _table_lookup(table, idx):
    n = table.shape[0]
    hit = idx[..., None] == jnp.arange(n, dtype=jnp.int32)
    return jnp.sum(jnp.where(hit, table, 0), axis=-1)


def _class_plan(cls, rank, counts, tile):
    T = cls.shape[0]
    n_tiles = (T + N_CLASSES * tile) // tile
    order = jnp.argsort(cls, stable=True).astype(jnp.int32)
    padded = ((counts + tile - 1) // tile) * tile
    seg_end = jnp.cumsum(padded)
    seg_start = seg_end - padded
    src_start = jnp.cumsum(counts) - counts
    tok_row = _table_lookup(seg_start, cls) + rank

    tile_first = jnp.arange(n_tiles, dtype=jnp.int32) * tile
    tile_class = jnp.minimum(jnp.sum((seg_end[None, :] <= tile_first[:, None]).astype(jnp.int32), axis=1),
                             N_CLASSES - 1)
    off = tile_first - _table_lookup(seg_start, tile_class)
    tile_cnt = jnp.clip(_table_lookup(counts, tile_class) - off, 0, tile)
    tile_src = jnp.clip(_table_lookup(src_start, tile_class) + off, 0, T - 1)
    grp = tile_class // N_PAIRS
    pair = tile_class % N_PAIRS
    tile_lo = grp * EXPERTS_PER_GROUP + _table_lookup(jnp.asarray(PAIR_LO, jnp.int32), pair)
    tile_hi = grp * EXPERTS_PER_GROUP + _table_lookup(jnp.asarray(PAIR_HI, jnp.int32), pair)
    return order, tile_lo, tile_hi, tile_src, tile_cnt, tok_row


def _moe_pair_kernel(order_ref, lo_ref, hi_ref, src_ref, cnt_ref, hfx_hbm, wga_ref, wua_ref, wda_ref, wgb_ref,
                     wub_ref, wdb_ref, o_ref, buf_ref, sem_ref):
    i = pl.program_id(0)
    n = pl.num_programs(0)
    tile = o_ref.shape[0]
    n_tok = order_ref.shape[0]
    slot = i % 2

    def start(t, s):
        base = src_ref[t]
        _start_row_gather(hfx_hbm, lambda j: order_ref[jnp.minimum(base + j, n_tok - 1)], tile, buf_ref, s, sem_ref)

    @pl.when((i == 0) & (cnt_ref[0] > 0))
    def _():
        start(0, 0)

    @pl.when((cnt_ref[i] <= 0) & (i > 0) & (cnt_ref[jnp.maximum(i - 1, 0)] > 0))
    def _():
        _wait_row_gather(hfx_hbm, tile, buf_ref, slot, sem_ref)

    @pl.when(cnt_ref[i] > 0)
    def _():
        start(jnp.minimum(i + 1, n - 1), 1 - slot)
        _wait_row_gather(hfx_hbm, tile, buf_ref, slot, sem_ref)
        d_model = buf_ref.shape[2] - LANES
        x = buf_ref[slot, :, :d_model].astype(BF16)
        live = lax.broadcasted_iota(jnp.int32, (tile, 1), 0) < cnt_ref[i]
        gates = buf_ref[slot, :, d_model:]
        g_lo = jnp.where(live, gates[:, 0:1], 0.0)
        g_hi = jnp.where(live, gates[:, 1:2], 0.0)

        def hidden(wg_ref, wu_ref, gate):
            hg = jnp.dot(x, wg_ref[...], preferred_element_type=F32)
            hu = jnp.dot(x, wu_ref[...], preferred_element_type=F32)
            return (hg * jax.nn.sigmoid(hg) * hu * gate).astype(BF16)

        o_ref[...] = (jnp.dot(hidden(wga_ref, wua_ref, g_lo), wda_ref[...], preferred_element_type=F32)
                      + jnp.dot(hidden(wgb_ref, wub_ref, g_hi), wdb_ref[...], preferred_element_type=F32))

    @pl.when(cnt_ref[i] <= 0)
    def _():
        o_ref[...] = jnp.zeros_like(o_ref)


def _moe_pair_call(hfx, order, tile_lo, tile_hi, tile_src, tile_cnt, wg, wu, wd, tile):
    T, W = hfx.shape
    D = W - LANES
    F = wg.shape[-1]
    n_tiles = tile_lo.shape[0]
    lo = lambda i, o, tl, th, ts, tc: (tl[i], 0, 0)
    hi = lambda i, o, tl, th, ts, tc: (th[i], 0, 0)
    grid_spec = pltpu.PrefetchScalarGridSpec(
        num_scalar_prefetch=5,
        grid=(n_tiles,),
        in_specs=[
            pl.BlockSpec(memory_space=pl.ANY),
            pl.BlockSpec((None, D, F), lo), pl.BlockSpec((None, D, F), lo), pl.BlockSpec((None, F, D), lo),
            pl.BlockSpec((None, D, F), hi), pl.BlockSpec((None, D, F), hi), pl.BlockSpec((None, F, D), hi),
        ],
        out_specs=pl.BlockSpec((tile, D), lambda i, o, tl, th, ts, tc: (i, 0)),
        scratch_shapes=[pltpu.VMEM((2, tile, W), F32), pltpu.SemaphoreType.DMA((2,))],
    )
    return pl.pallas_call(
        _moe_pair_kernel,
        grid_spec=grid_spec,
        out_shape=jax.ShapeDtypeStruct((n_tiles * tile, D), F32),
        compiler_params=_cparams(1),
        name="moe_pair_grouped",
    )(order, tile_lo, tile_hi, tile_src, tile_cnt, hfx, wg, wu, wd, wg, wu, wd)


def _final_kernel(idx_ref, xm_ref, g2_ref, ys_hbm, g_ref, o_ref, buf_ref, sem_ref):
    x = xm_ref[...] + g2_ref[...] * _gathered_rows(idx_ref, ys_hbm, buf_ref, sem_ref, xm_ref.shape[0])
    o_ref[...] = x * lax.rsqrt(jnp.mean(x * x, axis=-1, keepdims=True) + EPS) * g_ref[...]
    _drain_gathered_rows(ys_hbm, buf_ref, sem_ref, xm_ref.shape[0])


def _final_call(x_mid, tok_row, g2, mod_map, ys, g, tm):
    T, D = x_mid.shape
    tok = lambda i, *_: (i, 0)
    grid_spec = pltpu.PrefetchScalarGridSpec(
        num_scalar_prefetch=1,
        grid=(T // tm,),
        in_specs=[pl.BlockSpec((tm, D), tok), pl.BlockSpec((None, 1, D), lambda i, *_: mod_map(i)),
                  pl.BlockSpec(memory_space=pl.ANY), pl.BlockSpec((1, D), lambda i, *_: (0, 0))],
        out_specs=pl.BlockSpec((tm, D), tok),
        scratch_shapes=[pltpu.VMEM((2, tm, ys.shape[1]), ys.dtype), pltpu.SemaphoreType.DMA((2,))],
    )
    return pl.pallas_call(
        _final_kernel,
        grid_spec=grid_spec,
        out_shape=jax.ShapeDtypeStruct((T, D), F32),
        compiler_params=_cparams(1),
        name="moe_residual_final_norm",
    )(tok_row, x_mid, g2, ys, g[None, :].astype(F32))


def kernel(x, c, ctx, c_ctx, ada_w, ada_b, norm1_g, norm2_g, w_in, hy_short_w, hy_short_b, hy_f_w1, hy_f_b1, hy_f_freq, hy_f_w2, hy_f_b2, hy_f_w3, hy_bias, gm_ln_g, gm_ln_b, gm_ws, gm_bs, cv_dw_w, cv_dw_b, cv_ln_g, cv_ln_b, mla_qa_norm, w_uq, mla_kva_norm, w_ukv, mix_norm_g, w_out, w_router, router_bias, exp_w_gate, exp_w_up, exp_w_down, final_norm_g):
    B, n_lat, D = x.shape
    n_ctx = ctx.shape[1]
    T_lat, T_ctx = B * n_lat, B * n_ctx
    tm_lat = min(512, n_lat)
    tm_ctx = min(256, n_ctx)
    assert n_lat % tm_lat == 0 and n_ctx % tm_ctx == 0 and n_lat % ATT_TILE == 0
    assert tm_lat % GM_CHUNK == 0 and tm_ctx % GM_CHUNK == 0 and T_lat % MOE_TILE == 0 and T_ctx % MOE_TILE == 0

    tps_lat, tps_ctx = n_lat // tm_lat, n_ctx // tm_ctx
    dft = {L: _half_dft(L) for L in {n_lat, n_ctx}}
    rope_lat = _rope_slot_tables(n_lat)
    rope_ctx = _identity_slot_tables(tm_ctx)
    wr_pad = jnp.pad(w_router.astype(F32), ((0, 0), (0, LANES - N_EXPERTS)))
    wr_hi = wr_pad.astype(BF16)
    wr_lo = (wr_pad - wr_hi.astype(F32)).astype(BF16)
    rbias = router_bias.astype(F32)[:, None]
    cond = jnp.concatenate([c, c_ctx[None, :]], axis=0)
    streams = {
        'lat': dict(x=x.reshape(T_lat, D), L=n_lat, tm=tm_lat, mod_map=lambda i: (i // tps_lat, 0, 0),
                    rope=rope_lat, rope_map=lambda i: (i % tps_lat, 0), row0=0, moe_in=None),
        'ctx': dict(x=ctx.reshape(T_ctx, D), L=n_ctx, tm=tm_ctx, mod_map=lambda i: (B, 0, 0),
                    rope=rope_ctx, rope_map=lambda i: (0, 0), row0=T_lat, moe_in=None),
    }

    for l in range(DEPTH):
        last = l == DEPTH - 1
        m = jax.nn.silu(cond) @ ada_w[l] + ada_b[l]
        sh1, sc1, g1, sh2, sc2, g2 = [t[:, None, :] for t in jnp.split(m, 6, axis=-1)]
        a1 = norm1_g[l][None, None, :] * (1.0 + sc1)
        a2 = norm2_g[l][None, None, :] * (1.0 + sc2)
        qg = jnp.pad(mla_qa_norm[l], (0, PQ_PAD - MLA_Q_RANK))[None, :].astype(F32)
        kvg = mla_kva_norm[l][None, :].astype(F32)
        win, wuq, wkv = _pack_w_in(w_in[l]), _pack_w_uq(w_uq[l]), _pack_w_ukv(w_ukv[l])
        filt = (hy_f_w1[l], hy_f_b1[l], hy_f_freq[l], hy_f_w2[l], hy_f_b2[l], hy_f_w3[l])
        wg, wu, wd = exp_w_gate[l].astype(BF16), exp_w_up[l].astype(BF16), exp_w_down[l].astype(BF16)

        proj = {name: _proj_call(s['x'], a1, sh1, s['mod_map'], win, qg, wuq, kvg, wkv, s['rope'], s['rope_map'],
                                 s['tm'], s['moe_in']) for name, s in streams.items()}
        for name, s in streams.items():
            if s['moe_in'] is not None:
                s['x'] = proj[name][6]
        k_ctx, v_ctx = proj['ctx'][4], proj['ctx'][5]

        active = ('lat',) if last else ('lat', 'ctx')
        x_mid, cls, rank = {}, {}, {}
        t_moe = T_lat if last else T_lat + T_ctx
        hfx = None if last else jnp.zeros((t_moe, D + LANES), F32)
        counts = jnp.zeros((CLASS_ROWS, LANES), F32)
        for name in active:
            s = streams[name]
            p_hy, p_gm, p_cv, q, k, v = proj[name][:6]
            if name == 'lat':
                y_at = _attn_lat_call(q, k, v, k_ctx, v_ctx, B, n_lat, n_ctx)
            else:
                y_at = _attn_ctx_call(q, k, v, B, n_ctx)
            filters = _hyena_filters(s['L'], *filt)
            y_hy_t = _hyena_call(p_hy, hy_short_w[l], hy_short_b[l], filters, hy_bias[l], B, s['L'], dft[s['L']])
            y_gm, y_cv = _mixers_call(p_gm, p_cv, gm_ln_g[l], gm_ln_b[l], gm_ws[l], gm_bs[l], cv_dw_w[l],
                                      cv_dw_b[l], cv_ln_g[l], cv_ln_b[l], s['L'], s['tm'])
            x_mid[name], hfx, cls[name], rank[name], counts = _out_call(
                y_hy_t, y_gm, y_cv, y_at, s['x'], mix_norm_g[l][None, :].astype(F32), w_out[l].astype(BF16),
                g1, a2, sh2, s['mod_map'], wr_hi, wr_lo, rbias, counts, s['L'], s['tm'],
                hfx, t_moe, s['row0'])

        cat = lambda d: jnp.concatenate([d[name][0] for name in active], axis=0)
        order, tile_lo, tile_hi, tile_src, tile_cnt, tok_row = _class_plan(
            cat(cls), cat(rank), counts[:N_CLASSES, 0].astype(jnp.int32), MOE_TILE)
        ys = _moe_pair_call(hfx, order, tile_lo, tile_hi, tile_src, tile_cnt, wg, wu, wd, MOE_TILE)
        for name in active:
            s = streams[name]
            s['x'] = x_mid[name]
            s['moe_in'] = (lax.slice_in_dim(tok_row, s['row0'], s['row0'] + x_mid[name].shape[0]), g2, ys)

    s = streams['lat']
    tok_row, g2, ys = s['moe_in']
    return _final_call(s['x'], tok_row, g2, s['mod_map'], ys, final_norm_g, s['tm']).reshape(B, n_lat, D)
```
